```python
import math
import jax
import jax.numpy as jnp
from jax import lax
import numpy as np

D_MODEL = 1024
BATCH = 8
SEQ = 2048
DEPTH = 4

CTX_LEN = 256
GRID_W = 64

HEAD_DIM = 64
N_Q_HEADS = 8
N_KV_HEADS = 2
Q_PER_KV = N_Q_HEADS // N_KV_HEADS
ATTN_WIDTH = N_Q_HEADS * HEAD_DIM
KV_WIDTH = N_KV_HEADS * HEAD_DIM
POOL_WINDOWS = (2, 4, 8, 16)
N_POOL_GROUPS = len(POOL_WINDOWS)
POOL_WIDTH = D_MODEL - ATTN_WIDTH
POOL_GROUP_DIM = POOL_WIDTH // N_POOL_GROUPS
IN_WIDTH = ATTN_WIDTH + 2 * KV_WIDTH + POOL_WIDTH
MIX_WIDTH = ATTN_WIDTH + POOL_WIDTH
WINDOW = 128
Q_BLOCK = 128
ROPE_BASE = 10000.0
ROPE_FREQS = HEAD_DIM // 4

D_FF_DENSE = 2816
N_EXPERTS = 8
TOP_K = 2
D_FF_EXPERT = 3584
N_DENSE = (DEPTH + 1) // 2
N_MOE = DEPTH // 2

ALPHA = (2 * DEPTH) ** 0.25
BETA = (8 * DEPTH) ** -0.25
LN_EPS = 1e-6
NEG_INF = -1e30

kernel_name = "hybrid_pool_window_gqa_moe_dit"


def _layernorm(x):
    xf = x.astype(jnp.float32)
    mu = jnp.mean(xf, axis=-1, keepdims=True)
    var = jnp.mean(jnp.square(xf - mu), axis=-1, keepdims=True)
    return ((xf - mu) * lax.rsqrt(var + LN_EPS)).astype(x.dtype)


def _post_norm(residual, update, gate, g, b):
    return _layernorm(ALPHA * residual + gate[:, None, :] * update) * g + b


def _modulate(x, shift, scale):
    return _layernorm(x) * (1.0 + scale[:, None, :]) + shift[:, None, :]


def _axial_rope_tables(n_tokens):
    rows = n_tokens // GRID_W
    row = jnp.repeat(jnp.arange(rows, dtype=jnp.float32), GRID_W)
    col = jnp.tile(jnp.arange(GRID_W, dtype=jnp.float32), rows)
    inv = ROPE_BASE ** (-jnp.arange(ROPE_FREQS, dtype=jnp.float32) / ROPE_FREQS)
    ang_r = row[:, None] * inv[None, :]
    ang_c = col[:, None] * inv[None, :]
    return (jnp.cos(ang_r), jnp.sin(ang_r), jnp.cos(ang_c), jnp.sin(ang_c))


def _rotate_half(x, cos, sin):
    x1, x2 = jnp.split(x, 2, axis=-1)
    cos = cos[None, :, None, :]
    sin = sin[None, :, None, :]
    return jnp.concatenate([x1 * cos - x2 * sin, x2 * cos + x1 * sin], axis=-1)


def _apply_axial_rope(x, tables):
    cos_r, sin_r, cos_c, sin_c = tables
    xr, xc = jnp.split(x.astype(jnp.float32), 2, axis=-1)
    out = jnp.concatenate([_rotate_half(xr, cos_r, sin_r), _rotate_half(xc, cos_c, sin_c)], axis=-1)
    return out.astype(x.dtype)


def _split_proj(p):
    b, n = p.shape[0], p.shape[1]
    q = p[..., :ATTN_WIDTH].reshape(b, n, N_Q_HEADS, HEAD_DIM)
    k = p[..., ATTN_WIDTH:ATTN_WIDTH + KV_WIDTH].reshape(b, n, N_KV_HEADS, HEAD_DIM)
    v = p[..., ATTN_WIDTH + KV_WIDTH:ATTN_WIDTH + 2 * KV_WIDTH].reshape(b, n, N_KV_HEADS, HEAD_DIM)
    u = p[..., ATTN_WIDTH + 2 * KV_WIDTH:]
    return q, k, v, u


def _latent_window_attention(q, k, v, k_ctx, v_ctx, sink):
    b, n_lat = q.shape[0], q.shape[1]
    nb = n_lat // Q_BLOCK
    n_ctx = k_ctx.shape[1]
    scale = HEAD_DIM ** -0.5
    qb = (q * scale).reshape(b, nb, Q_BLOCK, N_KV_HEADS, Q_PER_KV, HEAD_DIM)
    pad = ((0, 0), (WINDOW, WINDOW), (0, 0), (0, 0))
    kp = jnp.pad(k, pad).reshape(b, nb + 2, Q_BLOCK, N_KV_HEADS, HEAD_DIM)
    vp = jnp.pad(v, pad).reshape(b, nb + 2, Q_BLOCK, N_KV_HEADS, HEAD_DIM)
    kb = jnp.concatenate([kp[:, :-2], kp[:, 1:-1], kp[:, 2:]], axis=2)
    vb = jnp.concatenate([vp[:, :-2], vp[:, 1:-1], vp[:, 2:]], axis=2)
    s_band = jnp.einsum('bnqkgd,bnjkd->bnkgqj', qb, kb, preferred_element_type=jnp.float32)
    blk = jnp.arange(nb)[:, None, None]
    qi = jnp.arange(Q_BLOCK)[None, :, None]
    kj = jnp.arange(3 * Q_BLOCK)[None, None, :]
    pos_k = blk * Q_BLOCK - WINDOW + kj
    valid = (jnp.abs(kj - WINDOW - qi) <= WINDOW) & (pos_k >= 0) & (pos_k < n_lat)
    s_band = jnp.where(valid[None, :, None, None, :, :], s_band, NEG_INF)
    s_ctx = jnp.einsum('bnqkgd,bckd->bnkgqc', qb, k_ctx, preferred_element_type=jnp.float32)
    s_sink = jnp.broadcast_to(sink.astype(jnp.float32).reshape(1, 1, N_KV_HEADS, Q_PER_KV, 1, 1),
                              s_band.shape[:-1] + (1,))
    p = jax.nn.softmax(jnp.concatenate([s_band, s_ctx, s_sink], axis=-1), axis=-1)
    n_band = 3 * Q_BLOCK
    p_band = p[..., :n_band].astype(v.dtype)
    p_ctx = p[..., n_band:n_band + n_ctx].astype(v.dtype)
    out = (jnp.einsum('bnkgqj,bnjkd->bnqkgd', p_band, vb)
           + jnp.einsum('bnkgqc,bckd->bnqkgd', p_ctx, v_ctx))
    return out.reshape(b, n_lat, ATTN_WIDTH)


def _context_attention(q, k, v, sink):
    b, n_ctx = q.shape[0], q.shape[1]
    qg = (q * HEAD_DIM ** -0.5).reshape(b, n_ctx, N_KV_HEADS, Q_PER_KV, HEAD_DIM)
    s = jnp.einsum('bqkgd,bckd->bkgqc', qg, k, preferred_element_type=jnp.float32)
    s_sink = jnp.broadcast_to(sink.astype(jnp.float32).reshape(1, N_KV_HEADS, Q_PER_KV, 1, 1),
                              s.shape[:-1] + (1,))
    p = jax.nn.softmax(jnp.concatenate([s, s_sink], axis=-1), axis=-1)[..., :n_ctx].astype(v.dtype)
    out = jnp.einsum('bkgqc,bckd->bqkgd', p, v)
    return out.reshape(b, n_ctx, ATTN_WIDTH)


def _multiscale_pool(u, w_pool, pool_scale):
    b, n = u.shape[0], u.shape[1]
    ug = u.reshape(b, n, N_POOL_GROUPS, POOL_GROUP_DIM)
    cs = jnp.cumsum(ug.astype(jnp.float32), axis=1)
    cs = jnp.concatenate([jnp.zeros_like(cs[:, :1]), cs], axis=1)
    t = jnp.arange(n)
    pooled = []
    for g, w in enumerate(POOL_WINDOWS):
        lo = jnp.clip(t - w // 2, 0, n)
        hi = jnp.clip(t + w // 2, 0, n)
        cnt = (hi - lo).astype(jnp.float32)[None, :, None]
        pooled.append((cs[:, hi, g] - cs[:, lo, g]) / cnt)
    pooled = jnp.stack(pooled, axis=2).astype(u.dtype)
    mixed = jnp.einsum('bngd,gde->bnge', pooled - ug, w_pool)
    return mixed.reshape(b, n, POOL_WIDTH) * pool_scale


def _token_mixer(a_lat, a_ctx, w_in, w_pool, pool_scale, sink, w_out, rope, need_ctx_out):
    q, k, v, u = _split_proj(a_lat @ w_in)
    qc, kc, vc, uc = _split_proj(a_ctx @ w_in)
    q = _apply_axial_rope(q, rope)
    k = _apply_axial_rope(k, rope)
    attn = _latent_window_attention(q, k, v, kc, vc, sink)
    pool = _multiscale_pool(u, w_pool, pool_scale)
    y_lat = jnp.concatenate([attn, pool], axis=-1) @ w_out
    if not need_ctx_out:
        return y_lat, None
    attn_c = _context_attention(qc, kc, vc, sink)
    pool_c = _multiscale_pool(uc, w_pool, pool_scale)
    y_ctx = jnp.concatenate([attn_c, pool_c], axis=-1) @ w_out
    return y_lat, y_ctx


def _swiglu(h, w1, w3, w2):
    return (jax.nn.silu(h @ w1) * (h @ w3)) @ w2


def _moe_swiglu(h, router, w1, w3, w2):
    logits = jnp.einsum('bnd,de->bne', h, router, preferred_element_type=jnp.float32)
    top_vals, top_idx = lax.top_k(logits, TOP_K)
    top_w = jax.nn.softmax(top_vals, axis=-1)
    gates = jnp.sum(jax.nn.one_hot(top_idx, N_EXPERTS, dtype=jnp.float32) * top_w[..., None], axis=-2)
    gates = gates.astype(h.dtype)
    out = jnp.zeros_like(h)
    for e in range(N_EXPERTS):
        out = out + gates[..., e:e + 1] * _swiglu(h, w1[e], w3[e], w2[e])
    return out


def setup_inputs(seed: int = 0) -> dict:
    key = jax.random.key(seed)
    ks = jax.random.split(key, 24)
    nrm = jax.random.normal
    f32 = jnp.float32
    d = D_MODEL
    return {
        "x": nrm(ks[0], (BATCH, SEQ, d), f32),
        "c": nrm(ks[1], (BATCH, d), f32),
        "ctx": nrm(ks[2], (BATCH, CTX_LEN, d), f32),
        "c_ctx": nrm(ks[3], (d,), f32),
        "w_ada": nrm(ks[4], (DEPTH, d, 6 * d), f32) * (0.5 * d ** -0.5),
        "b_ada": nrm(ks[5], (DEPTH, 6 * d), f32) * 0.02,
        "w_in": nrm(ks[6], (DEPTH, d, IN_WIDTH), f32) * d ** -0.5,
        "w_pool": nrm(ks[7], (DEPTH, N_POOL_GROUPS, POOL_GROUP_DIM, POOL_GROUP_DIM), f32) * POOL_GROUP_DIM ** -0.5,
        "pool_scale": 1.0 + 0.02 * nrm(ks[8], (DEPTH, POOL_WIDTH), f32),
        "sink": 0.5 * nrm(ks[9], (DEPTH, N_Q_HEADS), f32),
        "w_out": nrm(ks[10], (DEPTH, MIX_WIDTH, d), f32) * (BETA * MIX_WIDTH ** -0.5),
        "ln_g": 1.0 + 0.02 * nrm(ks[11], (DEPTH, 2, d), f32),
        "ln_b": 0.02 * nrm(ks[12], (DEPTH, 2, d), f32),
        "dense_w1": nrm(ks[13], (N_DENSE, d, D_FF_DENSE), f32) * d ** -0.5,
        "dense_w3": nrm(ks[14], (N_DENSE, d, D_FF_DENSE), f32) * d ** -0.5,
        "dense_w2": nrm(ks[15], (N_DENSE, D_FF_DENSE, d), f32) * (BETA * D_FF_DENSE ** -0.5),
        "router": nrm(ks[16], (N_MOE, d, N_EXPERTS), f32) * d ** -0.5,
        "moe_w1": nrm(ks[17], (N_MOE, N_EXPERTS, d, D_FF_EXPERT), f32) * d ** -0.5,
        "moe_w3": nrm(ks[18], (N_MOE, N_EXPERTS, d, D_FF_EXPERT), f32) * d ** -0.5,
        "moe_w2": nrm(ks[19], (N_MOE, N_EXPERTS, D_FF_EXPERT, d), f32) * (BETA * D_FF_EXPERT ** -0.5),
    }


def reference(x, c, ctx, c_ctx, w_ada, b_ada, w_in, w_pool, pool_scale, sink, w_out, ln_g, ln_b,
              dense_w1, dense_w3, dense_w2, router, moe_w1, moe_w3, moe_w2):
    rope = _axial_rope_tables(x.shape[1])
    silu_c = jax.nn.silu(c)
    silu_cc = jax.nn.silu(c_ctx)[None, :]
    h, hc = x, ctx
    for l in range(DEPTH):
        need_ctx = l < DEPTH - 1
        mod = silu_c @ w_ada[l] + b_ada[l]
        mod_c = silu_cc @ w_ada[l] + b_ada[l]
        sh1, sc1, g1, sh2, sc2, g2 = jnp.split(mod, 6, axis=-1)
        csh1, csc1, cg1, csh2, csc2, cg2 = jnp.split(mod_c, 6, axis=-1)

        y_lat, y_ctx = _token_mixer(_modulate(h, sh1, sc1), _modulate(hc, csh1, csc1),
                                    w_in[l], w_pool[l], pool_scale[l], sink[l], w_out[l], rope, need_ctx)
        h = _post_norm(h, y_lat, g1, ln_g[l, 0], ln_b[l, 0])
        if need_ctx:
            hc = _post_norm(hc, y_ctx, cg1, ln_g[l, 0], ln_b[l, 0])

        a = _modulate(h, sh2, sc2)
        if l % 2 == 0:
            i = l // 2
            f_lat = _swiglu(a, dense_w1[i], dense_w3[i], dense_w2[i])
        else:
            i = l // 2
            f_lat = _moe_swiglu(a, router[i], moe_w1[i], moe_w3[i], moe_w2[i])
        h = _post_norm(h, f_lat, g2, ln_g[l, 1], ln_b[l, 1])
        if need_ctx:
            ac = _modulate(hc, csh2, csc2)
            if l % 2 == 0:
                f_ctx = _swiglu(ac, dense_w1[i], dense_w3[i], dense_w2[i])
            else:
                f_ctx = _moe_swiglu(ac, router[i], moe_w1[i], moe_w3[i], moe_w2[i])
            hc = _post_norm(hc, f_ctx, cg2, ln_g[l, 1], ln_b[l, 1])
    return h
```

```python
import functools

import jax
import jax.numpy as jnp
from jax import lax
from jax.experimental import pallas as pl
from jax.experimental.pallas import tpu as pltpu

F32 = jnp.float32
BF16 = jnp.bfloat16
I32 = jnp.int32

D = 1024
B = 8
SEQ = 2048
DEPTH = 4
CTX = 256
GRID_W = 64
HEAD_DIM = 64
N_Q_HEADS = 8
N_KV_HEADS = 2
ATTN_W = N_Q_HEADS * HEAD_DIM
KV_W = N_KV_HEADS * HEAD_DIM
POOL_WINDOWS = (2, 4, 8, 16)
POOL_W = D - ATTN_W
POOL_G = POOL_W // len(POOL_WINDOWS)
IN_W = ATTN_W + 2 * KV_W + POOL_W
WINDOW = 128
ROPE_BASE = 10000.0
ROPE_FREQS = HEAD_DIM // 4
D_FF_DENSE = 2816
N_EXPERTS = 8
D_FF_EXPERT = 3584
ALPHA = (2 * DEPTH) ** 0.25
LN_EPS = 1e-6
NEG_INF = -1e30

NL = B * SEQ
NC = B * CTX
NT = NL + NC
MOD_ROWS = 16
LANES = 128

TM = 512
TQ = 128
MOE_TM = 512
MOE_TF = 512
GATHER_ROWS = 512
COMBINE_ROWS = 256

VMEM_LIMIT = 56 * 1024 * 1024


def _cparams(n_axes, vmem=VMEM_LIMIT):
    return pltpu.CompilerParams(
        dimension_semantics=("arbitrary",) * n_axes, vmem_limit_bytes=vmem)


def _layernorm(x):
    mu = jnp.mean(x, axis=-1, keepdims=True)
    xc = x - mu
    var = jnp.mean(xc * xc, axis=-1, keepdims=True)
    return xc * lax.rsqrt(var + LN_EPS)


def _mod_row(tile, tiles_per_batch, n_latent_tiles):
    return jnp.where(tile < n_latent_tiles, tile // tiles_per_batch, B)


ADA_TN = 1536


def _ada_kernel(s_ref, w_ref, b_ref, o_ref):
    s = jax.nn.silu(s_ref[...]).astype(BF16)
    w = w_ref[...].astype(BF16)
    o_ref[...] = jnp.dot(s, w, preferred_element_type=F32) + b_ref[...]


def _ada_tables(cond, w_ada, b_ada):
    n_col = (6 * D) // ADA_TN
    out = pl.pallas_call(
        _ada_kernel,
        grid=(DEPTH, n_col),
        in_specs=[
            pl.BlockSpec((MOD_ROWS, D), lambda l, j: (0, 0)),
            pl.BlockSpec((None, D, ADA_TN), lambda l, j: (l, 0, j)),
            pl.BlockSpec((None, 1, ADA_TN), lambda l, j: (l, 0, j)),
        ],
        out_specs=pl.BlockSpec((None, MOD_ROWS, ADA_TN), lambda l, j: (l, 0, j)),
        out_shape=jax.ShapeDtypeStruct((DEPTH, MOD_ROWS, 6 * D), F32),
        compiler_params=_cparams(2),
        name="ada_tables",
    )(cond, w_ada, b_ada.reshape(DEPTH, 1, 6 * D))
    return out.reshape(DEPTH, MOD_ROWS, 6, D)


def _inproj_kernel(h_ref, mod_ref, w_ref, cos_ref, sin_ref,
                   q_ref, kk_ref, vv_ref, u_ref):
    a = _layernorm(h_ref[...]) * (1.0 + mod_ref[1:2, :]) + mod_ref[0:1, :]
    p = jnp.dot(a.astype(BF16), w_ref[...], preferred_element_type=F32)
    cos = cos_ref[...]
    sin = sin_ref[...]
    lane = lax.broadcasted_iota(I32, (1, LANES), 1)
    first_half = (lane % 32) < 16
    low = lane < HEAD_DIM

    def rope(x):
        swapped = jnp.where(first_half, pltpu.roll(x, LANES - 16, 1), pltpu.roll(x, 16, 1))
        return x * cos + swapped * sin

    for c in range(ATTN_W // LANES):
        qc = rope(p[:, c * LANES:(c + 1) * LANES]) * (HEAD_DIM ** -0.5)
        q_ref[:, c * LANES:(c + 1) * LANES] = qc.astype(BF16)
    k = rope(p[:, ATTN_W:ATTN_W + KV_W])
    k_sw = pltpu.roll(k, HEAD_DIM, 1)
    kk_ref[:, 0:LANES] = jnp.where(low, k, k_sw).astype(BF16)
    kk_ref[:, LANES:2 * LANES] = jnp.where(low, k_sw, k).astype(BF16)
    v = p[:, ATTN_W + KV_W:ATTN_W + 2 * KV_W]
    v_sw = pltpu.roll(v, HEAD_DIM, 1)
    vv_ref[:, 0:LANES] = jnp.where(low, v, v_sw).astype(BF16)
    vv_ref[:, LANES:2 * LANES] = jnp.where(low, v_sw, v).astype(BF16)
    u_ref[...] = p[:, ATTN_W + 2 * KV_W:]


def _inproj(h, mods, w_in, cos_t, sin_t, layer):
    n_tiles = NT // TM
    tiles_per_batch = SEQ // TM
    n_lat = NL // TM

    def rope_idx(i):
        return (jnp.where(i < n_lat, i % tiles_per_batch, tiles_per_batch), 0)

    return pl.pallas_call(
        _inproj_kernel,
        grid=(n_tiles,),
        in_specs=[
            pl.BlockSpec((TM, D), lambda i: (i, 0)),
            pl.BlockSpec((None, None, 6, D),
                         lambda i: (layer, _mod_row(i, tiles_per_batch, n_lat), 0, 0)),
            pl.BlockSpec((None, D, IN_W), lambda i: (layer, 0, 0)),
            pl.BlockSpec((TM, LANES), rope_idx),
            pl.BlockSpec((TM, LANES), rope_idx),
        ],
        out_specs=[
            pl.BlockSpec((TM, ATTN_W), lambda i: (i, 0)),
            pl.BlockSpec((TM, 2 * KV_W), lambda i: (i, 0)),
            pl.BlockSpec((TM, 2 * KV_W), lambda i: (i, 0)),
            pl.BlockSpec((TM, POOL_W), lambda i: (i, 0)),
        ],
        out_shape=[
            jax.ShapeDtypeStruct((NT, ATTN_W), BF16),
            jax.ShapeDtypeStruct((NT, 2 * KV_W), BF16),
            jax.ShapeDtypeStruct((NT, 2 * KV_W), BF16),
            jax.ShapeDtypeStruct((NT, POOL_W), F32),
        ],
        compiler_params=_cparams(1),
        name="inproj",
    )(h, mods, w_in, cos_t, sin_t)


def _rope_tables():
    rows = SEQ // GRID_W
    row = jnp.repeat(jnp.arange(rows, dtype=F32), GRID_W)
    col = jnp.tile(jnp.arange(GRID_W, dtype=F32), rows)
    inv = ROPE_BASE ** (-jnp.arange(ROPE_FREQS, dtype=F32) / ROPE_FREQS)
    ang_r = row[:, None] * inv[None, :]
    ang_c = col[:, None] * inv[None, :]
    cr, sr, cc, sc = jnp.cos(ang_r), jnp.sin(ang_r), jnp.cos(ang_c), jnp.sin(ang_c)
    cos_h = jnp.concatenate([cr, cr, cc, cc], axis=-1)
    sin_h = jnp.concatenate([-sr, sr, -sc, sc], axis=-1)
    cos_t = jnp.concatenate([jnp.tile(cos_h, (1, 2)), jnp.ones((TM, LANES), F32)], axis=0)
    sin_t = jnp.concatenate([jnp.tile(sin_h, (1, 2)), jnp.zeros((TM, LANES), F32)], axis=0)
    return cos_t, sin_t


def _pool_group(top, mid, bot, t0, seq_len, width):
    rows = mid.shape[0]
    slab = jnp.concatenate([top, mid, bot], axis=0)
    half = width // 2
    s = slab
    span = 1
    while span < width:
        n = s.shape[0] - span
        s = s[0:n] + s[span:span + n]
        span *= 2
    start = 8 - half
    total = s[start:start + rows]
    pos = t0 + lax.broadcasted_iota(I32, (rows, 1), 0)
    hi = jnp.minimum(pos + half, seq_len)
    lo = jnp.maximum(pos - half, 0)
    cnt = (hi - lo).astype(F32)
    return total / cnt - mid


def _mixer_kernel(sink_ref, q_ref, kk_ref, vv_ref, kkc_ref, vvc_ref, u_ref, wp_ref, ps_ref,
                  o_ref, *, tq, seq_len, band):
    t0 = pl.program_id(1) * tq
    lane = lax.broadcasted_iota(I32, (1, LANES), 1)
    low = lane < HEAD_DIM

    if band:
        n_band = tq + 2 * WINDOW
        ks = jnp.clip(t0 - WINDOW, 0, seq_len - n_band)
        ks = pl.multiple_of(ks, LANES)
        kk = jnp.concatenate([kk_ref[pl.ds(ks, n_band), :], kkc_ref[...]], axis=0)
        vv = jnp.concatenate([vv_ref[pl.ds(ks, n_band), :], vvc_ref[...]], axis=0)
        qpos = t0 + lax.broadcasted_iota(I32, (tq, n_band), 0)
        kpos = ks + lax.broadcasted_iota(I32, (tq, n_band), 1)
        band_ok = jnp.abs(kpos - qpos) <= WINDOW
        valid = jnp.concatenate([band_ok, jnp.ones((tq, CTX), jnp.bool_)], axis=1)
    else:
        kk = kkc_ref[...]
        vv = vvc_ref[...]
        valid = None
    nk = kk.shape[0]
    zero = jnp.zeros_like(kk[:, 0:LANES])

    for g in range(N_KV_HEADS):
        kg = kk[:, g * LANES:(g + 1) * LANES]
        vg = vv[:, g * LANES:(g + 1) * LANES]
        k_st = jnp.concatenate([jnp.where(low, kg, zero), jnp.where(low, zero, kg)], axis=0)
        v_st = jnp.concatenate([jnp.where(low, vg, zero), jnp.where(low, zero, vg)], axis=0)
        for pr in range(2):
            j = 2 * g + pr
            qp = q_ref[:, j * LANES:(j + 1) * LANES]
            s = lax.dot_general(qp, k_st, (((1,), (1,)), ((), ())), preferred_element_type=F32)
            ps = []
            ls = []
            for hh in range(2):
                sink = sink_ref[2 * j + hh]
                sh = s[:, hh * nk:(hh + 1) * nk]
                if valid is not None:
                    sh = jnp.where(valid, sh, NEG_INF)
                m = jnp.maximum(jnp.max(sh, axis=-1, keepdims=True), sink)
                p = jnp.exp(sh - m)
                ls.append(jnp.sum(p, axis=-1, keepdims=True) + jnp.exp(sink - m))
                ps.append(p.astype(BF16))
            o = jnp.dot(jnp.concatenate(ps, axis=1), v_st, preferred_element_type=F32)
            o = o / jnp.where(low, ls[0], ls[1])
            o_ref[:, j * LANES:(j + 1) * LANES] = o.astype(BF16)

    zeros8 = jnp.zeros((8, POOL_W), F32)
    if band:
        t0a = pl.multiple_of(t0, 8)
        mid = u_ref[pl.ds(t0a, tq), :]
        top_s = pl.multiple_of(jnp.maximum(t0 - 8, 0), 8)
        bot_s = pl.multiple_of(jnp.minimum(t0 + tq, seq_len - 8), 8)
        top = jnp.where(t0 > 0, u_ref[pl.ds(top_s, 8), :], zeros8)
        bot = jnp.where(t0 + tq < seq_len, u_ref[pl.ds(bot_s, 8), :], zeros8)
    else:
        mid = u_ref[...]
        top = zeros8
        bot = zeros8
    for g, width in enumerate(POOL_WINDOWS):
        sl = slice(g * POOL_G, (g + 1) * POOL_G)
        diff = _pool_group(top[:, sl], mid[:, sl], bot[:, sl], t0, seq_len, width)
        mixed = jnp.dot(diff.astype(BF16), wp_ref[g], preferred_element_type=F32)
        mixed = mixed * ps_ref[:, sl]
        o_ref[:, ATTN_W + g * POOL_G:ATTN_W + (g + 1) * POOL_G] = mixed.astype(BF16)


def _mixer_latent(q, kk, vv, u, sink_l, w_pool, pool_scale, layer):
    nq = SEQ // TQ
    ctx_blk = NL // CTX
    kern = functools.partial(_mixer_kernel, tq=TQ, seq_len=SEQ, band=True)
    return pl.pallas_call(
        kern,
        grid=(B, nq),
        in_specs=[
            pl.BlockSpec(memory_space=pltpu.SMEM),
            pl.BlockSpec((TQ, ATTN_W), lambda b, t: (b * nq + t, 0)),
            pl.BlockSpec((SEQ, 2 * KV_W), lambda b, t: (b, 0)),
            pl.BlockSpec((SEQ, 2 * KV_W), lambda b, t: (b, 0)),
            pl.BlockSpec((CTX, 2 * KV_W), lambda b, t: (ctx_blk + b, 0)),
            pl.BlockSpec((CTX, 2 * KV_W), lambda b, t: (ctx_blk + b, 0)),
            pl.BlockSpec((SEQ, POOL_W), lambda b, t: (b, 0)),
            pl.BlockSpec((None, len(POOL_WINDOWS), POOL_G, POOL_G), lambda b, t: (layer, 0, 0, 0)),
            pl.BlockSpec((None, 1, POOL_W), lambda b, t: (layer, 0, 0)),
        ],
        out_specs=pl.BlockSpec((TQ, D), lambda b, t: (b * nq + t, 0)),
        out_shape=jax.ShapeDtypeStruct((NL, D), BF16),
        compiler_params=_cparams(2),
        name="mixer_latent",
    )(sink_l, q, kk, vv, kk, vv, u, w_pool, pool_scale)


def _mixer_context(q, kk, vv, u, sink_l, w_pool, pool_scale, layer):
    ctx_blk = NL // CTX
    kern = functools.partial(_mixer_kernel, tq=CTX, seq_len=CTX, band=False)

    def body(sink_ref, q_ref, kkc_ref, vvc_ref, u_ref, wp_ref, ps_ref, o_ref):
        kern(sink_ref, q_ref, None, None, kkc_ref, vvc_ref, u_ref, wp_ref, ps_ref, o_ref)

    return pl.pallas_call(
        body,
        grid=(B, 1),
        in_specs=[
            pl.BlockSpec(memory_space=pltpu.SMEM),
            pl.BlockSpec((CTX, ATTN_W), lambda b, t: (ctx_blk + b, 0)),
            pl.BlockSpec((CTX, 2 * KV_W), lambda b, t: (ctx_blk + b, 0)),
            pl.BlockSpec((CTX, 2 * KV_W), lambda b, t: (ctx_blk + b, 0)),
            pl.BlockSpec((CTX, POOL_W), lambda b, t: (ctx_blk + b, 0)),
            pl.BlockSpec((None, len(POOL_WINDOWS), POOL_G, POOL_G), lambda b, t: (layer, 0, 0, 0)),
            pl.BlockSpec((None, 1, POOL_W), lambda b, t: (layer, 0, 0)),
        ],
        out_specs=pl.BlockSpec((CTX, D), lambda b, t: (b, 0)),
        out_shape=jax.ShapeDtypeStruct((NC, D), BF16),
        compiler_params=_cparams(2),
        name="mixer_context",
    )(sink_l, q, kk, vv, u, w_pool, pool_scale)


def _post_norm(h, upd, gate, g, b):
    return _layernorm(ALPHA * h + gate * upd) * g + b


ROW_TILE = D // LANES


def _store_row_tiles(ref, x):
    rows = x.shape[0]
    for c in range(ROW_TILE):
        ref[pl.ds(c, rows, stride=ROW_TILE), :] = x[:, c * LANES:(c + 1) * LANES]


def _load_row_tiles(ref, rows):
    return jnp.concatenate(
        [ref[pl.ds(c, rows, stride=ROW_TILE), :] for c in range(ROW_TILE)], axis=1)


def _route_top2(a, router_ref, ri_ref, rw_ref):
    logits = jnp.dot(a, router_ref[...], preferred_element_type=F32,
                     precision=lax.Precision.HIGHEST)
    lane = lax.broadcasted_iota(I32, logits.shape, 1)
    lane_f = lane.astype(F32)
    logits = jnp.where(lane < N_EXPERTS, logits, -jnp.inf)
    m1 = jnp.max(logits, axis=-1, keepdims=True)
    i1 = jnp.min(jnp.where(logits == m1, lane_f, float(LANES)), axis=-1, keepdims=True)
    rest = jnp.where(lane_f == i1, -jnp.inf, logits)
    m2 = jnp.max(rest, axis=-1, keepdims=True)
    i2 = jnp.min(jnp.where(rest == m2, lane_f, float(LANES)), axis=-1, keepdims=True)
    e = jnp.exp(m2 - m1)
    w1 = 1.0 / (1.0 + e)
    w2 = e / (1.0 + e)
    ri_ref[...] = jnp.where(lane == 0, i1, jnp.where(lane == 1, i2, 0.0)).astype(I32)
    rw_ref[...] = jnp.where(lane == 0, w1, jnp.where(lane == 1, w2, 0.0))


def _outproj_kernel(mixl_ref, mixc_ref, h_ref, mod_ref, w_ref, lng_ref, lnb_ref, *rest,
                    moe, n_lat, has_ctx):
    if moe:
        router_ref, h1_ref, a2_ref, ri_ref, rw_ref = rest
    else:
        h1_ref, a2_ref = rest
    mix = mixl_ref[...]
    if has_ctx:
        mix = jnp.where(pl.program_id(0) >= n_lat, mixc_ref[...], mix)
    y = jnp.dot(mix, w_ref[...], preferred_element_type=F32)
    h1 = _post_norm(h_ref[...], y, mod_ref[2:3, :], lng_ref[0:1, :], lnb_ref[0:1, :])
    h1_ref[...] = h1
    a2 = _layernorm(h1) * (1.0 + mod_ref[4:5, :]) + mod_ref[3:4, :]
    if moe:
        _store_row_tiles(a2_ref, a2)
        _route_top2(a2, router_ref, ri_ref, rw_ref)
    else:
        a2_ref[...] = a2.astype(BF16)


def _outproj(mix_lat, mix_ctx, h, mods, w_out, ln_g, ln_b, router_pad, layer, n_rows):
    n_tiles = n_rows // TM
    tiles_per_batch = SEQ // TM
    n_lat = NL // TM
    moe = router_pad is not None
    has_ctx = n_rows > NL
    in_specs = [
        pl.BlockSpec((TM, D), lambda i: (jnp.minimum(i, n_lat - 1), 0)),
        pl.BlockSpec((TM, D), lambda i: (jnp.maximum(i - n_lat, 0), 0)),
        pl.BlockSpec((TM, D), lambda i: (i, 0)),
        pl.BlockSpec((None, None, 6, D),
                     lambda i: (layer, _mod_row(i, tiles_per_batch, n_lat), 0, 0)),
        pl.BlockSpec((None, D, D), lambda i: (layer, 0, 0)),
        pl.BlockSpec((None, 2, D), lambda i: (layer, 0, 0)),
        pl.BlockSpec((None, 2, D), lambda i: (layer, 0, 0)),
    ]
    out_specs = [pl.BlockSpec((TM, D), lambda i: (i, 0)), pl.BlockSpec((TM, D), lambda i: (i, 0))]
    out_shape = [jax.ShapeDtypeStruct((n_rows, D), F32), jax.ShapeDtypeStruct((n_rows, D), BF16)]
    args = [mix_lat, mix_ctx, h, mods, w_out, ln_g, ln_b]
    if moe:
        in_specs.append(pl.BlockSpec((None, D, LANES), lambda i: (layer // 2, 0, 0)))
        out_specs[1] = pl.BlockSpec((TM * ROW_TILE, LANES), lambda i: (i, 0))
        out_shape[1] = jax.ShapeDtypeStruct((n_rows * ROW_TILE, LANES), F32)
        out_specs += [pl.BlockSpec((TM, LANES), lambda i: (i, 0)),
                      pl.BlockSpec((TM, LANES), lambda i: (i, 0))]
        out_shape += [jax.ShapeDtypeStruct((n_rows, LANES), I32),
                      jax.ShapeDtypeStruct((n_rows, LANES), F32)]
        args.append(router_pad)
    return pl.pallas_call(
        functools.partial(_outproj_kernel, moe=moe, n_lat=n_lat, has_ctx=has_ctx),
        grid=(n_tiles,),
        in_specs=in_specs,
        out_specs=out_specs,
        out_shape=out_shape,
        compiler_params=_cparams(1),
        name="outproj_moe" if moe else "outproj",
    )(*args)


def _ffn_dense_kernel(a_ref, h_ref, mod_ref, w1_ref, w3_ref, w2_ref, lng_ref, lnb_ref, o_ref):
    a = a_ref[...]
    g = jnp.dot(a, w1_ref[...], preferred_element_type=F32)
    u = jnp.dot(a, w3_ref[...], preferred_element_type=F32)
    t = (jax.nn.silu(g) * u).astype(BF16)
    f = jnp.dot(t, w2_ref[...], preferred_element_type=F32)
    o_ref[...] = _post_norm(h_ref[...], f, mod_ref[5:6, :], lng_ref[1:2, :], lnb_ref[1:2, :])


def _ffn_dense(a2, h1, mods, w1, w3, w2, ln_g, ln_b, layer, n_rows):
    n_tiles = n_rows // TM
    tiles_per_batch = SEQ // TM
    n_lat = NL // TM
    idx = layer // 2
    resident = pl.Buffered(1)
    return pl.pallas_call(
        _ffn_dense_kernel,
        grid=(n_tiles,),
        in_specs=[
            pl.BlockSpec((TM, D), lambda i: (i, 0)),
            pl.BlockSpec((TM, D), lambda i: (i, 0)),
            pl.BlockSpec((None, None, 6, D),
                         lambda i: (layer, _mod_row(i, tiles_per_batch, n_lat), 0, 0)),
            pl.BlockSpec((None, D, D_FF_DENSE), lambda i: (idx, 0, 0), pipeline_mode=resident),
            pl.BlockSpec((None, D, D_FF_DENSE), lambda i: (idx, 0, 0), pipeline_mode=resident),
            pl.BlockSpec((None, D_FF_DENSE, D), lambda i: (idx, 0, 0), pipeline_mode=resident),
            pl.BlockSpec((None, 2, D), lambda i: (layer, 0, 0)),
            pl.BlockSpec((None, 2, D), lambda i: (layer, 0, 0)),
        ],
        out_specs=pl.BlockSpec((TM, D), lambda i: (i, 0)),
        out_shape=jax.ShapeDtypeStruct((n_rows, D), F32),
        compiler_params=_cparams(1),
        name="ffn_dense",
    )(a2, h1, mods, w1, w3, w2, ln_g, ln_b)


def _dispatch_kernel(src_ref, a_hbm, o_hbm, sem):
    i = pl.program_id(0)
    n = pl.num_programs(0)
    base = i * GATHER_ROWS
    slot = i % 2

    def issue(r, carry):
        s = pl.multiple_of(src_ref[base + r] * ROW_TILE, ROW_TILE)
        d = pl.multiple_of((base + r) * ROW_TILE, ROW_TILE)
        pltpu.make_async_copy(a_hbm.at[pl.ds(s, ROW_TILE)], o_hbm.at[pl.ds(d, ROW_TILE)],
                              sem.at[slot]).start()
        return carry

    lax.fori_loop(0, GATHER_ROWS, issue, 0, unroll=8)

    def wait_rows(s):
        blk = o_hbm.at[pl.ds(0, GATHER_ROWS * ROW_TILE)]
        pltpu.make_async_copy(blk, blk, sem.at[s]).wait()

    @pl.when(i > 0)
    def _():
        wait_rows(1 - slot)

    @pl.when(i == n - 1)
    def _():
        wait_rows(slot)


def _dispatch(src, a2, n_slots):
    return pl.pallas_call(
        _dispatch_kernel,
        grid_spec=pltpu.PrefetchScalarGridSpec(
            num_scalar_prefetch=1,
            grid=(n_slots // GATHER_ROWS,),
            in_specs=[pl.BlockSpec(memory_space=pl.ANY)],
            out_specs=pl.BlockSpec(memory_space=pl.ANY),
            scratch_shapes=[pltpu.SemaphoreType.DMA((2,))],
        ),
        out_shape=jax.ShapeDtypeStruct((n_slots * ROW_TILE, LANES), F32),
        compiler_params=_cparams(1),
        name="moe_dispatch",
    )(src, a2)


def _experts_kernel(te_ref, tv_ref, x_ref, w1_ref, w3_ref, w2_ref, o_ref, xb_ref, acc_ref):
    i = pl.program_id(0)
    j = pl.program_id(1)
    nj = pl.num_programs(1)
    valid = tv_ref[i] > 0

    @pl.when(jnp.logical_and(valid, j == 0))
    def _():
        xb_ref[...] = _load_row_tiles(x_ref, MOE_TM).astype(BF16)

    @pl.when(valid)
    def _():
        x = xb_ref[...]
        g = jnp.dot(x, w1_ref[...], preferred_element_type=F32)
        u = jnp.dot(x, w3_ref[...], preferred_element_type=F32)
        t = (jax.nn.silu(g) * u).astype(BF16)
        part = jnp.dot(t, w2_ref[...], preferred_element_type=F32)

        @pl.when(j == 0)
        def _():
            acc_ref[...] = part

        @pl.when(j > 0)
        def _():
            acc_ref[...] += part

        @pl.when(j == nj - 1)
        def _():
            _store_row_tiles(o_ref, acc_ref[...])

    @pl.when(jnp.logical_and(jnp.logical_not(valid), j == nj - 1))
    def _():
        o_ref[...] = jnp.zeros_like(o_ref)


def _experts(tile_e, tile_v, xs, w1, w3, w2, idx, n_slots):
    n_tiles = n_slots // MOE_TM
    nj = D_FF_EXPERT // MOE_TF

    def chunk(j, tv, i):
        return jnp.where(tv[i] > 0, j, nj - 1)

    return pl.pallas_call(
        _experts_kernel,
        grid_spec=pltpu.PrefetchScalarGridSpec(
            num_scalar_prefetch=2,
            grid=(n_tiles, nj),
            in_specs=[
                pl.BlockSpec((MOE_TM * ROW_TILE, LANES), lambda i, j, te, tv: (i, 0)),
                pl.BlockSpec((None, None, D, MOE_TF),
                             lambda i, j, te, tv: (idx, te[i], 0, chunk(j, tv, i))),
                pl.BlockSpec((None, None, D, MOE_TF),
                             lambda i, j, te, tv: (idx, te[i], 0, chunk(j, tv, i))),
                pl.BlockSpec((None, None, MOE_TF, D),
                             lambda i, j, te, tv: (idx, te[i], chunk(j, tv, i), 0)),
            ],
            out_specs=pl.BlockSpec((MOE_TM * ROW_TILE, LANES), lambda i, j, te, tv: (i, 0)),
            scratch_shapes=[pltpu.VMEM((MOE_TM, D), BF16), pltpu.VMEM((MOE_TM, D), F32)],
        ),
        out_shape=jax.ShapeDtypeStruct((n_slots * ROW_TILE, LANES), F32),
        compiler_params=_cparams(2),
        name="moe_experts",
    )(tile_e, tile_v, xs, w1, w3, w2)


def _combine_kernel(pos_ref, y_hbm, h_ref, rw_ref, mod_ref, lng_ref, lnb_ref, o_ref, buf, sem):
    i = pl.program_id(0)
    n = pl.num_programs(0)

    def issue(step, slot):
        base = step * COMBINE_ROWS

        def body(r, carry):
            d = pl.multiple_of(r * ROW_TILE, ROW_TILE)
            for k in range(2):
                p = pl.multiple_of(pos_ref[2 * (base + r) + k] * ROW_TILE, ROW_TILE)
                pltpu.make_async_copy(y_hbm.at[pl.ds(p, ROW_TILE)],
                                      buf.at[slot, k, pl.ds(d, ROW_TILE)], sem.at[slot]).start()
            return carry

        lax.fori_loop(0, COMBINE_ROWS, body, 0, unroll=4)

    @pl.when(i == 0)
    def _():
        issue(0, 0)

    @pl.when(i + 1 < n)
    def _():
        issue(i + 1, (i + 1) % 2)

    slot = i % 2
    for k in range(2):
        pltpu.make_async_copy(y_hbm.at[pl.ds(0, COMBINE_ROWS * ROW_TILE)], buf.at[slot, k],
                              sem.at[slot]).wait()
    w = rw_ref[...]
    f = (w[:, 0:1] * _load_row_tiles(buf.at[slot, 0], COMBINE_ROWS)
         + w[:, 1:2] * _load_row_tiles(buf.at[slot, 1], COMBINE_ROWS))
    o_ref[...] = _post_norm(h_ref[...], f, mod_ref[5:6, :], lng_ref[1:2, :], lnb_ref[1:2, :])


def _combine(pos, y, h1, rw, mods, ln_g, ln_b, layer, n_rows):
    n_tiles = n_rows // COMBINE_ROWS
    tiles_per_batch = SEQ // COMBINE_ROWS
    n_lat = NL // COMBINE_ROWS
    return pl.pallas_call(
        _combine_kernel,
        grid_spec=pltpu.PrefetchScalarGridSpec(
            num_scalar_prefetch=1,
            grid=(n_tiles,),
            in_specs=[
                pl.BlockSpec(memory_space=pl.ANY),
                pl.BlockSpec((COMBINE_ROWS, D), lambda i, p: (i, 0)),
                pl.BlockSpec((COMBINE_ROWS, LANES), lambda i, p: (i, 0)),
                pl.BlockSpec((None, None, 6, D),
                             lambda i, p: (layer, _mod_row(i, tiles_per_batch, n_lat), 0, 0)),
                pl.BlockSpec((None, 2, D), lambda i, p: (layer, 0, 0)),
                pl.BlockSpec((None, 2, D), lambda i, p: (layer, 0, 0)),
            ],
            out_specs=pl.BlockSpec((COMBINE_ROWS, D), lambda i, p: (i, 0)),
            scratch_shapes=[pltpu.VMEM((2, 2, COMBINE_ROWS * ROW_TILE, LANES), F32),
                            pltpu.SemaphoreType.DMA((2,))],
        ),
        out_shape=jax.ShapeDtypeStruct((n_rows, D), F32),
        compiler_params=_cparams(1),
        name="moe_combine",
    )(pos, y, h1, rw, mods, ln_g, ln_b)


def _route_plan(eid, n_rows, n_slots):
    n_tiles = n_slots // MOE_TM
    e_flat = eid.reshape(-1)
    onehot = (e_flat[:, None] == jnp.arange(N_EXPERTS, dtype=I32)[None, :]).astype(I32)
    csum = jnp.cumsum(onehot, axis=0)
    rank = jnp.sum((csum - onehot) * onehot, axis=1)
    count = csum[-1]
    padded = ((count + MOE_TM - 1) // MOE_TM) * MOE_TM
    ends = jnp.cumsum(padded)
    offs = ends - padded
    pos = jnp.sum(onehot * offs[None, :], axis=1) + rank
    src = jnp.zeros((n_slots,), I32).at[pos].set(jnp.arange(2 * n_rows, dtype=I32) // 2)
    tile_start = jnp.arange(n_tiles, dtype=I32) * MOE_TM
    tile_e = jnp.minimum(jnp.sum((tile_start[:, None] >= ends[None, :]).astype(I32), axis=1),
                         N_EXPERTS - 1)
    tile_v = (tile_start < ends[-1]).astype(I32)
    return pos.astype(I32), src, tile_e.astype(I32), tile_v


def _moe(a2, h1, ri, rw, mods, w1, w3, w2, ln_g, ln_b, layer, n_rows):
    n_slots = 2 * n_rows + N_EXPERTS * MOE_TM
    pos, src, tile_e, tile_v = _route_plan(ri[:n_rows, :2], n_rows, n_slots)
    xs = _dispatch(src, a2, n_slots)
    ys = _experts(tile_e, tile_v, xs, w1, w3, w2, layer // 2, n_slots)
    return _combine(pos, ys, h1, rw, mods, ln_g, ln_b, layer, n_rows)


def kernel(x, c, ctx, c_ctx, w_ada, b_ada, w_in, w_pool, pool_scale, sink, w_out, ln_g, ln_b,
           dense_w1, dense_w3, dense_w2, router, moe_w1, moe_w3, moe_w2):
    cond = jnp.concatenate([c, c_ctx[None, :], jnp.zeros((MOD_ROWS - B - 1, D), F32)], axis=0)
    mods = _ada_tables(cond, w_ada, b_ada)
    cos_t, sin_t = _rope_tables()

    w_in_b = w_in.astype(BF16)
    w_pool_b = w_pool.astype(BF16)
    w_out_b = w_out.astype(BF16)
    dw1, dw3, dw2 = dense_w1.astype(BF16), dense_w3.astype(BF16), dense_w2.astype(BF16)
    mw1, mw3, mw2 = moe_w1.astype(BF16), moe_w3.astype(BF16), moe_w2.astype(BF16)
    router_pad = jnp.pad(router, ((0, 0), (0, 0), (0, LANES - N_EXPERTS)))
    pool_scale3 = pool_scale.reshape(DEPTH, 1, POOL_W)

    h = jnp.concatenate([x.reshape(NL, D), ctx.reshape(NC, D)], axis=0)
    for l in range(DEPTH):
        last = l == DEPTH - 1
        n_rows = NL if last else NT
        q, kk, vv, u = _inproj(h, mods, w_in_b, cos_t, sin_t, l)
        mix = _mixer_latent(q, kk, vv, u, sink[l], w_pool_b, pool_scale3, l)
        mix_c = mix if last else _mixer_context(q, kk, vv, u, sink[l], w_pool_b, pool_scale3, l)
        if l % 2 == 0:
            h1, a2 = _outproj(mix, mix_c, h, mods, w_out_b, ln_g, ln_b, None, l, n_rows)
            h = _ffn_dense(a2, h1, mods, dw1, dw3, dw2, ln_g, ln_b, l, n_rows)
        else:
            h1, a2, ri, rw = _outproj(mix, mix_c, h, mods, w_out_b, ln_g, ln_b, router_pad, l,
                                      n_rows)
            h = _moe(a2, h1, ri, rw, mods, mw1, mw3, mw2, ln_g, ln_b, l, n_rows)
    return h.reshape(B, SEQ, D)
```

```python
import functools

import jax
import jax.numpy as jnp
from jax import lax
from jax.experimental import pallas as pl
from jax.experimental.pallas import tpu as pltpu

F32 = jnp.float32
BF16 = jnp.bfloat16
I32 = jnp.int32

D = 1024
B = 8
SEQ = 2048
DEPTH = 4
CTX = 256
GRID_W = 64
HEAD_DIM = 64
N_Q_HEADS = 8
N_KV_HEADS = 2
ATTN_W = N_Q_HEADS * HEAD_DIM
KV_W = N_KV_HEADS * HEAD_DIM
POOL_WINDOWS = (2, 4, 8, 16)
POOL_W = D - ATTN_W
POOL_G = POOL_W // len(POOL_WINDOWS)
IN_W = ATTN_W + 2 * KV_W + POOL_W
WINDOW = 128
ROPE_BASE = 10000.0
ROPE_FREQS = HEAD_DIM // 4
D_FF_DENSE = 2816
N_EXPERTS = 8
D_FF_EXPERT = 3584
ALPHA = (2 * DEPTH) ** 0.25
LN_EPS = 1e-6
NEG_INF = -1e30

NL = B * SEQ
NC = B * CTX
NT = NL + NC
MOD_ROWS = 16
LANES = 128

TM = 512
TQ = 128
MOE_TM = 512
FF_CHUNK = 1024
COMBINE_ROWS = 256

VMEM_LIMIT = 56 * 1024 * 1024


def _cparams(n_axes, vmem=VMEM_LIMIT):
    return pltpu.CompilerParams(
        dimension_semantics=("arbitrary",) * n_axes, vmem_limit_bytes=vmem)


def _layernorm(x):
    mu = jnp.mean(x, axis=-1, keepdims=True)
    xc = x - mu
    var = jnp.mean(xc * xc, axis=-1, keepdims=True)
    return xc * lax.rsqrt(var + LN_EPS)


def _mod_row(tile, tiles_per_batch, n_latent_tiles):
    return jnp.where(tile < n_latent_tiles, tile // tiles_per_batch, B)


ADA_TN = 1536


def _ada_kernel(s_ref, w_ref, b_ref, o_ref):
    s = jax.nn.silu(s_ref[...]).astype(BF16)
    w = w_ref[...].astype(BF16)
    o_ref[...] = jnp.dot(s, w, preferred_element_type=F32) + b_ref[...]


def _ada_tables(cond, w_ada, b_ada):
    n_col = (6 * D) // ADA_TN
    out = pl.pallas_call(
        _ada_kernel,
        grid=(DEPTH, n_col),
        in_specs=[
            pl.BlockSpec((MOD_ROWS, D), lambda l, j: (0, 0)),
            pl.BlockSpec((None, D, ADA_TN), lambda l, j: (l, 0, j)),
            pl.BlockSpec((None, 1, ADA_TN), lambda l, j: (l, 0, j)),
        ],
        out_specs=pl.BlockSpec((None, MOD_ROWS, ADA_TN), lambda l, j: (l, 0, j)),
        out_shape=jax.ShapeDtypeStruct((DEPTH, MOD_ROWS, 6 * D), F32),
        compiler_params=_cparams(2),
        name="ada_tables",
    )(cond, w_ada, b_ada.reshape(DEPTH, 1, 6 * D))
    return out.reshape(DEPTH, MOD_ROWS, 6, D)


def _inproj_kernel(h_ref, mod_ref, w_ref, cos_ref, sin_ref,
                   q_ref, kk_ref, vv_ref, u_ref):
    a = _layernorm(h_ref[...]) * (1.0 + mod_ref[1:2, :]) + mod_ref[0:1, :]
    p = jnp.dot(a.astype(BF16), w_ref[...], preferred_element_type=F32)
    cos = cos_ref[...]
    sin = sin_ref[...]
    lane = lax.broadcasted_iota(I32, (1, LANES), 1)
    first_half = (lane % 32) < 16
    low = lane < HEAD_DIM

    def rope(x):
        swapped = jnp.where(first_half, pltpu.roll(x, LANES - 16, 1), pltpu.roll(x, 16, 1))
        return x * cos + swapped * sin

    for c in range(ATTN_W // LANES):
        qc = rope(p[:, c * LANES:(c + 1) * LANES]) * (HEAD_DIM ** -0.5)
        q_ref[:, c * LANES:(c + 1) * LANES] = qc.astype(BF16)
    k = rope(p[:, ATTN_W:ATTN_W + KV_W])
    k_sw = pltpu.roll(k, HEAD_DIM, 1)
    kk_ref[:, 0:LANES] = jnp.where(low, k, k_sw).astype(BF16)
    kk_ref[:, LANES:2 * LANES] = jnp.where(low, k_sw, k).astype(BF16)
    v = p[:, ATTN_W + KV_W:ATTN_W + 2 * KV_W]
    v_sw = pltpu.roll(v, HEAD_DIM, 1)
    vv_ref[:, 0:LANES] = jnp.where(low, v, v_sw).astype(BF16)
    vv_ref[:, LANES:2 * LANES] = jnp.where(low, v_sw, v).astype(BF16)
    u_ref[...] = p[:, ATTN_W + 2 * KV_W:]


def _inproj(h, mods, w_in, cos_t, sin_t, layer):
    n_tiles = NT // TM
    tiles_per_batch = SEQ // TM
    n_lat = NL // TM

    def rope_idx(i):
        return (jnp.where(i < n_lat, i % tiles_per_batch, tiles_per_batch), 0)

    return pl.pallas_call(
        _inproj_kernel,
        grid=(n_tiles,),
        in_specs=[
            pl.BlockSpec((TM, D), lambda i: (i, 0)),
            pl.BlockSpec((None, None, 6, D),
                         lambda i: (layer, _mod_row(i, tiles_per_batch, n_lat), 0, 0)),
            pl.BlockSpec((None, D, IN_W), lambda i: (layer, 0, 0)),
            pl.BlockSpec((TM, LANES), rope_idx),
            pl.BlockSpec((TM, LANES), rope_idx),
        ],
        out_specs=[
            pl.BlockSpec((TM, ATTN_W), lambda i: (i, 0)),
            pl.BlockSpec((TM, 2 * KV_W), lambda i: (i, 0)),
            pl.BlockSpec((TM, 2 * KV_W), lambda i: (i, 0)),
            pl.BlockSpec((TM, POOL_W), lambda i: (i, 0)),
        ],
        out_shape=[
            jax.ShapeDtypeStruct((NT, ATTN_W), BF16),
            jax.ShapeDtypeStruct((NT, 2 * KV_W), BF16),
            jax.ShapeDtypeStruct((NT, 2 * KV_W), BF16),
            jax.ShapeDtypeStruct((NT, POOL_W), F32),
        ],
        compiler_params=_cparams(1),
        name="inproj",
    )(h, mods, w_in, cos_t, sin_t)


def _rope_tables():
    rows = SEQ // GRID_W
    row = jnp.repeat(jnp.arange(rows, dtype=F32), GRID_W)
    col = jnp.tile(jnp.arange(GRID_W, dtype=F32), rows)
    inv = ROPE_BASE ** (-jnp.arange(ROPE_FREQS, dtype=F32) / ROPE_FREQS)
    ang_r = row[:, None] * inv[None, :]
    ang_c = col[:, None] * inv[None, :]
    cr, sr, cc, sc = jnp.cos(ang_r), jnp.sin(ang_r), jnp.cos(ang_c), jnp.sin(ang_c)
    cos_h = jnp.concatenate([cr, cr, cc, cc], axis=-1)
    sin_h = jnp.concatenate([-sr, sr, -sc, sc], axis=-1)
    cos_t = jnp.concatenate([jnp.tile(cos_h, (1, 2)), jnp.ones((TM, LANES), F32)], axis=0)
    sin_t = jnp.concatenate([jnp.tile(sin_h, (1, 2)), jnp.zeros((TM, LANES), F32)], axis=0)
    return cos_t, sin_t


def _pool_group(top, mid, bot, t0, seq_len, width):
    rows = mid.shape[0]
    slab = jnp.concatenate([top, mid, bot], axis=0)
    half = width // 2
    s = slab
    span = 1
    while span < width:
        n = s.shape[0] - span
        s = s[0:n] + s[span:span + n]
        span *= 2
    start = 8 - half
    total = s[start:start + rows]
    pos = t0 + lax.broadcasted_iota(I32, (rows, 1), 0)
    hi = jnp.minimum(pos + half, seq_len)
    lo = jnp.maximum(pos - half, 0)
    cnt = (hi - lo).astype(F32)
    return total / cnt - mid


def _mixer_kernel(sink_ref, q_ref, kk_ref, vv_ref, kkc_ref, vvc_ref, u_ref, wp_ref, ps_ref,
                  o_ref, *, tq, seq_len, band):
    t0 = pl.program_id(1) * tq
    lane = lax.broadcasted_iota(I32, (1, LANES), 1)
    low = lane < HEAD_DIM

    if band:
        n_band = tq + 2 * WINDOW
        ks = jnp.clip(t0 - WINDOW, 0, seq_len - n_band)
        ks = pl.multiple_of(ks, LANES)
        kk = jnp.concatenate([kk_ref[pl.ds(ks, n_band), :], kkc_ref[...]], axis=0)
        vv = jnp.concatenate([vv_ref[pl.ds(ks, n_band), :], vvc_ref[...]], axis=0)
        qpos = t0 + lax.broadcasted_iota(I32, (tq, n_band), 0)
        kpos = ks + lax.broadcasted_iota(I32, (tq, n_band), 1)
        band_ok = jnp.abs(kpos - qpos) <= WINDOW
        valid = jnp.concatenate([band_ok, jnp.ones((tq, CTX), jnp.bool_)], axis=1)
    else:
        kk = kkc_ref[...]
        vv = vvc_ref[...]
        valid = None
    nk = kk.shape[0]
    zero = jnp.zeros_like(kk[:, 0:LANES])

    for g in range(N_KV_HEADS):
        kg = kk[:, g * LANES:(g + 1) * LANES]
        vg = vv[:, g * LANES:(g + 1) * LANES]
        k_st = jnp.concatenate([jnp.where(low, kg, zero), jnp.where(low, zero, kg)], axis=0)
        v_st = jnp.concatenate([jnp.where(low, vg, zero), jnp.where(low, zero, vg)], axis=0)
        for pr in range(2):
            j = 2 * g + pr
            qp = q_ref[:, j * LANES:(j + 1) * LANES]
            s = lax.dot_general(qp, k_st, (((1,), (1,)), ((), ())), preferred_element_type=F32)
            ps = []
            ls = []
            for hh in range(2):
                sink = sink_ref[2 * j + hh]
                sh = s[:, hh * nk:(hh + 1) * nk]
                if valid is not None:
                    sh = jnp.where(valid, sh, NEG_INF)
                m = jnp.maximum(jnp.max(sh, axis=-1, keepdims=True), sink)
                p = jnp.exp(sh - m)
                ls.append(jnp.sum(p, axis=-1, keepdims=True) + jnp.exp(sink - m))
                ps.append(p.astype(BF16))
            o = jnp.dot(jnp.concatenate(ps, axis=1), v_st, preferred_element_type=F32)
            o = o / jnp.where(low, ls[0], ls[1])
            o_ref[:, j * LANES:(j + 1) * LANES] = o.astype(BF16)

    zeros8 = jnp.zeros((8, POOL_W), F32)
    if band:
        t0a = pl.multiple_of(t0, 8)
        mid = u_ref[pl.ds(t0a, tq), :]
        top_s = pl.multiple_of(jnp.maximum(t0 - 8, 0), 8)
        bot_s = pl.multiple_of(jnp.minimum(t0 + tq, seq_len - 8), 8)
        top = jnp.where(t0 > 0, u_ref[pl.ds(top_s, 8), :], zeros8)
        bot = jnp.where(t0 + tq < seq_len, u_ref[pl.ds(bot_s, 8), :], zeros8)
    else:
        mid = u_ref[...]
        top = zeros8
        bot = zeros8
    for g, width in enumerate(POOL_WINDOWS):
        sl = slice(g * POOL_G, (g + 1) * POOL_G)
        diff = _pool_group(top[:, sl], mid[:, sl], bot[:, sl], t0, seq_len, width)
        mixed = jnp.dot(diff.astype(BF16), wp_ref[g], preferred_element_type=F32)
        mixed = mixed * ps_ref[:, sl]
        o_ref[:, ATTN_W + g * POOL_G:ATTN_W + (g + 1) * POOL_G] = mixed.astype(BF16)


def _mixer_latent(q, kk, vv, u, sink_l, w_pool, pool_scale, layer):
    nq = SEQ // TQ
    ctx_blk = NL // CTX
    kern = functools.partial(_mixer_kernel, tq=TQ, seq_len=SEQ, band=True)
    return pl.pallas_call(
        kern,
        grid=(B, nq),
        in_specs=[
            pl.BlockSpec(memory_space=pltpu.SMEM),
            pl.BlockSpec((TQ, ATTN_W), lambda b, t: (b * nq + t, 0)),
            pl.BlockSpec((SEQ, 2 * KV_W), lambda b, t: (b, 0)),
            pl.BlockSpec((SEQ, 2 * KV_W), lambda b, t: (b, 0)),
            pl.BlockSpec((CTX, 2 * KV_W), lambda b, t: (ctx_blk + b, 0)),
            pl.BlockSpec((CTX, 2 * KV_W), lambda b, t: (ctx_blk + b, 0)),
            pl.BlockSpec((SEQ, POOL_W), lambda b, t: (b, 0)),
            pl.BlockSpec((None, len(POOL_WINDOWS), POOL_G, POOL_G), lambda b, t: (layer, 0, 0, 0)),
            pl.BlockSpec((None, 1, POOL_W), lambda b, t: (layer, 0, 0)),
        ],
        out_specs=pl.BlockSpec((TQ, D), lambda b, t: (b * nq + t, 0)),
        out_shape=jax.ShapeDtypeStruct((NL, D), BF16),
        compiler_params=_cparams(2),
        name="mixer_latent",
    )(sink_l, q, kk, vv, kk, vv, u, w_pool, pool_scale)


def _mixer_context(q, kk, vv, u, sink_l, w_pool, pool_scale, layer):
    ctx_blk = NL // CTX
    kern = functools.partial(_mixer_kernel, tq=CTX, seq_len=CTX, band=False)

    def body(sink_ref, q_ref, kkc_ref, vvc_ref, u_ref, wp_ref, ps_ref, o_ref):
        kern(sink_ref, q_ref, None, None, kkc_ref, vvc_ref, u_ref, wp_ref, ps_ref, o_ref)

    return pl.pallas_call(
        body,
        grid=(B, 1),
        in_specs=[
            pl.BlockSpec(memory_space=pltpu.SMEM),
            pl.BlockSpec((CTX, ATTN_W), lambda b, t: (ctx_blk + b, 0)),
            pl.BlockSpec((CTX, 2 * KV_W), lambda b, t: (ctx_blk + b, 0)),
            pl.BlockSpec((CTX, 2 * KV_W), lambda b, t: (ctx_blk + b, 0)),
            pl.BlockSpec((CTX, POOL_W), lambda b, t: (ctx_blk + b, 0)),
            pl.BlockSpec((None, len(POOL_WINDOWS), POOL_G, POOL_G), lambda b, t: (layer, 0, 0, 0)),
            pl.BlockSpec((None, 1, POOL_W), lambda b, t: (layer, 0, 0)),
        ],
        out_specs=pl.BlockSpec((CTX, D), lambda b, t: (b, 0)),
        out_shape=jax.ShapeDtypeStruct((NC, D), BF16),
        compiler_params=_cparams(2),
        name="mixer_context",
    )(sink_l, q, kk, vv, u, w_pool, pool_scale)


def _post_norm(h, upd, gate, g, b):
    return _layernorm(ALPHA * h + gate * upd) * g + b


ROW_TILE = D // LANES


def _store_row_tiles(ref, x):
    rows = x.shape[0]
    for c in range(ROW_TILE):
        ref[pl.ds(c, rows, stride=ROW_TILE), :] = x[:, c * LANES:(c + 1) * LANES]


def _load_row_tiles(ref, rows):
    return jnp.concatenate(
        [ref[pl.ds(c, rows, stride=ROW_TILE), :] for c in range(ROW_TILE)], axis=1)


def _route_top2(a, router_ref, ri_ref, rw_ref):
    logits = jnp.dot(a, router_ref[...], preferred_element_type=F32,
                     precision=lax.Precision.HIGHEST)
    lane = lax.broadcasted_iota(I32, logits.shape, 1)
    lane_f = lane.astype(F32)
    logits = jnp.where(lane < N_EXPERTS, logits, -jnp.inf)
    m1 = jnp.max(logits, axis=-1, keepdims=True)
    i1 = jnp.min(jnp.where(logits == m1, lane_f, float(LANES)), axis=-1, keepdims=True)
    rest = jnp.where(lane_f == i1, -jnp.inf, logits)
    m2 = jnp.max(rest, axis=-1, keepdims=True)
    i2 = jnp.min(jnp.where(rest == m2, lane_f, float(LANES)), axis=-1, keepdims=True)
    e = jnp.exp(m2 - m1)
    w1 = 1.0 / (1.0 + e)
    w2 = e / (1.0 + e)
    ri_ref[...] = jnp.where(lane == 0, i1, jnp.where(lane == 1, i2, 0.0)).astype(I32)
    rw_ref[...] = jnp.where(lane == 0, w1, jnp.where(lane == 1, w2, 0.0))


def _outproj_kernel(mixl_ref, mixc_ref, h_ref, mod_ref, w_ref, lng_ref, lnb_ref, *rest,
                    moe, n_lat, has_ctx):
    if moe:
        router_ref, h1_ref, a2_ref, ri_ref, rw_ref = rest
    else:
        h1_ref, a2_ref = rest
    mix = mixl_ref[...]
    if has_ctx:
        mix = jnp.where(pl.program_id(0) >= n_lat, mixc_ref[...], mix)
    y = jnp.dot(mix, w_ref[...], preferred_element_type=F32)
    h1 = _post_norm(h_ref[...], y, mod_ref[2:3, :], lng_ref[0:1, :], lnb_ref[0:1, :])
    h1_ref[...] = h1
    a2 = _layernorm(h1) * (1.0 + mod_ref[4:5, :]) + mod_ref[3:4, :]
    if moe:
        _store_row_tiles(a2_ref, a2)
        _route_top2(a2, router_ref, ri_ref, rw_ref)
    else:
        a2_ref[...] = a2.astype(BF16)


def _outproj(mix_lat, mix_ctx, h, mods, w_out, ln_g, ln_b, router_pad, layer, n_rows):
    n_tiles = n_rows // TM
    tiles_per_batch = SEQ // TM
    n_lat = NL // TM
    moe = router_pad is not None
    has_ctx = n_rows > NL
    in_specs = [
        pl.BlockSpec((TM, D), lambda i: (jnp.minimum(i, n_lat - 1), 0)),
        pl.BlockSpec((TM, D), lambda i: (jnp.maximum(i - n_lat, 0), 0)),
        pl.BlockSpec((TM, D), lambda i: (i, 0)),
        pl.BlockSpec((None, None, 6, D),
                     lambda i: (layer, _mod_row(i, tiles_per_batch, n_lat), 0, 0)),
        pl.BlockSpec((None, D, D), lambda i: (layer, 0, 0)),
        pl.BlockSpec((None, 2, D), lambda i: (layer, 0, 0)),
        pl.BlockSpec((None, 2, D), lambda i: (layer, 0, 0)),
    ]
    out_specs = [pl.BlockSpec((TM, D), lambda i: (i, 0)), pl.BlockSpec((TM, D), lambda i: (i, 0))]
    out_shape = [jax.ShapeDtypeStruct((n_rows, D), F32), jax.ShapeDtypeStruct((n_rows, D), BF16)]
    args = [mix_lat, mix_ctx, h, mods, w_out, ln_g, ln_b]
    if moe:
        in_specs.append(pl.BlockSpec((None, D, LANES), lambda i: (layer // 2, 0, 0)))
        out_specs[1] = pl.BlockSpec((TM * ROW_TILE, LANES), lambda i: (i, 0))
        out_shape[1] = jax.ShapeDtypeStruct((n_rows * ROW_TILE, LANES), F32)
        out_specs += [pl.BlockSpec((TM, LANES), lambda i: (i, 0)),
                      pl.BlockSpec((TM, LANES), lambda i: (i, 0))]
        out_shape += [jax.ShapeDtypeStruct((n_rows, LANES), I32),
                      jax.ShapeDtypeStruct((n_rows, LANES), F32)]
        args.append(router_pad)
    return pl.pallas_call(
        functools.partial(_outproj_kernel, moe=moe, n_lat=n_lat, has_ctx=has_ctx),
        grid=(n_tiles,),
        in_specs=in_specs,
        out_specs=out_specs,
        out_shape=out_shape,
        compiler_params=_cparams(1),
        name="outproj_moe" if moe else "outproj",
    )(*args)


def _swiglu_hidden(x, w1_ref, w3_ref, t_ref, d_ff):
    for c0 in range(0, d_ff, FF_CHUNK):
        c1 = min(c0 + FF_CHUNK, d_ff)
        g = jnp.dot(x, w1_ref[:, c0:c1], preferred_element_type=F32)
        u = jnp.dot(x, w3_ref[:, c0:c1], preferred_element_type=F32)
        t_ref[:, c0:c1] = (jax.nn.silu(g) * u).astype(BF16)


def _ffn_dense_kernel(a_ref, h_ref, mod_ref, w1_ref, w3_ref, w2_ref, lng_ref, lnb_ref, o_ref,
                      t_ref):
    _swiglu_hidden(a_ref[...], w1_ref, w3_ref, t_ref, D_FF_DENSE)
    f = jnp.dot(t_ref[...], w2_ref[...], preferred_element_type=F32)
    o_ref[...] = _post_norm(h_ref[...], f, mod_ref[5:6, :], lng_ref[1:2, :], lnb_ref[1:2, :])


def _ffn_dense(a2, h1, mods, w1, w3, w2, ln_g, ln_b, layer, n_rows):
    n_tiles = n_rows // TM
    tiles_per_batch = SEQ // TM
    n_lat = NL // TM
    idx = layer // 2
    resident = pl.Buffered(1)
    return pl.pallas_call(
        _ffn_dense_kernel,
        grid=(n_tiles,),
        in_specs=[
            pl.BlockSpec((TM, D), lambda i: (i, 0)),
            pl.BlockSpec((TM, D), lambda i: (i, 0)),
            pl.BlockSpec((None, None, 6, D),
                         lambda i: (layer, _mod_row(i, tiles_per_batch, n_lat), 0, 0)),
            pl.BlockSpec((None, D, D_FF_DENSE), lambda i: (idx, 0, 0), pipeline_mode=resident),
            pl.BlockSpec((None, D, D_FF_DENSE), lambda i: (idx, 0, 0), pipeline_mode=resident),
            pl.BlockSpec((None, D_FF_DENSE, D), lambda i: (idx, 0, 0), pipeline_mode=resident),
            pl.BlockSpec((None, 2, D), lambda i: (layer, 0, 0)),
            pl.BlockSpec((None, 2, D), lambda i: (layer, 0, 0)),
        ],
        out_specs=pl.BlockSpec((TM, D), lambda i: (i, 0)),
        out_shape=jax.ShapeDtypeStruct((n_rows, D), F32),
        scratch_shapes=[pltpu.VMEM((TM, D_FF_DENSE), BF16)],
        compiler_params=_cparams(1),
        name="ffn_dense",
    )(a2, h1, mods, w1, w3, w2, ln_g, ln_b)


def _experts_kernel(pos_ref, te_ref, tv_ref, cnt_ref, off_ref, end_ref,
                    a_hbm, w1_ref, w3_ref, w2_ref, o_ref, src_ref, xg_ref, t_ref, sem,
                    *, n_assign):
    i = pl.program_id(0)
    n = pl.num_programs(0)

    def issue(tile, slot):
        base = tile * MOE_TM

        def body(r, carry):
            s = pl.multiple_of(src_ref[base + r] * ROW_TILE, ROW_TILE)
            d = pl.multiple_of(r * ROW_TILE, ROW_TILE)
            pltpu.make_async_copy(a_hbm.at[pl.ds(s, ROW_TILE)],
                                  xg_ref.at[slot, pl.ds(d, ROW_TILE)], sem.at[slot]).start()
            return carry

        lax.fori_loop(0, MOE_TM, body, 0, unroll=8)

    @pl.when(i == 0)
    def _():
        for e in range(N_EXPERTS):
            def clear(s, carry):
                src_ref[s] = 0
                return carry
            lax.fori_loop(off_ref[e] + cnt_ref[e], end_ref[e], clear, 0)

        def fill(a, carry):
            src_ref[pos_ref[a]] = lax.shift_right_logical(a, 1)
            return carry

        lax.fori_loop(0, n_assign, fill, 0, unroll=8)
        issue(0, 0)

    nxt = jnp.minimum(i + 1, n - 1)

    @pl.when(jnp.logical_and(i + 1 < n, tv_ref[nxt] > 0))
    def _():
        issue(i + 1, (i + 1) % 2)

    valid = tv_ref[i] > 0

    @pl.when(valid)
    def _():
        slot = i % 2
        pltpu.make_async_copy(a_hbm.at[pl.ds(0, MOE_TM * ROW_TILE)], xg_ref.at[slot],
                              sem.at[slot]).wait()
        x = _load_row_tiles(xg_ref.at[slot], MOE_TM).astype(BF16)
        _swiglu_hidden(x, w1_ref, w3_ref, t_ref, D_FF_EXPERT)
        y = jnp.dot(t_ref[...], w2_ref[...], preferred_element_type=F32)
        _store_row_tiles(o_ref, y)

    @pl.when(jnp.logical_not(valid))
    def _():
        o_ref[...] = jnp.zeros_like(o_ref)


def _experts(plan, a2r, w1, w3, w2, idx, n_rows, n_slots):
    n_tiles = n_slots // MOE_TM
    resident = pl.Buffered(1)

    def w_idx(i, pos, te, *_):
        return (idx, te[i], 0, 0)

    return pl.pallas_call(
        functools.partial(_experts_kernel, n_assign=2 * n_rows),
        grid_spec=pltpu.PrefetchScalarGridSpec(
            num_scalar_prefetch=6,
            grid=(n_tiles,),
            in_specs=[
                pl.BlockSpec(memory_space=pl.ANY),
                pl.BlockSpec((None, None, D, D_FF_EXPERT), w_idx, pipeline_mode=resident),
                pl.BlockSpec((None, None, D, D_FF_EXPERT), w_idx, pipeline_mode=resident),
                pl.BlockSpec((None, None, D_FF_EXPERT, D), w_idx, pipeline_mode=resident),
            ],
            out_specs=pl.BlockSpec((MOE_TM * ROW_TILE, LANES), lambda i, *_: (i, 0)),
            scratch_shapes=[
                pltpu.SMEM((n_slots,), I32),
                pltpu.VMEM((2, MOE_TM * ROW_TILE, LANES), F32),
                pltpu.VMEM((MOE_TM, D_FF_EXPERT), BF16),
                pltpu.SemaphoreType.DMA((2,)),
            ],
        ),
        out_shape=jax.ShapeDtypeStruct((n_slots * ROW_TILE, LANES), F32),
        compiler_params=_cparams(1),
        name="moe_experts",
    )(*plan, a2r, w1, w3, w2)


def _combine_kernel(pos_ref, y_hbm, h_ref, rw_ref, mod_ref, lng_ref, lnb_ref, o_ref, buf, sem):
    i = pl.program_id(0)
    n = pl.num_programs(0)

    def issue(step, slot):
        base = step * COMBINE_ROWS

        def body(r, carry):
            d = pl.multiple_of(r * ROW_TILE, ROW_TILE)
            for k in range(2):
                p = pl.multiple_of(pos_ref[2 * (base + r) + k] * ROW_TILE, ROW_TILE)
                pltpu.make_async_copy(y_hbm.at[pl.ds(p, ROW_TILE)],
                                      buf.at[slot, k, pl.ds(d, ROW_TILE)], sem.at[slot]).start()
            return carry

        lax.fori_loop(0, COMBINE_ROWS, body, 0, unroll=4)

    @pl.when(i == 0)
    def _():
        issue(0, 0)

    @pl.when(i + 1 < n)
    def _():
        issue(i + 1, (i + 1) % 2)

    slot = i % 2
    for k in range(2):
        pltpu.make_async_copy(y_hbm.at[pl.ds(0, COMBINE_ROWS * ROW_TILE)], buf.at[slot, k],
                              sem.at[slot]).wait()
    w = rw_ref[...]
    f = (w[:, 0:1] * _load_row_tiles(buf.at[slot, 0], COMBINE_ROWS)
         + w[:, 1:2] * _load_row_tiles(buf.at[slot, 1], COMBINE_ROWS))
    o_ref[...] = _post_norm(h_ref[...], f, mod_ref[5:6, :], lng_ref[1:2, :], lnb_ref[1:2, :])


def _combine(pos, y, h1, rw, mods, ln_g, ln_b, layer, n_rows):
    n_tiles = n_rows // COMBINE_ROWS
    tiles_per_batch = SEQ // COMBINE_ROWS
    n_lat = NL // COMBINE_ROWS
    return pl.pallas_call(
        _combine_kernel,
        grid_spec=pltpu.PrefetchScalarGridSpec(
            num_scalar_prefetch=1,
            grid=(n_tiles,),
            in_specs=[
                pl.BlockSpec(memory_space=pl.ANY),
                pl.BlockSpec((COMBINE_ROWS, D), lambda i, p: (i, 0)),
                pl.BlockSpec((COMBINE_ROWS, LANES), lambda i, p: (i, 0)),
                pl.BlockSpec((None, None, 6, D),
                             lambda i, p: (layer, _mod_row(i, tiles_per_batch, n_lat), 0, 0)),
                pl.BlockSpec((None, 2, D), lambda i, p: (layer, 0, 0)),
                pl.BlockSpec((None, 2, D), lambda i, p: (layer, 0, 0)),
            ],
            out_specs=pl.BlockSpec((COMBINE_ROWS, D), lambda i, p: (i, 0)),
            scratch_shapes=[pltpu.VMEM((2, 2, COMBINE_ROWS * ROW_TILE, LANES), F32),
                            pltpu.SemaphoreType.DMA((2,))],
        ),
        out_shape=jax.ShapeDtypeStruct((n_rows, D), F32),
        compiler_params=_cparams(1),
        name="moe_combine",
    )(pos, y, h1, rw, mods, ln_g, ln_b)


def _route_plan(eid, n_rows, n_slots):
    n_tiles = n_slots // MOE_TM
    e_flat = eid.reshape(-1)
    onehot = (e_flat[:, None] == jnp.arange(N_EXPERTS, dtype=I32)[None, :]).astype(I32)
    csum = jnp.cumsum(onehot, axis=0)
    rank = jnp.sum((csum - onehot) * onehot, axis=1)
    count = csum[-1]
    padded = ((count + MOE_TM - 1) // MOE_TM) * MOE_TM
    ends = jnp.cumsum(padded)
    offs = ends - padded
    pos = jnp.sum(onehot * offs[None, :], axis=1) + rank
    tile_start = jnp.arange(n_tiles, dtype=I32) * MOE_TM
    tile_e = jnp.minimum(jnp.sum((tile_start[:, None] >= ends[None, :]).astype(I32), axis=1),
                         N_EXPERTS - 1)
    tile_v = (tile_start < ends[-1]).astype(I32)
    return (pos.astype(I32), tile_e.astype(I32), tile_v, count.astype(I32), offs.astype(I32),
            ends.astype(I32))


def _moe(a2r, h1, ri, rw, mods, w1, w3, w2, ln_g, ln_b, layer, n_rows):
    n_slots = 2 * n_rows + N_EXPERTS * MOE_TM
    plan = _route_plan(ri[:, :2], n_rows, n_slots)
    ys = _experts(plan, a2r, w1, w3, w2, layer // 2, n_rows, n_slots)
    return _combine(plan[0], ys, h1, rw, mods, ln_g, ln_b, layer, n_rows)


def kernel(x, c, ctx, c_ctx, w_ada, b_ada, w_in, w_pool, pool_scale, sink, w_out, ln_g, ln_b,
           dense_w1, dense_w3, dense_w2, router, moe_w1, moe_w3, moe_w2):
    cond = jnp.concatenate([c, c_ctx[None, :], jnp.zeros((MOD_ROWS - B - 1, D), F32)], axis=0)
    mods = _ada_tables(cond, w_ada, b_ada)
    cos_t, sin_t = _rope_tables()

    w_in_b = w_in.astype(BF16)
    w_pool_b = w_pool.astype(BF16)
    w_out_b = w_out.astype(BF16)
    dw1, dw3, dw2 = dense_w1.astype(BF16), dense_w3.astype(BF16), dense_w2.astype(BF16)
    mw1, mw3, mw2 = moe_w1.astype(BF16), moe_w3.astype(BF16), moe_w2.astype(BF16)
    router_pad = jnp.pad(router, ((0, 0), (0, 0), (0, LANES - N_EXPERTS)))
    pool_scale3 = pool_scale.reshape(DEPTH, 1, POOL_W)

    h = jnp.concatenate([x.reshape(NL, D), ctx.reshape(NC, D)], axis=0)
    for l in range(DEPTH):
        last = l == DEPTH - 1
        n_rows = NL if last else NT
        q, kk, vv, u = _inproj(h, mods, w_in_b, cos_t, sin_t, l)
        mix = _mixer_latent(q, kk, vv, u, sink[l], w_pool_b, pool_scale3, l)
        mix_c = mix if last else _mixer_context(q, kk, vv, u, sink[l], w_pool_b, pool_scale3, l)
        if l % 2 == 0:
            h1, a2 = _outproj(mix, mix_c, h, mods, w_out_b, ln_g, ln_b, None, l, n_rows)
            h = _ffn_dense(a2, h1, mods, dw1, dw3, dw2, ln_g, ln_b, l, n_rows)
        else:
            h1, a2, ri, rw = _outproj(mix, mix_c, h, mods, w_out_b, ln_g, ln_b, router_pad, l,
                                      n_rows)
            h = _moe(a2, h1, ri, rw, mods, mw1, mw3, mw2, ln_g, ln_b, l, n_rows)
    return h.reshape(B, SEQ, D)
```

```python
import functools

import jax
import jax.numpy as jnp
from jax import lax
from jax.experimental import pallas as pl
from jax.experimental.pallas import tpu as pltpu

F32 = jnp.float32
BF16 = jnp.bfloat16
I32 = jnp.int32

D = 1024
B = 8
SEQ = 2048
DEPTH = 4
CTX = 256
GRID_W = 64
HEAD_DIM = 64
N_Q_HEADS = 8
N_KV_HEADS = 2
ATTN_W = N_Q_HEADS * HEAD_DIM
KV_W = N_KV_HEADS * HEAD_DIM
POOL_WINDOWS = (2, 4, 8, 16)
POOL_W = D - ATTN_W
POOL_G = POOL_W // len(POOL_WINDOWS)
IN_W = ATTN_W + 2 * KV_W + POOL_W
WINDOW = 128
ROPE_BASE = 10000.0
ROPE_FREQS = HEAD_DIM // 4
D_FF_DENSE = 2816
N_EXPERTS = 8
D_FF_EXPERT = 3584
ALPHA = (2 * DEPTH) ** 0.25
LN_EPS = 1e-6
NEG_INF = -1e30
LOG2E = 1.4426950408889634

NL = B * SEQ
NC = B * CTX
NT = NL + NC
MOD_ROWS = 16
LANES = 128

TM = 512
TQ = 128
MOE_TM = 512
FF_CHUNK = 1024
COMBINE_ROWS = 256

VMEM_LIMIT = 56 * 1024 * 1024


def _cparams(n_axes, vmem=VMEM_LIMIT):
    return pltpu.CompilerParams(
        dimension_semantics=("arbitrary",) * n_axes, vmem_limit_bytes=vmem)


def _layernorm(x):
    mu = jnp.mean(x, axis=-1, keepdims=True)
    xc = x - mu
    var = jnp.mean(xc * xc, axis=-1, keepdims=True)
    return xc * lax.rsqrt(var + LN_EPS)


def _mod_row(tile, tiles_per_batch, n_latent_tiles):
    return jnp.where(tile < n_latent_tiles, tile // tiles_per_batch, B)


ADA_TN = 1536


def _ada_kernel(s_ref, w_ref, b_ref, o_ref):
    s = jax.nn.silu(s_ref[...]).astype(BF16)
    w = w_ref[...].astype(BF16)
    o_ref[...] = jnp.dot(s, w, preferred_element_type=F32) + b_ref[...]


def _ada_tables(cond, w_ada, b_ada):
    n_col = (6 * D) // ADA_TN
    out = pl.pallas_call(
        _ada_kernel,
        grid=(DEPTH, n_col),
        in_specs=[
            pl.BlockSpec((MOD_ROWS, D), lambda l, j: (0, 0)),
            pl.BlockSpec((None, D, ADA_TN), lambda l, j: (l, 0, j)),
            pl.BlockSpec((None, 1, ADA_TN), lambda l, j: (l, 0, j)),
        ],
        out_specs=pl.BlockSpec((None, MOD_ROWS, ADA_TN), lambda l, j: (l, 0, j)),
        out_shape=jax.ShapeDtypeStruct((DEPTH, MOD_ROWS, 6 * D), F32),
        compiler_params=_cparams(2),
        name="ada_tables",
    )(cond, w_ada, b_ada.reshape(DEPTH, 1, 6 * D))
    return out.reshape(DEPTH, MOD_ROWS, 6, D)


def _inproj_kernel(h_ref, mod_ref, w_ref, cos_ref, sin_ref,
                   q_ref, kk_ref, vv_ref, u_ref):
    a = _layernorm(h_ref[...]) * (1.0 + mod_ref[1:2, :]) + mod_ref[0:1, :]
    p = jnp.dot(a.astype(BF16), w_ref[...], preferred_element_type=F32)
    cos = cos_ref[...]
    sin = sin_ref[...]
    lane = lax.broadcasted_iota(I32, (1, LANES), 1)
    first_half = (lane % 32) < 16
    low = lane < HEAD_DIM

    def rope(x):
        swapped = jnp.where(first_half, pltpu.roll(x, LANES - 16, 1), pltpu.roll(x, 16, 1))
        return x * cos + swapped * sin

    for c in range(ATTN_W // LANES):
        qc = rope(p[:, c * LANES:(c + 1) * LANES]) * (HEAD_DIM ** -0.5 * LOG2E)
        q_ref[:, c * LANES:(c + 1) * LANES] = qc.astype(BF16)
    k = rope(p[:, ATTN_W:ATTN_W + KV_W])
    k_sw = pltpu.roll(k, HEAD_DIM, 1)
    kk_ref[:, 0:LANES] = jnp.where(low, k, k_sw).astype(BF16)
    kk_ref[:, LANES:2 * LANES] = jnp.where(low, k_sw, k).astype(BF16)
    v = p[:, ATTN_W + KV_W:ATTN_W + 2 * KV_W]
    v_sw = pltpu.roll(v, HEAD_DIM, 1)
    vv_ref[:, 0:LANES] = jnp.where(low, v, v_sw).astype(BF16)
    vv_ref[:, LANES:2 * LANES] = jnp.where(low, v_sw, v).astype(BF16)
    u_ref[...] = p[:, ATTN_W + 2 * KV_W:]


def _inproj(h, mods, w_in, cos_t, sin_t, layer):
    n_tiles = NT // TM
    tiles_per_batch = SEQ // TM
    n_lat = NL // TM

    def rope_idx(i):
        return (jnp.where(i < n_lat, i % tiles_per_batch, tiles_per_batch), 0)

    return pl.pallas_call(
        _inproj_kernel,
        grid=(n_tiles,),
        in_specs=[
            pl.BlockSpec((TM, D), lambda i: (i, 0)),
            pl.BlockSpec((None, None, 6, D),
                         lambda i: (layer, _mod_row(i, tiles_per_batch, n_lat), 0, 0)),
            pl.BlockSpec((None, D, IN_W), lambda i: (layer, 0, 0)),
            pl.BlockSpec((TM, LANES), rope_idx),
            pl.BlockSpec((TM, LANES), rope_idx),
        ],
        out_specs=[
            pl.BlockSpec((TM, ATTN_W), lambda i: (i, 0)),
            pl.BlockSpec((TM, 2 * KV_W), lambda i: (i, 0)),
            pl.BlockSpec((TM, 2 * KV_W), lambda i: (i, 0)),
            pl.BlockSpec((TM, POOL_W), lambda i: (i, 0)),
        ],
        out_shape=[
            jax.ShapeDtypeStruct((NT, ATTN_W), BF16),
            jax.ShapeDtypeStruct((NT, 2 * KV_W), BF16),
            jax.ShapeDtypeStruct((NT, 2 * KV_W), BF16),
            jax.ShapeDtypeStruct((NT, POOL_W), F32),
        ],
        compiler_params=_cparams(1),
        name="inproj",
    )(h, mods, w_in, cos_t, sin_t)


def _rope_tables():
    rows = SEQ // GRID_W
    row = jnp.repeat(jnp.arange(rows, dtype=F32), GRID_W)
    col = jnp.tile(jnp.arange(GRID_W, dtype=F32), rows)
    inv = ROPE_BASE ** (-jnp.arange(ROPE_FREQS, dtype=F32) / ROPE_FREQS)
    ang_r = row[:, None] * inv[None, :]
    ang_c = col[:, None] * inv[None, :]
    cr, sr, cc, sc = jnp.cos(ang_r), jnp.sin(ang_r), jnp.cos(ang_c), jnp.sin(ang_c)
    cos_h = jnp.concatenate([cr, cr, cc, cc], axis=-1)
    sin_h = jnp.concatenate([-sr, sr, -sc, sc], axis=-1)
    cos_t = jnp.concatenate([jnp.tile(cos_h, (1, 2)), jnp.ones((TM, LANES), F32)], axis=0)
    sin_t = jnp.concatenate([jnp.tile(sin_h, (1, 2)), jnp.zeros((TM, LANES), F32)], axis=0)
    return cos_t, sin_t


def _pool_group(top, mid, bot, t0, seq_len, width):
    rows = mid.shape[0]
    slab = jnp.concatenate([top, mid, bot], axis=0)
    half = width // 2
    s = slab
    span = 1
    while span < width:
        n = s.shape[0] - span
        s = s[0:n] + s[span:span + n]
        span *= 2
    start = 8 - half
    total = s[start:start + rows]
    pos = t0 + lax.broadcasted_iota(I32, (rows, 1), 0)
    hi = jnp.minimum(pos + half, seq_len)
    lo = jnp.maximum(pos - half, 0)
    cnt = (hi - lo).astype(F32)
    return total / cnt - mid


def _mixer_kernel(sink_ref, q_ref, kk_ref, vv_ref, kkc_ref, vvc_ref, u_ref, wp_ref, ps_ref,
                  o_ref, *, tq, seq_len, band):
    t0 = pl.program_id(1) * tq
    lane = lax.broadcasted_iota(I32, (1, LANES), 1)
    low = lane < HEAD_DIM

    if band:
        n_band = tq + 2 * WINDOW
        ks = jnp.clip(t0 - WINDOW, 0, seq_len - n_band)
        ks = pl.multiple_of(ks, LANES)
        kk = jnp.concatenate([kk_ref[pl.ds(ks, n_band), :], kkc_ref[...]], axis=0)
        vv = jnp.concatenate([vv_ref[pl.ds(ks, n_band), :], vvc_ref[...]], axis=0)
        qpos = t0 + lax.broadcasted_iota(I32, (tq, n_band), 0)
        kpos = ks + lax.broadcasted_iota(I32, (tq, n_band), 1)
        bias = jnp.where(jnp.abs(kpos - qpos) <= WINDOW, 0.0, NEG_INF)
    else:
        n_band = 0
        kk = kkc_ref[...]
        vv = vvc_ref[...]
        bias = None
    nk = kk.shape[0]
    zero = jnp.zeros_like(kk[:, 0:LANES])
    low_f = jnp.broadcast_to(jnp.where(low, 1.0, 0.0), (nk, LANES))
    ones_st = jnp.concatenate([low_f, 1.0 - low_f], axis=0).astype(BF16)

    for g in range(N_KV_HEADS):
        kg = kk[:, g * LANES:(g + 1) * LANES]
        vg = vv[:, g * LANES:(g + 1) * LANES]
        k_st = jnp.concatenate([jnp.where(low, kg, zero), jnp.where(low, zero, kg)], axis=0)
        v_st = jnp.concatenate([jnp.where(low, vg, zero), jnp.where(low, zero, vg)], axis=0)
        v_st = jnp.concatenate([v_st, ones_st], axis=1)
        for pr in range(2):
            j = 2 * g + pr
            qp = q_ref[:, j * LANES:(j + 1) * LANES]
            s = lax.dot_general(qp, k_st, (((1,), (1,)), ((), ())), preferred_element_type=F32)
            ps = []
            sink_terms = []
            for hh in range(2):
                sink = sink_ref[2 * j + hh] * LOG2E
                sh = s[:, hh * nk:(hh + 1) * nk]
                if bias is not None:
                    parts = [sh[:, :n_band] + bias, sh[:, n_band:]]
                else:
                    parts = [sh]
                m = sink
                for part in parts:
                    m = jnp.maximum(jnp.max(part, axis=-1, keepdims=True), m)
                sink_terms.append(jnp.exp2(sink - m))
                ps += [jnp.exp2(part - m).astype(BF16) for part in parts]
            od = jnp.dot(jnp.concatenate(ps, axis=1), v_st, preferred_element_type=F32)
            den = od[:, LANES:] + jnp.where(low, sink_terms[0], sink_terms[1])
            o_ref[:, j * LANES:(j + 1) * LANES] = (od[:, :LANES] / den).astype(BF16)

    zeros8 = jnp.zeros((8, POOL_W), F32)
    if band:
        t0a = pl.multiple_of(t0, 8)
        mid = u_ref[pl.ds(t0a, tq), :]
        top_s = pl.multiple_of(jnp.maximum(t0 - 8, 0), 8)
        bot_s = pl.multiple_of(jnp.minimum(t0 + tq, seq_len - 8), 8)
        top = jnp.where(t0 > 0, u_ref[pl.ds(top_s, 8), :], zeros8)
        bot = jnp.where(t0 + tq < seq_len, u_ref[pl.ds(bot_s, 8), :], zeros8)
    else:
        mid = u_ref[...]
        top = zeros8
        bot = zeros8
    for g, width in enumerate(POOL_WINDOWS):
        sl = slice(g * POOL_G, (g + 1) * POOL_G)
        diff = _pool_group(top[:, sl], mid[:, sl], bot[:, sl], t0, seq_len, width)
        mixed = jnp.dot(diff.astype(BF16), wp_ref[g], preferred_element_type=F32)
        mixed = mixed * ps_ref[:, sl]
        o_ref[:, ATTN_W + g * POOL_G:ATTN_W + (g + 1) * POOL_G] = mixed.astype(BF16)


def _mixer_latent(q, kk, vv, u, sink_l, w_pool, pool_scale, layer):
    nq = SEQ // TQ
    ctx_blk = NL // CTX
    kern = functools.partial(_mixer_kernel, tq=TQ, seq_len=SEQ, band=True)
    return pl.pallas_call(
        kern,
        grid=(B, nq),
        in_specs=[
            pl.BlockSpec(memory_space=pltpu.SMEM),
            pl.BlockSpec((TQ, ATTN_W), lambda b, t: (b * nq + t, 0)),
            pl.BlockSpec((SEQ, 2 * KV_W), lambda b, t: (b, 0)),
            pl.BlockSpec((SEQ, 2 * KV_W), lambda b, t: (b, 0)),
            pl.BlockSpec((CTX, 2 * KV_W), lambda b, t: (ctx_blk + b, 0)),
            pl.BlockSpec((CTX, 2 * KV_W), lambda b, t: (ctx_blk + b, 0)),
            pl.BlockSpec((SEQ, POOL_W), lambda b, t: (b, 0)),
            pl.BlockSpec((None, len(POOL_WINDOWS), POOL_G, POOL_G), lambda b, t: (layer, 0, 0, 0)),
            pl.BlockSpec((None, 1, POOL_W), lambda b, t: (layer, 0, 0)),
        ],
        out_specs=pl.BlockSpec((TQ, D), lambda b, t: (b * nq + t, 0)),
        out_shape=jax.ShapeDtypeStruct((NL, D), BF16),
        compiler_params=_cparams(2),
        name="mixer_latent",
    )(sink_l, q, kk, vv, kk, vv, u, w_pool, pool_scale)


def _mixer_context(q, kk, vv, u, sink_l, w_pool, pool_scale, layer):
    ctx_blk = NL // CTX
    kern = functools.partial(_mixer_kernel, tq=CTX, seq_len=CTX, band=False)

    def body(sink_ref, q_ref, kkc_ref, vvc_ref, u_ref, wp_ref, ps_ref, o_ref):
        kern(sink_ref, q_ref, None, None, kkc_ref, vvc_ref, u_ref, wp_ref, ps_ref, o_ref)

    return pl.pallas_call(
        body,
        grid=(B, 1),
        in_specs=[
            pl.BlockSpec(memory_space=pltpu.SMEM),
            pl.BlockSpec((CTX, ATTN_W), lambda b, t: (ctx_blk + b, 0)),
            pl.BlockSpec((CTX, 2 * KV_W), lambda b, t: (ctx_blk + b, 0)),
            pl.BlockSpec((CTX, 2 * KV_W), lambda b, t: (ctx_blk + b, 0)),
            pl.BlockSpec((CTX, POOL_W), lambda b, t: (ctx_blk + b, 0)),
            pl.BlockSpec((None, len(POOL_WINDOWS), POOL_G, POOL_G), lambda b, t: (layer, 0, 0, 0)),
            pl.BlockSpec((None, 1, POOL_W), lambda b, t: (layer, 0, 0)),
        ],
        out_specs=pl.BlockSpec((CTX, D), lambda b, t: (b, 0)),
        out_shape=jax.ShapeDtypeStruct((NC, D), BF16),
        compiler_params=_cparams(2),
        name="mixer_context",
    )(sink_l, q, kk, vv, u, w_pool, pool_scale)


def _post_norm(h, upd, gate, g, b):
    return _layernorm(ALPHA * h + gate * upd) * g + b


ROW_TILE = D // LANES


def _store_row_tiles(ref, x):
    rows = x.shape[0]
    for c in range(ROW_TILE):
        ref[pl.ds(c, rows, stride=ROW_TILE), :] = x[:, c * LANES:(c + 1) * LANES]


def _load_row_tiles(ref, rows):
    return jnp.concatenate(
        [ref[pl.ds(c, rows, stride=ROW_TILE), :] for c in range(ROW_TILE)], axis=1)


def _route_top2(a, router_ref, ri_ref, rw_ref):
    a_hi = a.astype(BF16)
    a_lo = (a - a_hi.astype(F32)).astype(BF16)
    t_hi = jnp.dot(a_hi, router_ref[...], preferred_element_type=F32)
    t_lo = jnp.dot(a_lo, router_ref[:, :LANES], preferred_element_type=F32)
    logits = t_hi[:, :LANES] + t_hi[:, LANES:] + t_lo
    lane = lax.broadcasted_iota(I32, logits.shape, 1)
    lane_f = lane.astype(F32)
    logits = jnp.where(lane < N_EXPERTS, logits, -jnp.inf)
    m1 = jnp.max(logits, axis=-1, keepdims=True)
    i1 = jnp.min(jnp.where(logits == m1, lane_f, float(LANES)), axis=-1, keepdims=True)
    rest = jnp.where(lane_f == i1, -jnp.inf, logits)
    m2 = jnp.max(rest, axis=-1, keepdims=True)
    i2 = jnp.min(jnp.where(rest == m2, lane_f, float(LANES)), axis=-1, keepdims=True)
    e = jnp.exp(m2 - m1)
    w1 = 1.0 / (1.0 + e)
    w2 = e / (1.0 + e)
    ri_ref[...] = jnp.where(lane == 0, i1, jnp.where(lane == 1, i2, 0.0)).astype(I32)
    rw_ref[...] = jnp.where(lane == 0, w1, jnp.where(lane == 1, w2, 0.0))


def _outproj_kernel(mixl_ref, mixc_ref, h_ref, mod_ref, w_ref, lng_ref, lnb_ref, *rest,
                    moe, n_lat, has_ctx):
    if moe:
        router_ref, h1_ref, a2_ref, ri_ref, rw_ref = rest
    else:
        h1_ref, a2_ref = rest
    mix = mixl_ref[...]
    if has_ctx:
        mix = jnp.where(pl.program_id(0) >= n_lat, mixc_ref[...], mix)
    y = jnp.dot(mix, w_ref[...], preferred_element_type=F32)
    h1 = _post_norm(h_ref[...], y, mod_ref[2:3, :], lng_ref[0:1, :], lnb_ref[0:1, :])
    h1_ref[...] = h1
    a2 = _layernorm(h1) * (1.0 + mod_ref[4:5, :]) + mod_ref[3:4, :]
    if moe:
        _store_row_tiles(a2_ref, a2)
        _route_top2(a2, router_ref, ri_ref, rw_ref)
    else:
        a2_ref[...] = a2.astype(BF16)


def _outproj(mix_lat, mix_ctx, h, mods, w_out, ln_g, ln_b, router_pad, layer, n_rows):
    n_tiles = n_rows // TM
    tiles_per_batch = SEQ // TM
    n_lat = NL // TM
    moe = router_pad is not None
    has_ctx = n_rows > NL
    in_specs = [
        pl.BlockSpec((TM, D), lambda i: (jnp.minimum(i, n_lat - 1), 0)),
        pl.BlockSpec((TM, D), lambda i: (jnp.maximum(i - n_lat, 0), 0)),
        pl.BlockSpec((TM, D), lambda i: (i, 0)),
        pl.BlockSpec((None, None, 6, D),
                     lambda i: (layer, _mod_row(i, tiles_per_batch, n_lat), 0, 0)),
        pl.BlockSpec((None, D, D), lambda i: (layer, 0, 0)),
        pl.BlockSpec((None, 2, D), lambda i: (layer, 0, 0)),
        pl.BlockSpec((None, 2, D), lambda i: (layer, 0, 0)),
    ]
    out_specs = [pl.BlockSpec((TM, D), lambda i: (i, 0)), pl.BlockSpec((TM, D), lambda i: (i, 0))]
    out_shape = [jax.ShapeDtypeStruct((n_rows, D), F32), jax.ShapeDtypeStruct((n_rows, D), BF16)]
    args = [mix_lat, mix_ctx, h, mods, w_out, ln_g, ln_b]
    if moe:
        in_specs.append(pl.BlockSpec((None, D, 2 * LANES), lambda i: (layer // 2, 0, 0)))
        out_specs[1] = pl.BlockSpec((TM * ROW_TILE, LANES), lambda i: (i, 0))
        out_shape[1] = jax.ShapeDtypeStruct((n_rows * ROW_TILE, LANES), F32)
        out_specs += [pl.BlockSpec((TM, LANES), lambda i: (i, 0)),
                      pl.BlockSpec((TM, LANES), lambda i: (i, 0))]
        out_shape += [jax.ShapeDtypeStruct((n_rows, LANES), I32),
                      jax.ShapeDtypeStruct((n_rows, LANES), F32)]
        args.append(router_pad)
    return pl.pallas_call(
        functools.partial(_outproj_kernel, moe=moe, n_lat=n_lat, has_ctx=has_ctx),
        grid=(n_tiles,),
        in_specs=in_specs,
        out_specs=out_specs,
        out_shape=out_shape,
        compiler_params=_cparams(1),
        name="outproj_moe" if moe else "outproj",
    )(*args)


def _swiglu_hidden(x, w1_ref, w3_ref, t_ref, d_ff):
    for c0 in range(0, d_ff, FF_CHUNK):
        c1 = min(c0 + FF_CHUNK, d_ff)
        g = jnp.dot(x, w1_ref[:, c0:c1], preferred_element_type=F32)
        u = jnp.dot(x, w3_ref[:, c0:c1], preferred_element_type=F32)
        t_ref[:, c0:c1] = (jax.nn.silu(g) * u).astype(BF16)


def _ffn_dense_kernel(a_ref, h_ref, mod_ref, w1_ref, w3_ref, w2_ref, lng_ref, lnb_ref, o_ref,
                      t_ref):
    _swiglu_hidden(a_ref[...], w1_ref, w3_ref, t_ref, D_FF_DENSE)
    f = jnp.dot(t_ref[...], w2_ref[...], preferred_element_type=F32)
    o_ref[...] = _post_norm(h_ref[...], f, mod_ref[5:6, :], lng_ref[1:2, :], lnb_ref[1:2, :])


def _ffn_dense(a2, h1, mods, w1, w3, w2, ln_g, ln_b, layer, n_rows):
    n_tiles = n_rows // TM
    tiles_per_batch = SEQ // TM
    n_lat = NL // TM
    idx = layer // 2
    resident = pl.Buffered(1)
    return pl.pallas_call(
        _ffn_dense_kernel,
        grid=(n_tiles,),
        in_specs=[
            pl.BlockSpec((TM, D), lambda i: (i, 0)),
            pl.BlockSpec((TM, D), lambda i: (i, 0)),
            pl.BlockSpec((None, None, 6, D),
                         lambda i: (layer, _mod_row(i, tiles_per_batch, n_lat), 0, 0)),
            pl.BlockSpec((None, D, D_FF_DENSE), lambda i: (idx, 0, 0), pipeline_mode=resident),
            pl.BlockSpec((None, D, D_FF_DENSE), lambda i: (idx, 0, 0), pipeline_mode=resident),
            pl.BlockSpec((None, D_FF_DENSE, D), lambda i: (idx, 0, 0), pipeline_mode=resident),
            pl.BlockSpec((None, 2, D), lambda i: (layer, 0, 0)),
            pl.BlockSpec((None, 2, D), lambda i: (layer, 0, 0)),
        ],
        out_specs=pl.BlockSpec((TM, D), lambda i: (i, 0)),
        out_shape=jax.ShapeDtypeStruct((n_rows, D), F32),
        scratch_shapes=[pltpu.VMEM((TM, D_FF_DENSE), BF16)],
        compiler_params=_cparams(1),
        name="ffn_dense",
    )(a2, h1, mods, w1, w3, w2, ln_g, ln_b)


def _experts_kernel(pos_ref, te_ref, tv_ref, cnt_ref, off_ref, end_ref,
                    a_hbm, w1_ref, w3_ref, w2_ref, o_ref, src_ref, xg_ref, t_ref, sem,
                    *, n_assign, n_slots):
    i = pl.program_id(0)
    n = pl.num_programs(0)

    def start_row(tile, slot, r):
        s = pl.multiple_of(src_ref[tile * MOE_TM + r] * ROW_TILE, ROW_TILE)
        d = r * ROW_TILE if isinstance(r, int) else pl.multiple_of(r * ROW_TILE, ROW_TILE)
        pltpu.make_async_copy(a_hbm.at[pl.ds(s, ROW_TILE)],
                              xg_ref.at[slot, pl.ds(d, ROW_TILE)], sem.at[slot]).start()

    def wait_tile(slot):
        pltpu.make_async_copy(a_hbm.at[pl.ds(0, MOE_TM * ROW_TILE)], xg_ref.at[slot],
                              sem.at[slot]).wait()

    def clear(s, carry):
        src_ref[s] = 0
        return carry

    @pl.when(i == 0)
    def _():
        for e in range(N_EXPERTS):
            lax.fori_loop(off_ref[e] + cnt_ref[e], end_ref[e], clear, 0)
        lax.fori_loop(end_ref[N_EXPERTS - 1], n_slots, clear, 0)

        def fill(a, carry):
            src_ref[pos_ref[a]] = lax.shift_right_logical(a, 1)
            return carry

        lax.fori_loop(0, n_assign, fill, 0, unroll=8)

        def first(r, carry):
            start_row(0, 0, r)
            return carry

        lax.fori_loop(0, MOE_TM, first, 0, unroll=8)

    valid = tv_ref[i] > 0
    slot = i % 2

    @pl.when(jnp.logical_or(i == 0, tv_ref[jnp.maximum(i - 1, 0)] > 0))
    def _():
        wait_tile(slot)

    @pl.when(valid)
    def _():
        x = _load_row_tiles(xg_ref.at[slot], MOE_TM).astype(BF16)
        nxt_tile = lax.rem(i + 1, n)
        chunks = [(c0, min(c0 + FF_CHUNK, D_FF_EXPERT)) for c0 in range(0, D_FF_EXPERT, FF_CHUNK)]
        per_chunk = MOE_TM // len(chunks)
        for ci, (c0, c1) in enumerate(chunks):
            for r in range(ci * per_chunk, (ci + 1) * per_chunk):
                start_row(nxt_tile, 1 - slot, r)
            g = jnp.dot(x, w1_ref[:, c0:c1], preferred_element_type=F32)
            u = jnp.dot(x, w3_ref[:, c0:c1], preferred_element_type=F32)
            t_ref[:, c0:c1] = (jax.nn.silu(g) * u).astype(BF16)
        y = jnp.dot(t_ref[...], w2_ref[...], preferred_element_type=F32)
        _store_row_tiles(o_ref, y)

    @pl.when(jnp.logical_not(valid))
    def _():
        o_ref[...] = jnp.zeros_like(o_ref)

    @pl.when(jnp.logical_and(valid, i == n - 1))
    def _():
        wait_tile(1 - slot)


def _experts(plan, a2r, w1, w3, w2, idx, n_rows, n_slots):
    n_tiles = n_slots // MOE_TM
    resident = pl.Buffered(1)

    def w_idx(i, pos, te, *_):
        return (idx, te[i], 0, 0)

    return pl.pallas_call(
        functools.partial(_experts_kernel, n_assign=2 * n_rows, n_slots=n_slots),
        grid_spec=pltpu.PrefetchScalarGridSpec(
            num_scalar_prefetch=6,
            grid=(n_tiles,),
            in_specs=[
                pl.BlockSpec(memory_space=pl.ANY),
                pl.BlockSpec((None, None, D, D_FF_EXPERT), w_idx, pipeline_mode=resident),
                pl.BlockSpec((None, None, D, D_FF_EXPERT), w_idx, pipeline_mode=resident),
                pl.BlockSpec((None, None, D_FF_EXPERT, D), w_idx, pipeline_mode=resident),
            ],
            out_specs=pl.BlockSpec((MOE_TM * ROW_TILE, LANES), lambda i, *_: (i, 0)),
            scratch_shapes=[
                pltpu.SMEM((n_slots,), I32),
                pltpu.VMEM((2, MOE_TM * ROW_TILE, LANES), F32),
                pltpu.VMEM((MOE_TM, D_FF_EXPERT), BF16),
                pltpu.SemaphoreType.DMA((2,)),
            ],
        ),
        out_shape=jax.ShapeDtypeStruct((n_slots * ROW_TILE, LANES), F32),
        compiler_params=_cparams(1),
        name="moe_experts",
    )(*plan, a2r, w1, w3, w2)


def _combine_kernel(pos_ref, y_hbm, h_ref, rw_ref, mod_ref, lng_ref, lnb_ref, o_ref, buf, sem):
    i = pl.program_id(0)
    n = pl.num_programs(0)

    def issue(step, slot):
        base = step * COMBINE_ROWS

        def body(r, carry):
            d = pl.multiple_of(r * ROW_TILE, ROW_TILE)
            for k in range(2):
                p = pl.multiple_of(pos_ref[2 * (base + r) + k] * ROW_TILE, ROW_TILE)
                pltpu.make_async_copy(y_hbm.at[pl.ds(p, ROW_TILE)],
                                      buf.at[slot, k, pl.ds(d, ROW_TILE)], sem.at[slot]).start()
            return carry

        lax.fori_loop(0, COMBINE_ROWS, body, 0, unroll=4)

    @pl.when(i == 0)
    def _():
        issue(0, 0)

    @pl.when(i + 1 < n)
    def _():
        issue(i + 1, (i + 1) % 2)

    slot = i % 2
    for k in range(2):
        pltpu.make_async_copy(y_hbm.at[pl.ds(0, COMBINE_ROWS * ROW_TILE)], buf.at[slot, k],
                              sem.at[slot]).wait()
    w = rw_ref[...]
    f = (w[:, 0:1] * _load_row_tiles(buf.at[slot, 0], COMBINE_ROWS)
         + w[:, 1:2] * _load_row_tiles(buf.at[slot, 1], COMBINE_ROWS))
    o_ref[...] = _post_norm(h_ref[...], f, mod_ref[5:6, :], lng_ref[1:2, :], lnb_ref[1:2, :])


def _combine(pos, y, h1, rw, mods, ln_g, ln_b, layer, n_rows):
    n_tiles = n_rows // COMBINE_ROWS
    tiles_per_batch = SEQ // COMBINE_ROWS
    n_lat = NL // COMBINE_ROWS
    return pl.pallas_call(
        _combine_kernel,
        grid_spec=pltpu.PrefetchScalarGridSpec(
            num_scalar_prefetch=1,
            grid=(n_tiles,),
            in_specs=[
                pl.BlockSpec(memory_space=pl.ANY),
                pl.BlockSpec((COMBINE_ROWS, D), lambda i, p: (i, 0)),
                pl.BlockSpec((COMBINE_ROWS, LANES), lambda i, p: (i, 0)),
                pl.BlockSpec((None, None, 6, D),
                             lambda i, p: (layer, _mod_row(i, tiles_per_batch, n_lat), 0, 0)),
                pl.BlockSpec((None, 2, D), lambda i, p: (layer, 0, 0)),
                pl.BlockSpec((None, 2, D), lambda i, p: (layer, 0, 0)),
            ],
            out_specs=pl.BlockSpec((COMBINE_ROWS, D), lambda i, p: (i, 0)),
            scratch_shapes=[pltpu.VMEM((2, 2, COMBINE_ROWS * ROW_TILE, LANES), F32),
                            pltpu.SemaphoreType.DMA((2,))],
        ),
        out_shape=jax.ShapeDtypeStruct((n_rows, D), F32),
        compiler_params=_cparams(1),
        name="moe_combine",
    )(pos, y, h1, rw, mods, ln_g, ln_b)


def _route_plan(eid, n_rows, n_slots):
    n_tiles = n_slots // MOE_TM
    e_flat = eid.reshape(-1)
    onehot = (e_flat[:, None] == jnp.arange(N_EXPERTS, dtype=I32)[None, :]).astype(I32)
    csum = jnp.cumsum(onehot, axis=0)
    rank = jnp.sum((csum - onehot) * onehot, axis=1)
    count = csum[-1]
    padded = ((count + MOE_TM - 1) // MOE_TM) * MOE_TM
    ends = jnp.cumsum(padded)
    offs = ends - padded
    pos = jnp.sum(onehot * offs[None, :], axis=1) + rank
    tile_start = jnp.arange(n_tiles, dtype=I32) * MOE_TM
    tile_e = jnp.minimum(jnp.sum((tile_start[:, None] >= ends[None, :]).astype(I32), axis=1),
                         N_EXPERTS - 1)
    tile_v = (tile_start < ends[-1]).astype(I32)
    return (pos.astype(I32), tile_e.astype(I32), tile_v, count.astype(I32), offs.astype(I32),
            ends.astype(I32))


def _moe(a2r, h1, ri, rw, mods, w1, w3, w2, ln_g, ln_b, layer, n_rows):
    n_slots = 2 * n_rows + N_EXPERTS * MOE_TM
    plan = _route_plan(ri[:, :2], n_rows, n_slots)
    ys = _experts(plan, a2r, w1, w3, w2, layer // 2, n_rows, n_slots)
    return _combine(plan[0], ys, h1, rw, mods, ln_g, ln_b, layer, n_rows)


def kernel(x, c, ctx, c_ctx, w_ada, b_ada, w_in, w_pool, pool_scale, sink, w_out, ln_g, ln_b,
           dense_w1, dense_w3, dense_w2, router, moe_w1, moe_w3, moe_w2):
    cond = jnp.concatenate([c, c_ctx[None, :], jnp.zeros((MOD_ROWS - B - 1, D), F32)], axis=0)
    mods = _ada_tables(cond, w_ada, b_ada)
    cos_t, sin_t = _rope_tables()

    w_in_b = w_in.astype(BF16)
    w_pool_b = w_pool.astype(BF16)
    w_out_b = w_out.astype(BF16)
    dw1, dw3, dw2 = dense_w1.astype(BF16), dense_w3.astype(BF16), dense_w2.astype(BF16)
    mw1, mw3, mw2 = moe_w1.astype(BF16), moe_w3.astype(BF16), moe_w2.astype(BF16)
    r_hi = router.astype(BF16)
    r_lo = (router - r_hi.astype(F32)).astype(BF16)
    lane_pad = ((0, 0), (0, 0), (0, LANES - N_EXPERTS))
    router_pad = jnp.concatenate([jnp.pad(r_hi, lane_pad), jnp.pad(r_lo, lane_pad)], axis=-1)
    pool_scale3 = pool_scale.reshape(DEPTH, 1, POOL_W)

    h = jnp.concatenate([x.reshape(NL, D), ctx.reshape(NC, D)], axis=0)
    for l in range(DEPTH):
        last = l == DEPTH - 1
        n_rows = NL if last else NT
        q, kk, vv, u = _inproj(h, mods, w_in_b, cos_t, sin_t, l)
        mix = _mixer_latent(q, kk, vv, u, sink[l], w_pool_b, pool_scale3, l)
        mix_c = mix if last else _mixer_context(q, kk, vv, u, sink[l], w_pool_b, pool_scale3, l)
        if l % 2 == 0:
            h1, a2 = _outproj(mix, mix_c, h, mods, w_out_b, ln_g, ln_b, None, l, n_rows)
            h = _ffn_dense(a2, h1, mods, dw1, dw3, dw2, ln_g, ln_b, l, n_rows)
        else:
            h1, a2, ri, rw = _outproj(mix, mix_c, h, mods, w_out_b, ln_g, ln_b, router_pad, l,
                                      n_rows)
            h = _moe(a2, h1, ri, rw, mods, mw1, mw3, mw2, ln_g, ln_b, l, n_rows)
    return h.reshape(B, SEQ, D)
```

```python
import functools

import jax
import jax.numpy as jnp
from jax import lax
from jax.experimental import pallas as pl
from jax.experimental.pallas import tpu as pltpu

F32 = jnp.float32
BF16 = jnp.bfloat16
I32 = jnp.int32

D = 1024
B = 8
SEQ = 2048
DEPTH = 4
CTX = 256
GRID_W = 64
HEAD_DIM = 64
N_Q_HEADS = 8
N_KV_HEADS = 2
ATTN_W = N_Q_HEADS * HEAD_DIM
KV_W = N_KV_HEADS * HEAD_DIM
POOL_WINDOWS = (2, 4, 8, 16)
POOL_W = D - ATTN_W
POOL_G = POOL_W // len(POOL_WINDOWS)
IN_W = ATTN_W + 2 * KV_W + POOL_W
WINDOW = 128
ROPE_BASE = 10000.0
ROPE_FREQS = HEAD_DIM // 4
D_FF_DENSE = 2816
N_EXPERTS = 8
D_FF_EXPERT = 3584
ALPHA = (2 * DEPTH) ** 0.25
LN_EPS = 1e-6
NEG_INF = -1e30
LOG2E = 1.4426950408889634

NL = B * SEQ
NC = B * CTX
NT = NL + NC
MOD_ROWS = 16
LANES = 128

TM = 512
TQ = 128
TQ_STEP = 256
ROW_SPLIT = 256
MOE_TM = 512
FF_CHUNK = 1024
COMBINE_ROWS = 256
COMBINE_CHUNKS = 4

VMEM_LIMIT = 56 * 1024 * 1024


def _cparams(n_axes, vmem=VMEM_LIMIT):
    return pltpu.CompilerParams(
        dimension_semantics=("arbitrary",) * n_axes, vmem_limit_bytes=vmem)


def _layernorm(x):
    mu = jnp.mean(x, axis=-1, keepdims=True)
    xc = x - mu
    var = jnp.mean(xc * xc, axis=-1, keepdims=True)
    return xc * lax.rsqrt(var + LN_EPS)


def _mod_row(tile, tiles_per_batch, n_latent_tiles):
    return jnp.where(tile < n_latent_tiles, tile // tiles_per_batch, B)


ADA_TN = 1536


def _ada_kernel(s_ref, w_ref, b_ref, o_ref):
    s = jax.nn.silu(s_ref[...]).astype(BF16)
    w = w_ref[...].astype(BF16)
    o_ref[...] = jnp.dot(s, w, preferred_element_type=F32) + b_ref[...]


def _ada_tables(cond, w_ada, b_ada):
    n_col = (6 * D) // ADA_TN
    out = pl.pallas_call(
        _ada_kernel,
        grid=(DEPTH, n_col),
        in_specs=[
            pl.BlockSpec((MOD_ROWS, D), lambda l, j: (0, 0)),
            pl.BlockSpec((None, D, ADA_TN), lambda l, j: (l, 0, j)),
            pl.BlockSpec((None, 1, ADA_TN), lambda l, j: (l, 0, j)),
        ],
        out_specs=pl.BlockSpec((None, MOD_ROWS, ADA_TN), lambda l, j: (l, 0, j)),
        out_shape=jax.ShapeDtypeStruct((DEPTH, MOD_ROWS, 6 * D), F32),
        compiler_params=_cparams(2),
        name="ada_tables",
    )(cond, w_ada, b_ada.reshape(DEPTH, 1, 6 * D))
    return out.reshape(DEPTH, MOD_ROWS, 6, D)


def _inproj_kernel(h_ref, mod_ref, w_ref, cos_ref, sin_ref,
                   q_ref, kk_ref, vv_ref, u_ref):
    lane = lax.broadcasted_iota(I32, (1, LANES), 1)
    first_half = (lane % 32) < 16
    low = lane < HEAD_DIM
    for r0 in range(0, TM, ROW_SPLIT):
        rs = slice(r0, r0 + ROW_SPLIT)
        a = _layernorm(h_ref[rs, :]) * (1.0 + mod_ref[1:2, :]) + mod_ref[0:1, :]
        p = jnp.dot(a.astype(BF16), w_ref[...], preferred_element_type=F32)
        cos = cos_ref[rs, :]
        sin = sin_ref[rs, :]

        def rope(x):
            swapped = jnp.where(first_half, pltpu.roll(x, LANES - 16, 1), pltpu.roll(x, 16, 1))
            return x * cos + swapped * sin

        for c in range(ATTN_W // LANES):
            qc = rope(p[:, c * LANES:(c + 1) * LANES]) * (HEAD_DIM ** -0.5 * LOG2E)
            q_ref[rs, c * LANES:(c + 1) * LANES] = qc.astype(BF16)
        k = rope(p[:, ATTN_W:ATTN_W + KV_W])
        k_sw = pltpu.roll(k, HEAD_DIM, 1)
        kk_ref[rs, 0:LANES] = jnp.where(low, k, k_sw).astype(BF16)
        kk_ref[rs, LANES:2 * LANES] = jnp.where(low, k_sw, k).astype(BF16)
        v = p[:, ATTN_W + KV_W:ATTN_W + 2 * KV_W]
        v_sw = pltpu.roll(v, HEAD_DIM, 1)
        vv_ref[rs, 0:LANES] = jnp.where(low, v, v_sw).astype(BF16)
        vv_ref[rs, LANES:2 * LANES] = jnp.where(low, v_sw, v).astype(BF16)
        u_ref[rs, :] = p[:, ATTN_W + 2 * KV_W:]


def _inproj(h, mods, w_in, cos_t, sin_t, layer):
    n_tiles = NT // TM
    tiles_per_batch = SEQ // TM
    n_lat = NL // TM

    def rope_idx(i):
        return (jnp.where(i < n_lat, i % tiles_per_batch, tiles_per_batch), 0)

    return pl.pallas_call(
        _inproj_kernel,
        grid=(n_tiles,),
        in_specs=[
            pl.BlockSpec((TM, D), lambda i: (i, 0)),
            pl.BlockSpec((None, None, 6, D),
                         lambda i: (layer, _mod_row(i, tiles_per_batch, n_lat), 0, 0)),
            pl.BlockSpec((None, D, IN_W), lambda i: (layer, 0, 0)),
            pl.BlockSpec((TM, LANES), rope_idx),
            pl.BlockSpec((TM, LANES), rope_idx),
        ],
        out_specs=[
            pl.BlockSpec((TM, ATTN_W), lambda i: (i, 0)),
            pl.BlockSpec((TM, 2 * KV_W), lambda i: (i, 0)),
            pl.BlockSpec((TM, 2 * KV_W), lambda i: (i, 0)),
            pl.BlockSpec((TM, POOL_W), lambda i: (i, 0)),
        ],
        out_shape=[
            jax.ShapeDtypeStruct((NT, ATTN_W), BF16),
            jax.ShapeDtypeStruct((NT, 2 * KV_W), BF16),
            jax.ShapeDtypeStruct((NT, 2 * KV_W), BF16),
            jax.ShapeDtypeStruct((NT, POOL_W), F32),
        ],
        compiler_params=_cparams(1),
        name="inproj",
    )(h, mods, w_in, cos_t, sin_t)


def _rope_tables():
    rows = SEQ // GRID_W
    row = jnp.repeat(jnp.arange(rows, dtype=F32), GRID_W)
    col = jnp.tile(jnp.arange(GRID_W, dtype=F32), rows)
    inv = ROPE_BASE ** (-jnp.arange(ROPE_FREQS, dtype=F32) / ROPE_FREQS)
    ang_r = row[:, None] * inv[None, :]
    ang_c = col[:, None] * inv[None, :]
    cr, sr, cc, sc = jnp.cos(ang_r), jnp.sin(ang_r), jnp.cos(ang_c), jnp.sin(ang_c)
    cos_h = jnp.concatenate([cr, cr, cc, cc], axis=-1)
    sin_h = jnp.concatenate([-sr, sr, -sc, sc], axis=-1)
    cos_t = jnp.concatenate([jnp.tile(cos_h, (1, 2)), jnp.ones((TM, LANES), F32)], axis=0)
    sin_t = jnp.concatenate([jnp.tile(sin_h, (1, 2)), jnp.zeros((TM, LANES), F32)], axis=0)
    return cos_t, sin_t


def _pool_group(top, mid, bot, t0, seq_len, width):
    rows = mid.shape[0]
    slab = jnp.concatenate([top, mid, bot], axis=0)
    half = width // 2
    s = slab
    span = 1
    while span < width:
        n = s.shape[0] - span
        s = s[0:n] + s[span:span + n]
        span *= 2
    start = 8 - half
    total = s[start:start + rows]
    pos = t0 + lax.broadcasted_iota(I32, (rows, 1), 0)
    hi = jnp.minimum(pos + half, seq_len)
    lo = jnp.maximum(pos - half, 0)
    cnt = (hi - lo).astype(F32)
    return total / cnt - mid


def _mixer_kernel(sink_ref, q_ref, kk_ref, vv_ref, kkc_ref, vvc_ref, u_ref, wp_ref, ps_ref,
                  o_ref, *, tq, sub, seq_len, band):
    for sb in range(tq // sub):
        _mixer_block(sink_ref, q_ref, kk_ref, vv_ref, kkc_ref, vvc_ref, u_ref, wp_ref, ps_ref,
                     o_ref, t0=pl.program_id(1) * tq + sb * sub, r0=sb * sub, tq=sub,
                     seq_len=seq_len, band=band)


def _mixer_block(sink_ref, q_ref, kk_ref, vv_ref, kkc_ref, vvc_ref, u_ref, wp_ref, ps_ref,
                 o_ref, *, t0, r0, tq, seq_len, band):
    lane = lax.broadcasted_iota(I32, (1, LANES), 1)
    low = lane < HEAD_DIM

    if band:
        n_band = tq + 2 * WINDOW
        ks = jnp.clip(t0 - WINDOW, 0, seq_len - n_band)
        ks = pl.multiple_of(ks, LANES)
        kk = jnp.concatenate([kk_ref[pl.ds(ks, n_band), :], kkc_ref[...]], axis=0)
        vv = jnp.concatenate([vv_ref[pl.ds(ks, n_band), :], vvc_ref[...]], axis=0)
        qpos = t0 + lax.broadcasted_iota(I32, (tq, n_band), 0)
        kpos = ks + lax.broadcasted_iota(I32, (tq, n_band), 1)
        bias = jnp.where(jnp.abs(kpos - qpos) <= WINDOW, 0.0, NEG_INF)
    else:
        n_band = 0
        kk = kkc_ref[...]
        vv = vvc_ref[...]
        bias = None
    nk = kk.shape[0]
    zero = jnp.zeros_like(kk[:, 0:LANES])
    low_f = jnp.broadcast_to(jnp.where(low, 1.0, 0.0), (nk, LANES))
    ones_st = jnp.concatenate([low_f, 1.0 - low_f], axis=0).astype(BF16)

    for g in range(N_KV_HEADS):
        kg = kk[:, g * LANES:(g + 1) * LANES]
        vg = vv[:, g * LANES:(g + 1) * LANES]
        k_st = jnp.concatenate([jnp.where(low, kg, zero), jnp.where(low, zero, kg)], axis=0)
        v_st = jnp.concatenate([jnp.where(low, vg, zero), jnp.where(low, zero, vg)], axis=0)
        v_st = jnp.concatenate([v_st, ones_st], axis=1)
        for pr in range(2):
            j = 2 * g + pr
            qp = q_ref[r0:r0 + tq, j * LANES:(j + 1) * LANES]
            s = lax.dot_general(qp, k_st, (((1,), (1,)), ((), ())), preferred_element_type=F32)
            ps = []
            sink_terms = []
            for hh in range(2):
                sink = sink_ref[2 * j + hh] * LOG2E
                sh = s[:, hh * nk:(hh + 1) * nk]
                if bias is not None:
                    parts = [sh[:, :n_band] + bias, sh[:, n_band:]]
                else:
                    parts = [sh]
                m = sink
                for part in parts:
                    m = jnp.maximum(jnp.max(part, axis=-1, keepdims=True), m)
                sink_terms.append(jnp.exp2(sink - m))
                ps += [jnp.exp2(part - m).astype(BF16) for part in parts]
            od = jnp.dot(jnp.concatenate(ps, axis=1), v_st, preferred_element_type=F32)
            den = od[:, LANES:] + jnp.where(low, sink_terms[0], sink_terms[1])
            o_ref[r0:r0 + tq, j * LANES:(j + 1) * LANES] = (od[:, :LANES] / den).astype(BF16)

    zeros8 = jnp.zeros((8, POOL_W), F32)
    if band:
        t0a = pl.multiple_of(t0, 8)
        mid = u_ref[pl.ds(t0a, tq), :]
        top_s = pl.multiple_of(jnp.maximum(t0 - 8, 0), 8)
        bot_s = pl.multiple_of(jnp.minimum(t0 + tq, seq_len - 8), 8)
        top = jnp.where(t0 > 0, u_ref[pl.ds(top_s, 8), :], zeros8)
        bot = jnp.where(t0 + tq < seq_len, u_ref[pl.ds(bot_s, 8), :], zeros8)
    else:
        mid = u_ref[...]
        top = zeros8
        bot = zeros8
    for g, width in enumerate(POOL_WINDOWS):
        sl = slice(g * POOL_G, (g + 1) * POOL_G)
        diff = _pool_group(top[:, sl], mid[:, sl], bot[:, sl], t0, seq_len, width)
        mixed = jnp.dot(diff.astype(BF16), wp_ref[g], preferred_element_type=F32)
        mixed = mixed * ps_ref[:, sl]
        o_ref[r0:r0 + tq, ATTN_W + g * POOL_G:ATTN_W + (g + 1) * POOL_G] = mixed.astype(BF16)


def _mixer_latent(q, kk, vv, u, sink_l, w_pool, pool_scale, layer):
    nq = SEQ // TQ_STEP
    ctx_blk = NL // CTX
    kern = functools.partial(_mixer_kernel, tq=TQ_STEP, sub=TQ, seq_len=SEQ, band=True)
    return pl.pallas_call(
        kern,
        grid=(B, nq),
        in_specs=[
            pl.BlockSpec(memory_space=pltpu.SMEM),
            pl.BlockSpec((TQ_STEP, ATTN_W), lambda b, t: (b * nq + t, 0)),
            pl.BlockSpec((SEQ, 2 * KV_W), lambda b, t: (b, 0)),
            pl.BlockSpec((SEQ, 2 * KV_W), lambda b, t: (b, 0)),
            pl.BlockSpec((CTX, 2 * KV_W), lambda b, t: (ctx_blk + b, 0)),
            pl.BlockSpec((CTX, 2 * KV_W), lambda b, t: (ctx_blk + b, 0)),
            pl.BlockSpec((SEQ, POOL_W), lambda b, t: (b, 0)),
            pl.BlockSpec((None, len(POOL_WINDOWS), POOL_G, POOL_G), lambda b, t: (layer, 0, 0, 0)),
            pl.BlockSpec((None, 1, POOL_W), lambda b, t: (layer, 0, 0)),
        ],
        out_specs=pl.BlockSpec((TQ_STEP, D), lambda b, t: (b * nq + t, 0)),
        out_shape=jax.ShapeDtypeStruct((NL, D), BF16),
        compiler_params=_cparams(2),
        name="mixer_latent",
    )(sink_l, q, kk, vv, kk, vv, u, w_pool, pool_scale)


def _mixer_context(q, kk, vv, u, sink_l, w_pool, pool_scale, layer):
    ctx_blk = NL // CTX
    kern = functools.partial(_mixer_kernel, tq=CTX, sub=CTX, seq_len=CTX, band=False)

    def body(sink_ref, q_ref, kkc_ref, vvc_ref, u_ref, wp_ref, ps_ref, o_ref):
        kern(sink_ref, q_ref, None, None, kkc_ref, vvc_ref, u_ref, wp_ref, ps_ref, o_ref)

    return pl.pallas_call(
        body,
        grid=(B, 1),
        in_specs=[
            pl.BlockSpec(memory_space=pltpu.SMEM),
            pl.BlockSpec((CTX, ATTN_W), lambda b, t: (ctx_blk + b, 0)),
            pl.BlockSpec((CTX, 2 * KV_W), lambda b, t: (ctx_blk + b, 0)),
            pl.BlockSpec((CTX, 2 * KV_W), lambda b, t: (ctx_blk + b, 0)),
            pl.BlockSpec((CTX, POOL_W), lambda b, t: (ctx_blk + b, 0)),
            pl.BlockSpec((None, len(POOL_WINDOWS), POOL_G, POOL_G), lambda b, t: (layer, 0, 0, 0)),
            pl.BlockSpec((None, 1, POOL_W), lambda b, t: (layer, 0, 0)),
        ],
        out_specs=pl.BlockSpec((CTX, D), lambda b, t: (b, 0)),
        out_shape=jax.ShapeDtypeStruct((NC, D), BF16),
        compiler_params=_cparams(2),
        name="mixer_context",
    )(sink_l, q, kk, vv, u, w_pool, pool_scale)


def _post_norm(h, upd, gate, g, b):
    return _layernorm(ALPHA * h + gate * upd) * g + b


ROW_TILE = D // LANES


def _store_row_tiles(ref, x):
    rows = x.shape[0]
    for c in range(ROW_TILE):
        ref[pl.ds(c, rows, stride=ROW_TILE), :] = x[:, c * LANES:(c + 1) * LANES]


def _load_row_tiles(ref, rows):
    return jnp.concatenate(
        [ref[pl.ds(c, rows, stride=ROW_TILE), :] for c in range(ROW_TILE)], axis=1)


def _route_top2(a, router_ref, ri_ref, rw_ref):
    a_hi = a.astype(BF16)
    a_lo = (a - a_hi.astype(F32)).astype(BF16)
    t_hi = jnp.dot(a_hi, router_ref[...], preferred_element_type=F32)
    t_lo = jnp.dot(a_lo, router_ref[:, :LANES], preferred_element_type=F32)
    logits = t_hi[:, :LANES] + t_hi[:, LANES:] + t_lo
    lane = lax.broadcasted_iota(I32, logits.shape, 1)
    lane_f = lane.astype(F32)
    logits = jnp.where(lane < N_EXPERTS, logits, -jnp.inf)
    m1 = jnp.max(logits, axis=-1, keepdims=True)
    i1 = jnp.min(jnp.where(logits == m1, lane_f, float(LANES)), axis=-1, keepdims=True)
    rest = jnp.where(lane_f == i1, -jnp.inf, logits)
    m2 = jnp.max(rest, axis=-1, keepdims=True)
    i2 = jnp.min(jnp.where(rest == m2, lane_f, float(LANES)), axis=-1, keepdims=True)
    e = jnp.exp(m2 - m1)
    w1 = 1.0 / (1.0 + e)
    w2 = e / (1.0 + e)
    ri_ref[...] = jnp.where(lane == 0, i1, jnp.where(lane == 1, i2, 0.0)).astype(I32)
    rw_ref[...] = jnp.where(lane == 0, w1, jnp.where(lane == 1, w2, 0.0))


def _outproj_kernel(mixl_ref, mixc_ref, h_ref, mod_ref, w_ref, lng_ref, lnb_ref, *rest,
                    moe, n_lat, has_ctx):
    if moe:
        router_ref, h1_ref, a2_ref, ri_ref, rw_ref = rest
    else:
        h1_ref, a2_ref = rest
    split = TM if moe else ROW_SPLIT
    for r0 in range(0, TM, split):
        rs = slice(r0, r0 + split)
        mix = mixl_ref[rs, :]
        if has_ctx:
            mix = jnp.where(pl.program_id(0) >= n_lat, mixc_ref[rs, :], mix)
        y = jnp.dot(mix, w_ref[...], preferred_element_type=F32)
        h1 = _post_norm(h_ref[rs, :], y, mod_ref[2:3, :], lng_ref[0:1, :], lnb_ref[0:1, :])
        h1_ref[rs, :] = h1
        a2 = _layernorm(h1) * (1.0 + mod_ref[4:5, :]) + mod_ref[3:4, :]
        if moe:
            _store_row_tiles(a2_ref.at[pl.ds(r0 * ROW_TILE, split * ROW_TILE)], a2)
            _route_top2(a2, router_ref, ri_ref.at[pl.ds(r0, split)], rw_ref.at[pl.ds(r0, split)])
        else:
            a2_ref[rs, :] = a2.astype(BF16)


def _outproj(mix_lat, mix_ctx, h, mods, w_out, ln_g, ln_b, router_pad, layer, n_rows):
    n_tiles = n_rows // TM
    tiles_per_batch = SEQ // TM
    n_lat = NL // TM
    moe = router_pad is not None
    has_ctx = n_rows > NL
    in_specs = [
        pl.BlockSpec((TM, D), lambda i: (jnp.minimum(i, n_lat - 1), 0)),
        pl.BlockSpec((TM, D), lambda i: (jnp.maximum(i - n_lat, 0), 0)),
        pl.BlockSpec((TM, D), lambda i: (i, 0)),
        pl.BlockSpec((None, None, 6, D),
                     lambda i: (layer, _mod_row(i, tiles_per_batch, n_lat), 0, 0)),
        pl.BlockSpec((None, D, D), lambda i: (layer, 0, 0)),
        pl.BlockSpec((None, 2, D), lambda i: (layer, 0, 0)),
        pl.BlockSpec((None, 2, D), lambda i: (layer, 0, 0)),
    ]
    out_specs = [pl.BlockSpec((TM, D), lambda i: (i, 0)), pl.BlockSpec((TM, D), lambda i: (i, 0))]
    out_shape = [jax.ShapeDtypeStruct((n_rows, D), F32), jax.ShapeDtypeStruct((n_rows, D), BF16)]
    args = [mix_lat, mix_ctx, h, mods, w_out, ln_g, ln_b]
    if moe:
        in_specs.append(pl.BlockSpec((None, D, 2 * LANES), lambda i: (layer // 2, 0, 0)))
        out_specs[1] = pl.BlockSpec((TM * ROW_TILE, LANES), lambda i: (i, 0))
        out_shape[1] = jax.ShapeDtypeStruct((n_rows * ROW_TILE, LANES), F32)
        out_specs += [pl.BlockSpec((TM, LANES), lambda i: (i, 0)),
                      pl.BlockSpec((TM, LANES), lambda i: (i, 0))]
        out_shape += [jax.ShapeDtypeStruct((n_rows, LANES), I32),
                      jax.ShapeDtypeStruct((n_rows, LANES), F32)]
        args.append(router_pad)
    return pl.pallas_call(
        functools.partial(_outproj_kernel, moe=moe, n_lat=n_lat, has_ctx=has_ctx),
        grid=(n_tiles,),
        in_specs=in_specs,
        out_specs=out_specs,
        out_shape=out_shape,
        compiler_params=_cparams(1),
        name="outproj_moe" if moe else "outproj",
    )(*args)


def _swiglu_hidden(x, w1_ref, w3_ref, t_ref, d_ff):
    for c0 in range(0, d_ff, FF_CHUNK):
        c1 = min(c0 + FF_CHUNK, d_ff)
        g = jnp.dot(x, w1_ref[:, c0:c1], preferred_element_type=F32)
        u = jnp.dot(x, w3_ref[:, c0:c1], preferred_element_type=F32)
        t_ref[:, c0:c1] = (jax.nn.silu(g) * u).astype(BF16)


def _ffn_dense_kernel(a_ref, h_ref, mod_ref, w1_ref, w3_ref, w2_ref, lng_ref, lnb_ref, o_ref,
                      t_ref):
    _swiglu_hidden(a_ref[...], w1_ref, w3_ref, t_ref, D_FF_DENSE)
    f = jnp.dot(t_ref[...], w2_ref[...], preferred_element_type=F32)
    o_ref[...] = _post_norm(h_ref[...], f, mod_ref[5:6, :], lng_ref[1:2, :], lnb_ref[1:2, :])


def _ffn_dense(a2, h1, mods, w1, w3, w2, ln_g, ln_b, layer, n_rows):
    n_tiles = n_rows // TM
    tiles_per_batch = SEQ // TM
    n_lat = NL // TM
    idx = layer // 2
    resident = pl.Buffered(1)
    return pl.pallas_call(
        _ffn_dense_kernel,
        grid=(n_tiles,),
        in_specs=[
            pl.BlockSpec((TM, D), lambda i: (i, 0)),
            pl.BlockSpec((TM, D), lambda i: (i, 0)),
            pl.BlockSpec((None, None, 6, D),
                         lambda i: (layer, _mod_row(i, tiles_per_batch, n_lat), 0, 0)),
            pl.BlockSpec((None, D, D_FF_DENSE), lambda i: (idx, 0, 0), pipeline_mode=resident),
            pl.BlockSpec((None, D, D_FF_DENSE), lambda i: (idx, 0, 0), pipeline_mode=resident),
            pl.BlockSpec((None, D_FF_DENSE, D), lambda i: (idx, 0, 0), pipeline_mode=resident),
            pl.BlockSpec((None, 2, D), lambda i: (layer, 0, 0)),
            pl.BlockSpec((None, 2, D), lambda i: (layer, 0, 0)),
        ],
        out_specs=pl.BlockSpec((TM, D), lambda i: (i, 0)),
        out_shape=jax.ShapeDtypeStruct((n_rows, D), F32),
        scratch_shapes=[pltpu.VMEM((TM, D_FF_DENSE), BF16)],
        compiler_params=_cparams(1),
        name="ffn_dense",
    )(a2, h1, mods, w1, w3, w2, ln_g, ln_b)


def _experts_kernel(pos_ref, te_ref, tv_ref, cnt_ref, off_ref, end_ref,
                    a_hbm, w1_ref, w3_ref, w2_ref, o_ref, src_ref, xg_ref, t_ref, sem,
                    *, n_assign, n_slots):
    i = pl.program_id(0)
    n = pl.num_programs(0)

    def start_row(tile, slot, r):
        s = pl.multiple_of(src_ref[tile * MOE_TM + r] * ROW_TILE, ROW_TILE)
        d = r * ROW_TILE if isinstance(r, int) else pl.multiple_of(r * ROW_TILE, ROW_TILE)
        pltpu.make_async_copy(a_hbm.at[pl.ds(s, ROW_TILE)],
                              xg_ref.at[slot, pl.ds(d, ROW_TILE)], sem.at[slot]).start()

    def wait_tile(slot):
        pltpu.make_async_copy(a_hbm.at[pl.ds(0, MOE_TM * ROW_TILE)], xg_ref.at[slot],
                              sem.at[slot]).wait()

    def clear(s, carry):
        src_ref[s] = 0
        return carry

    @pl.when(i == 0)
    def _():
        for e in range(N_EXPERTS):
            lax.fori_loop(off_ref[e] + cnt_ref[e], end_ref[e], clear, 0)
        lax.fori_loop(end_ref[N_EXPERTS - 1], n_slots, clear, 0)

        def fill(a, carry):
            src_ref[pos_ref[a]] = lax.shift_right_logical(a, 1)
            return carry

        lax.fori_loop(0, n_assign, fill, 0, unroll=8)

        def first(r, carry):
            start_row(0, 0, r)
            return carry

        lax.fori_loop(0, MOE_TM, first, 0, unroll=8)

    valid = tv_ref[i] > 0
    slot = i % 2

    @pl.when(jnp.logical_or(i == 0, tv_ref[jnp.maximum(i - 1, 0)] > 0))
    def _():
        wait_tile(slot)

    @pl.when(valid)
    def _():
        x = _load_row_tiles(xg_ref.at[slot], MOE_TM).astype(BF16)
        nxt_tile = lax.rem(i + 1, n)
        chunks = [(c0, min(c0 + FF_CHUNK, D_FF_EXPERT)) for c0 in range(0, D_FF_EXPERT, FF_CHUNK)]
        per_chunk = MOE_TM // len(chunks)
        for ci, (c0, c1) in enumerate(chunks):
            for r in range(ci * per_chunk, (ci + 1) * per_chunk):
                start_row(nxt_tile, 1 - slot, r)
            g = jnp.dot(x, w1_ref[:, c0:c1], preferred_element_type=F32)
            u = jnp.dot(x, w3_ref[:, c0:c1], preferred_element_type=F32)
            t_ref[:, c0:c1] = (jax.nn.silu(g) * u).astype(BF16)
        y = jnp.dot(t_ref[...], w2_ref[...], preferred_element_type=F32)
        _store_row_tiles(o_ref, y)

    @pl.when(jnp.logical_not(valid))
    def _():
        o_ref[...] = jnp.zeros_like(o_ref)

    @pl.when(jnp.logical_and(valid, i == n - 1))
    def _():
        wait_tile(1 - slot)


def _experts(plan, a2r, w1, w3, w2, idx, n_rows, n_slots):
    n_tiles = n_slots // MOE_TM
    resident = pl.Buffered(1)

    def w_idx(i, pos, te, *_):
        return (idx, te[i], 0, 0)

    return pl.pallas_call(
        functools.partial(_experts_kernel, n_assign=2 * n_rows, n_slots=n_slots),
        grid_spec=pltpu.PrefetchScalarGridSpec(
            num_scalar_prefetch=6,
            grid=(n_tiles,),
            in_specs=[
                pl.BlockSpec(memory_space=pl.ANY),
                pl.BlockSpec((None, None, D, D_FF_EXPERT), w_idx, pipeline_mode=resident),
                pl.BlockSpec((None, None, D, D_FF_EXPERT), w_idx, pipeline_mode=resident),
                pl.BlockSpec((None, None, D_FF_EXPERT, D), w_idx, pipeline_mode=resident),
            ],
            out_specs=pl.BlockSpec((MOE_TM * ROW_TILE, LANES), lambda i, *_: (i, 0)),
            scratch_shapes=[
                pltpu.SMEM((n_slots,), I32),
                pltpu.VMEM((2, MOE_TM * ROW_TILE, LANES), F32),
                pltpu.VMEM((MOE_TM, D_FF_EXPERT), BF16),
                pltpu.SemaphoreType.DMA((2,)),
            ],
        ),
        out_shape=jax.ShapeDtypeStruct((n_slots * ROW_TILE, LANES), F32),
        compiler_params=_cparams(1),
        name="moe_experts",
    )(*plan, a2r, w1, w3, w2)


def _combine_kernel(pos_ref, y_hbm, h_ref, rw_ref, mod_ref, lng_ref, lnb_ref, o_ref, buf, sem):
    i = pl.program_id(0)
    n = pl.num_programs(0)

    def start_row(step, slot, r):
        d = r * ROW_TILE if isinstance(r, int) else pl.multiple_of(r * ROW_TILE, ROW_TILE)
        for k in range(2):
            p = pl.multiple_of(pos_ref[2 * (step * COMBINE_ROWS + r) + k] * ROW_TILE, ROW_TILE)
            pltpu.make_async_copy(y_hbm.at[pl.ds(p, ROW_TILE)],
                                  buf.at[slot, k, pl.ds(d, ROW_TILE)], sem.at[slot]).start()

    def wait_tile(slot):
        for k in range(2):
            pltpu.make_async_copy(y_hbm.at[pl.ds(0, COMBINE_ROWS * ROW_TILE)], buf.at[slot, k],
                                  sem.at[slot]).wait()

    @pl.when(i == 0)
    def _():
        def first(r, carry):
            start_row(0, 0, r)
            return carry

        lax.fori_loop(0, COMBINE_ROWS, first, 0, unroll=4)

    slot = i % 2
    wait_tile(slot)
    nxt = lax.rem(i + 1, n)
    chunk = COMBINE_ROWS // COMBINE_CHUNKS
    for c in range(COMBINE_CHUNKS):
        for r in range(c * chunk, (c + 1) * chunk):
            start_row(nxt, 1 - slot, r)
        rs = slice(c * chunk, (c + 1) * chunk)
        tiles = pl.ds(c * chunk * ROW_TILE, chunk * ROW_TILE)
        w = rw_ref[rs, :]
        f = (w[:, 0:1] * _load_row_tiles(buf.at[slot, 0, tiles], chunk)
             + w[:, 1:2] * _load_row_tiles(buf.at[slot, 1, tiles], chunk))
        o_ref[rs, :] = _post_norm(h_ref[rs, :], f, mod_ref[5:6, :], lng_ref[1:2, :],
                                  lnb_ref[1:2, :])

    @pl.when(i == n - 1)
    def _():
        wait_tile(1 - slot)


def _combine(pos, y, h1, rw, mods, ln_g, ln_b, layer, n_rows):
    n_tiles = n_rows // COMBINE_ROWS
    tiles_per_batch = SEQ // COMBINE_ROWS
    n_lat = NL // COMBINE_ROWS
    return pl.pallas_call(
        _combine_kernel,
        grid_spec=pltpu.PrefetchScalarGridSpec(
            num_scalar_prefetch=1,
            grid=(n_tiles,),
            in_specs=[
                pl.BlockSpec(memory_space=pl.ANY),
                pl.BlockSpec((COMBINE_ROWS, D), lambda i, p: (i, 0)),
                pl.BlockSpec((COMBINE_ROWS, LANES), lambda i, p: (i, 0)),
                pl.BlockSpec((None, None, 6, D),
                             lambda i, p: (layer, _mod_row(i, tiles_per_batch, n_lat), 0, 0)),
                pl.BlockSpec((None, 2, D), lambda i, p: (layer, 0, 0)),
                pl.BlockSpec((None, 2, D), lambda i, p: (layer, 0, 0)),
            ],
            out_specs=pl.BlockSpec((COMBINE_ROWS, D), lambda i, p: (i, 0)),
            scratch_shapes=[pltpu.VMEM((2, 2, COMBINE_ROWS * ROW_TILE, LANES), F32),
                            pltpu.SemaphoreType.DMA((2,))],
        ),
        out_shape=jax.ShapeDtypeStruct((n_rows, D), F32),
        compiler_params=_cparams(1),
        name="moe_combine",
    )(pos, y, h1, rw, mods, ln_g, ln_b)


def _route_plan(eid, n_rows, n_slots):
    n_tiles = n_slots // MOE_TM
    e_flat = eid.reshape(-1)
    onehot = (e_flat[:, None] == jnp.arange(N_EXPERTS, dtype=I32)[None, :]).astype(I32)
    csum = jnp.cumsum(onehot, axis=0)
    rank = jnp.sum((csum - onehot) * onehot, axis=1)
    count = csum[-1]
    padded = ((count + MOE_TM - 1) // MOE_TM) * MOE_TM
    ends = jnp.cumsum(padded)
    offs = ends - padded
    pos = jnp.sum(onehot * offs[None, :], axis=1) + rank
    tile_start = jnp.arange(n_tiles, dtype=I32) * MOE_TM
    tile_e = jnp.minimum(jnp.sum((tile_start[:, None] >= ends[None, :]).astype(I32), axis=1),
                         N_EXPERTS - 1)
    tile_v = (tile_start < ends[-1]).astype(I32)
    return (pos.astype(I32), tile_e.astype(I32), tile_v, count.astype(I32), offs.astype(I32),
            ends.astype(I32))


def _moe(a2r, h1, ri, rw, mods, w1, w3, w2, ln_g, ln_b, layer, n_rows):
    n_slots = 2 * n_rows + N_EXPERTS * MOE_TM
    plan = _route_plan(ri[:, :2], n_rows, n_slots)
    ys = _experts(plan, a2r, w1, w3, w2, layer // 2, n_rows, n_slots)
    return _combine(plan[0], ys, h1, rw, mods, ln_g, ln_b, layer, n_rows)


def kernel(x, c, ctx, c_ctx, w_ada, b_ada, w_in, w_pool, pool_scale, sink, w_out, ln_g, ln_b,
           dense_w1, dense_w3, dense_w2, router, moe_w1, moe_w3, moe_w2):
    cond = jnp.concatenate([c, c_ctx[None, :], jnp.zeros((MOD_ROWS - B - 1, D), F32)], axis=0)
    mods = _ada_tables(cond, w_ada, b_ada)
    cos_t, sin_t = _rope_tables()

    w_in_b = w_in.astype(BF16)
    w_pool_b = w_pool.astype(BF16)
    w_out_b = w_out.astype(BF16)
    dw1, dw3, dw2 = dense_w1.astype(BF16), dense_w3.astype(BF16), dense_w2.astype(BF16)
    mw1, mw3, mw2 = moe_w1.astype(BF16), moe_w3.astype(BF16), moe_w2.astype(BF16)
    r_hi = router.astype(BF16)
    r_lo = (router - r_hi.astype(F32)).astype(BF16)
    lane_pad = ((0, 0), (0, 0), (0, LANES - N_EXPERTS))
    router_pad = jnp.concatenate([jnp.pad(r_hi, lane_pad), jnp.pad(r_lo, lane_pad)], axis=-1)
    pool_scale3 = pool_scale.reshape(DEPTH, 1, POOL_W)

    h = jnp.concatenate([x.reshape(NL, D), ctx.reshape(NC, D)], axis=0)
    for l in range(DEPTH):
        last = l == DEPTH - 1
        n_rows = NL if last else NT
        q, kk, vv, u = _inproj(h, mods, w_in_b, cos_t, sin_t, l)
        mix = _mixer_latent(q, kk, vv, u, sink[l], w_pool_b, pool_scale3, l)
        mix_c = mix if last else _mixer_context(q, kk, vv, u, sink[l], w_pool_b, pool_scale3, l)
        if l % 2 == 0:
            h1, a2 = _outproj(mix, mix_c, h, mods, w_out_b, ln_g, ln_b, None, l, n_rows)
            h = _ffn_dense(a2, h1, mods, dw1, dw3, dw2, ln_g, ln_b, l, n_rows)
        else:
            h1, a2, ri, rw = _outproj(mix, mix_c, h, mods, w_out_b, ln_g, ln_b, router_pad, l,
                                      n_rows)
            h = _moe(a2, h1, ri, rw, mods, mw1, mw3, mw2, ln_g, ln_b, l, n_rows)
    return h.reshape(B, SEQ, D)
```

```python
import functools

import jax
import jax.numpy as jnp
from jax import lax
from jax.experimental import pallas as pl
from jax.experimental.pallas import tpu as pltpu

F32 = jnp.float32
BF16 = jnp.bfloat16
I32 = jnp.int32

D = 1024
B = 8
SEQ = 2048
DEPTH = 4
CTX = 256
GRID_W = 64
HEAD_DIM = 64
N_Q_HEADS = 8
N_KV_HEADS = 2
ATTN_W = N_Q_HEADS * HEAD_DIM
KV_W = N_KV_HEADS * HEAD_DIM
POOL_WINDOWS = (2, 4, 8, 16)
POOL_W = D - ATTN_W
POOL_G = POOL_W // len(POOL_WINDOWS)
IN_W = ATTN_W + 2 * KV_W + POOL_W
WINDOW = 128
ROPE_BASE = 10000.0
ROPE_FREQS = HEAD_DIM // 4
D_FF_DENSE = 2816
N_EXPERTS = 8
D_FF_EXPERT = 3584
ALPHA = (2 * DEPTH) ** 0.25
LN_EPS = 1e-6
NEG_INF = -1e30
LOG2E = 1.4426950408889634

NL = B * SEQ
NC = B * CTX
NT = NL + NC
MOD_ROWS = 16
LANES = 128

TM = 512
TQ = 128
TQ_STEP = 256
ROW_SPLIT = 256
MOE_TM = 512
FF_CHUNK = 1024
W_CHUNK = 256
W_RING = 4
COMBINE_ROWS = 256
COMBINE_CHUNKS = 4
COMBINE_SLOTS = 3

VMEM_LIMIT = 56 * 1024 * 1024


def _cparams(n_axes, vmem=VMEM_LIMIT):
    return pltpu.CompilerParams(
        dimension_semantics=("arbitrary",) * n_axes, vmem_limit_bytes=vmem)


def _layernorm(x):
    mu = jnp.mean(x, axis=-1, keepdims=True)
    xc = x - mu
    var = jnp.mean(xc * xc, axis=-1, keepdims=True)
    return xc * lax.rsqrt(var + LN_EPS)


def _mod_row(tile, tiles_per_batch, n_latent_tiles):
    return jnp.where(tile < n_latent_tiles, tile // tiles_per_batch, B)


ADA_TN = 1536


def _ada_kernel(s_ref, w_ref, b_ref, o_ref):
    s = jax.nn.silu(s_ref[...]).astype(BF16)
    w = w_ref[...].astype(BF16)
    o_ref[...] = jnp.dot(s, w, preferred_element_type=F32) + b_ref[...]


def _ada_tables(cond, w_ada, b_ada):
    n_col = (6 * D) // ADA_TN
    out = pl.pallas_call(
        _ada_kernel,
        grid=(DEPTH, n_col),
        in_specs=[
            pl.BlockSpec((MOD_ROWS, D), lambda l, j: (0, 0)),
            pl.BlockSpec((None, D, ADA_TN), lambda l, j: (l, 0, j)),
            pl.BlockSpec((None, 1, ADA_TN), lambda l, j: (l, 0, j)),
        ],
        out_specs=pl.BlockSpec((None, MOD_ROWS, ADA_TN), lambda l, j: (l, 0, j)),
        out_shape=jax.ShapeDtypeStruct((DEPTH, MOD_ROWS, 6 * D), F32),
        compiler_params=_cparams(2),
        name="ada_tables",
    )(cond, w_ada, b_ada.reshape(DEPTH, 1, 6 * D))
    return out.reshape(DEPTH, MOD_ROWS, 6, D)


def _inproj_kernel(h_ref, mod_ref, w_ref, cos_ref, sin_ref,
                   q_ref, kk_ref, vv_ref, u_ref):
    lane = lax.broadcasted_iota(I32, (1, LANES), 1)
    first_half = (lane % 32) < 16
    low = lane < HEAD_DIM
    for r0 in range(0, TM, ROW_SPLIT):
        rs = slice(r0, r0 + ROW_SPLIT)
        a = _layernorm(h_ref[rs, :]) * (1.0 + mod_ref[1:2, :]) + mod_ref[0:1, :]
        p = jnp.dot(a.astype(BF16), w_ref[...], preferred_element_type=F32)
        cos = cos_ref[rs, :]
        sin = sin_ref[rs, :]

        def rope(x):
            swapped = jnp.where(first_half, pltpu.roll(x, LANES - 16, 1), pltpu.roll(x, 16, 1))
            return x * cos + swapped * sin

        for c in range(ATTN_W // LANES):
            qc = rope(p[:, c * LANES:(c + 1) * LANES]) * (HEAD_DIM ** -0.5 * LOG2E)
            q_ref[rs, c * LANES:(c + 1) * LANES] = qc.astype(BF16)
        k = rope(p[:, ATTN_W:ATTN_W + KV_W])
        k_sw = pltpu.roll(k, HEAD_DIM, 1)
        kk_ref[rs, 0:LANES] = jnp.where(low, k, k_sw).astype(BF16)
        kk_ref[rs, LANES:2 * LANES] = jnp.where(low, k_sw, k).astype(BF16)
        v = p[:, ATTN_W + KV_W:ATTN_W + 2 * KV_W]
        v_sw = pltpu.roll(v, HEAD_DIM, 1)
        vv_ref[rs, 0:LANES] = jnp.where(low, v, v_sw).astype(BF16)
        vv_ref[rs, LANES:2 * LANES] = jnp.where(low, v_sw, v).astype(BF16)
        u_ref[rs, :] = p[:, ATTN_W + 2 * KV_W:]


def _inproj(h, mods, w_in, cos_t, sin_t, layer):
    n_tiles = NT // TM
    tiles_per_batch = SEQ // TM
    n_lat = NL // TM

    def rope_idx(i):
        return (jnp.where(i < n_lat, i % tiles_per_batch, tiles_per_batch), 0)

    return pl.pallas_call(
        _inproj_kernel,
        grid=(n_tiles,),
        in_specs=[
            pl.BlockSpec((TM, D), lambda i: (i, 0)),
            pl.BlockSpec((None, None, 6, D),
                         lambda i: (layer, _mod_row(i, tiles_per_batch, n_lat), 0, 0)),
            pl.BlockSpec((None, D, IN_W), lambda i: (layer, 0, 0)),
            pl.BlockSpec((TM, LANES), rope_idx),
            pl.BlockSpec((TM, LANES), rope_idx),
        ],
        out_specs=[
            pl.BlockSpec((TM, ATTN_W), lambda i: (i, 0)),
            pl.BlockSpec((TM, 2 * KV_W), lambda i: (i, 0)),
            pl.BlockSpec((TM, 2 * KV_W), lambda i: (i, 0)),
            pl.BlockSpec((TM, POOL_W), lambda i: (i, 0)),
        ],
        out_shape=[
            jax.ShapeDtypeStruct((NT, ATTN_W), BF16),
            jax.ShapeDtypeStruct((NT, 2 * KV_W), BF16),
            jax.ShapeDtypeStruct((NT, 2 * KV_W), BF16),
            jax.ShapeDtypeStruct((NT, POOL_W), F32),
        ],
        compiler_params=_cparams(1),
        name="inproj",
    )(h, mods, w_in, cos_t, sin_t)


def _rope_tables():
    rows = SEQ // GRID_W
    row = jnp.repeat(jnp.arange(rows, dtype=F32), GRID_W)
    col = jnp.tile(jnp.arange(GRID_W, dtype=F32), rows)
    inv = ROPE_BASE ** (-jnp.arange(ROPE_FREQS, dtype=F32) / ROPE_FREQS)
    ang_r = row[:, None] * inv[None, :]
    ang_c = col[:, None] * inv[None, :]
    cr, sr, cc, sc = jnp.cos(ang_r), jnp.sin(ang_r), jnp.cos(ang_c), jnp.sin(ang_c)
    cos_h = jnp.concatenate([cr, cr, cc, cc], axis=-1)
    sin_h = jnp.concatenate([-sr, sr, -sc, sc], axis=-1)
    cos_t = jnp.concatenate([jnp.tile(cos_h, (1, 2)), jnp.ones((TM, LANES), F32)], axis=0)
    sin_t = jnp.concatenate([jnp.tile(sin_h, (1, 2)), jnp.zeros((TM, LANES), F32)], axis=0)
    return cos_t, sin_t


def _pool_group(top, mid, bot, t0, seq_len, width):
    rows = mid.shape[0]
    slab = jnp.concatenate([top, mid, bot], axis=0)
    half = width // 2
    s = slab
    span = 1
    while span < width:
        n = s.shape[0] - span
        s = s[0:n] + s[span:span + n]
        span *= 2
    start = 8 - half
    total = s[start:start + rows]
    pos = t0 + lax.broadcasted_iota(I32, (rows, 1), 0)
    hi = jnp.minimum(pos + half, seq_len)
    lo = jnp.maximum(pos - half, 0)
    cnt = (hi - lo).astype(F32)
    return total / cnt - mid


def _mixer_kernel(sink_ref, q_ref, kk_ref, vv_ref, kkc_ref, vvc_ref, u_ref, wp_ref, ps_ref,
                  o_ref, *, tq, sub, seq_len, band):
    for sb in range(tq // sub):
        _mixer_block(sink_ref, q_ref, kk_ref, vv_ref, kkc_ref, vvc_ref, u_ref, wp_ref, ps_ref,
                     o_ref, t0=pl.program_id(1) * tq + sb * sub, r0=sb * sub, tq=sub,
                     seq_len=seq_len, band=band)


def _mixer_block(sink_ref, q_ref, kk_ref, vv_ref, kkc_ref, vvc_ref, u_ref, wp_ref, ps_ref,
                 o_ref, *, t0, r0, tq, seq_len, band):
    lane = lax.broadcasted_iota(I32, (1, LANES), 1)
    low = lane < HEAD_DIM

    if band:
        n_band = tq + 2 * WINDOW
        ks = jnp.clip(t0 - WINDOW, 0, seq_len - n_band)
        ks = pl.multiple_of(ks, LANES)
        kk = jnp.concatenate([kk_ref[pl.ds(ks, n_band), :], kkc_ref[...]], axis=0)
        vv = jnp.concatenate([vv_ref[pl.ds(ks, n_band), :], vvc_ref[...]], axis=0)
        qpos = t0 + lax.broadcasted_iota(I32, (tq, n_band), 0)
        kpos = ks + lax.broadcasted_iota(I32, (tq, n_band), 1)
        bias = jnp.where(jnp.abs(kpos - qpos) <= WINDOW, 0.0, NEG_INF)
    else:
        n_band = 0
        kk = kkc_ref[...]
        vv = vvc_ref[...]
        bias = None
    nk = kk.shape[0]
    zero = jnp.zeros_like(kk[:, 0:LANES])
    low_f = jnp.broadcast_to(jnp.where(low, 1.0, 0.0), (nk, LANES))
    ones_st = jnp.concatenate([low_f, 1.0 - low_f], axis=0).astype(BF16)

    for g in range(N_KV_HEADS):
        kg = kk[:, g * LANES:(g + 1) * LANES]
        vg = vv[:, g * LANES:(g + 1) * LANES]
        k_st = jnp.concatenate([jnp.where(low, kg, zero), jnp.where(low, zero, kg)], axis=0)
        v_st = jnp.concatenate([jnp.where(low, vg, zero), jnp.where(low, zero, vg)], axis=0)
        v_st = jnp.concatenate([v_st, ones_st], axis=1)
        for pr in range(2):
            j = 2 * g + pr
            qp = q_ref[r0:r0 + tq, j * LANES:(j + 1) * LANES]
            s = lax.dot_general(qp, k_st, (((1,), (1,)), ((), ())), preferred_element_type=F32)
            ps = []
            sink_terms = []
            for hh in range(2):
                sink = sink_ref[2 * j + hh] * LOG2E
                sh = s[:, hh * nk:(hh + 1) * nk]
                if bias is not None:
                    parts = [sh[:, :n_band] + bias, sh[:, n_band:]]
                else:
                    parts = [sh]
                m = sink
                for part in parts:
                    m = jnp.maximum(jnp.max(part, axis=-1, keepdims=True), m)
                sink_terms.append(jnp.exp2(sink - m))
                ps += [jnp.exp2(part - m).astype(BF16) for part in parts]
            od = jnp.dot(jnp.concatenate(ps, axis=1), v_st, preferred_element_type=F32)
            den = od[:, LANES:] + jnp.where(low, sink_terms[0], sink_terms[1])
            o_ref[r0:r0 + tq, j * LANES:(j + 1) * LANES] = (od[:, :LANES] / den).astype(BF16)

    zeros8 = jnp.zeros((8, POOL_W), F32)
    if band:
        t0a = pl.multiple_of(t0, 8)
        mid = u_ref[pl.ds(t0a, tq), :]
        top_s = pl.multiple_of(jnp.maximum(t0 - 8, 0), 8)
        bot_s = pl.multiple_of(jnp.minimum(t0 + tq, seq_len - 8), 8)
        top = jnp.where(t0 > 0, u_ref[pl.ds(top_s, 8), :], zeros8)
        bot = jnp.where(t0 + tq < seq_len, u_ref[pl.ds(bot_s, 8), :], zeros8)
    else:
        mid = u_ref[...]
        top = zeros8
        bot = zeros8
    for g, width in enumerate(POOL_WINDOWS):
        sl = slice(g * POOL_G, (g + 1) * POOL_G)
        diff = _pool_group(top[:, sl], mid[:, sl], bot[:, sl], t0, seq_len, width)
        mixed = jnp.dot(diff.astype(BF16), wp_ref[g], preferred_element_type=F32)
        mixed = mixed * ps_ref[:, sl]
        o_ref[r0:r0 + tq, ATTN_W + g * POOL_G:ATTN_W + (g + 1) * POOL_G] = mixed.astype(BF16)


def _mixer_latent(q, kk, vv, u, sink_l, w_pool, pool_scale, layer):
    nq = SEQ // TQ_STEP
    ctx_blk = NL // CTX
    kern = functools.partial(_mixer_kernel, tq=TQ_STEP, sub=TQ, seq_len=SEQ, band=True)
    return pl.pallas_call(
        kern,
        grid=(B, nq),
        in_specs=[
            pl.BlockSpec(memory_space=pltpu.SMEM),
            pl.BlockSpec((TQ_STEP, ATTN_W), lambda b, t: (b * nq + t, 0)),
            pl.BlockSpec((SEQ, 2 * KV_W), lambda b, t: (b, 0)),
            pl.BlockSpec((SEQ, 2 * KV_W), lambda b, t: (b, 0)),
            pl.BlockSpec((CTX, 2 * KV_W), lambda b, t: (ctx_blk + b, 0)),
            pl.BlockSpec((CTX, 2 * KV_W), lambda b, t: (ctx_blk + b, 0)),
            pl.BlockSpec((SEQ, POOL_W), lambda b, t: (b, 0)),
            pl.BlockSpec((None, len(POOL_WINDOWS), POOL_G, POOL_G), lambda b, t: (layer, 0, 0, 0)),
            pl.BlockSpec((None, 1, POOL_W), lambda b, t: (layer, 0, 0)),
        ],
        out_specs=pl.BlockSpec((TQ_STEP, D), lambda b, t: (b * nq + t, 0)),
        out_shape=jax.ShapeDtypeStruct((NL, D), BF16),
        compiler_params=_cparams(2),
        name="mixer_latent",
    )(sink_l, q, kk, vv, kk, vv, u, w_pool, pool_scale)


def _mixer_context(q, kk, vv, u, sink_l, w_pool, pool_scale, layer):
    ctx_blk = NL // CTX
    kern = functools.partial(_mixer_kernel, tq=CTX, sub=CTX, seq_len=CTX, band=False)

    def body(sink_ref, q_ref, kkc_ref, vvc_ref, u_ref, wp_ref, ps_ref, o_ref):
        kern(sink_ref, q_ref, None, None, kkc_ref, vvc_ref, u_ref, wp_ref, ps_ref, o_ref)

    return pl.pallas_call(
        body,
        grid=(B, 1),
        in_specs=[
            pl.BlockSpec(memory_space=pltpu.SMEM),
            pl.BlockSpec((CTX, ATTN_W), lambda b, t: (ctx_blk + b, 0)),
            pl.BlockSpec((CTX, 2 * KV_W), lambda b, t: (ctx_blk + b, 0)),
            pl.BlockSpec((CTX, 2 * KV_W), lambda b, t: (ctx_blk + b, 0)),
            pl.BlockSpec((CTX, POOL_W), lambda b, t: (ctx_blk + b, 0)),
            pl.BlockSpec((None, len(POOL_WINDOWS), POOL_G, POOL_G), lambda b, t: (layer, 0, 0, 0)),
            pl.BlockSpec((None, 1, POOL_W), lambda b, t: (layer, 0, 0)),
        ],
        out_specs=pl.BlockSpec((CTX, D), lambda b, t: (b, 0)),
        out_shape=jax.ShapeDtypeStruct((NC, D), BF16),
        compiler_params=_cparams(2),
        name="mixer_context",
    )(sink_l, q, kk, vv, u, w_pool, pool_scale)


def _post_norm(h, upd, gate, g, b):
    return _layernorm(ALPHA * h + gate * upd) * g + b


ROW_TILE = D // LANES


def _store_row_tiles(ref, x):
    rows = x.shape[0]
    for c in range(ROW_TILE):
        ref[pl.ds(c, rows, stride=ROW_TILE), :] = x[:, c * LANES:(c + 1) * LANES]


def _load_row_tiles(ref, rows):
    return jnp.concatenate(
        [ref[pl.ds(c, rows, stride=ROW_TILE), :] for c in range(ROW_TILE)], axis=1)


def _route_top2(a, router_ref, ri_ref, rw_ref):
    a_hi = a.astype(BF16)
    a_lo = (a - a_hi.astype(F32)).astype(BF16)
    t_hi = jnp.dot(a_hi, router_ref[...], preferred_element_type=F32)
    t_lo = jnp.dot(a_lo, router_ref[:, :LANES], preferred_element_type=F32)
    logits = t_hi[:, :LANES] + t_hi[:, LANES:] + t_lo
    lane = lax.broadcasted_iota(I32, logits.shape, 1)
    lane_f = lane.astype(F32)
    logits = jnp.where(lane < N_EXPERTS, logits, -jnp.inf)
    m1 = jnp.max(logits, axis=-1, keepdims=True)
    i1 = jnp.min(jnp.where(logits == m1, lane_f, float(LANES)), axis=-1, keepdims=True)
    rest = jnp.where(lane_f == i1, -jnp.inf, logits)
    m2 = jnp.max(rest, axis=-1, keepdims=True)
    i2 = jnp.min(jnp.where(rest == m2, lane_f, float(LANES)), axis=-1, keepdims=True)
    e = jnp.exp(m2 - m1)
    w1 = 1.0 / (1.0 + e)
    w2 = e / (1.0 + e)
    ri_ref[...] = jnp.where(lane == 0, i1, jnp.where(lane == 1, i2, 0.0)).astype(I32)
    rw_ref[...] = jnp.where(lane == 0, w1, jnp.where(lane == 1, w2, 0.0))


def _outproj_kernel(mixl_ref, mixc_ref, h_ref, mod_ref, w_ref, lng_ref, lnb_ref, *rest,
                    moe, n_lat, has_ctx):
    if moe:
        router_ref, h1_ref, a2_ref, ri_ref, rw_ref = rest
    else:
        h1_ref, a2_ref = rest
    split = TM if moe else ROW_SPLIT
    for r0 in range(0, TM, split):
        rs = slice(r0, r0 + split)
        mix = mixl_ref[rs, :]
        if has_ctx:
            mix = jnp.where(pl.program_id(0) >= n_lat, mixc_ref[rs, :], mix)
        y = jnp.dot(mix, w_ref[...], preferred_element_type=F32)
        h1 = _post_norm(h_ref[rs, :], y, mod_ref[2:3, :], lng_ref[0:1, :], lnb_ref[0:1, :])
        h1_ref[rs, :] = h1
        a2 = _layernorm(h1) * (1.0 + mod_ref[4:5, :]) + mod_ref[3:4, :]
        if moe:
            _store_row_tiles(a2_ref.at[pl.ds(r0 * ROW_TILE, split * ROW_TILE)], a2)
            _route_top2(a2, router_ref, ri_ref.at[pl.ds(r0, split)], rw_ref.at[pl.ds(r0, split)])
        else:
            a2_ref[rs, :] = a2.astype(BF16)


def _outproj(mix_lat, mix_ctx, h, mods, w_out, ln_g, ln_b, router_pad, layer, n_rows):
    n_tiles = n_rows // TM
    tiles_per_batch = SEQ // TM
    n_lat = NL // TM
    moe = router_pad is not None
    has_ctx = n_rows > NL
    in_specs = [
        pl.BlockSpec((TM, D), lambda i: (jnp.minimum(i, n_lat - 1), 0)),
        pl.BlockSpec((TM, D), lambda i: (jnp.maximum(i - n_lat, 0), 0)),
        pl.BlockSpec((TM, D), lambda i: (i, 0)),
        pl.BlockSpec((None, None, 6, D),
                     lambda i: (layer, _mod_row(i, tiles_per_batch, n_lat), 0, 0)),
        pl.BlockSpec((None, D, D), lambda i: (layer, 0, 0)),
        pl.BlockSpec((None, 2, D), lambda i: (layer, 0, 0)),
        pl.BlockSpec((None, 2, D), lambda i: (layer, 0, 0)),
    ]
    out_specs = [pl.BlockSpec((TM, D), lambda i: (i, 0)), pl.BlockSpec((TM, D), lambda i: (i, 0))]
    out_shape = [jax.ShapeDtypeStruct((n_rows, D), F32), jax.ShapeDtypeStruct((n_rows, D), BF16)]
    args = [mix_lat, mix_ctx, h, mods, w_out, ln_g, ln_b]
    if moe:
        in_specs.append(pl.BlockSpec((None, D, 2 * LANES), lambda i: (layer // 2, 0, 0)))
        out_specs[1] = pl.BlockSpec((TM * ROW_TILE, LANES), lambda i: (i, 0))
        out_shape[1] = jax.ShapeDtypeStruct((n_rows * ROW_TILE, LANES), F32)
        out_specs += [pl.BlockSpec((TM, LANES), lambda i: (i, 0)),
                      pl.BlockSpec((TM, LANES), lambda i: (i, 0))]
        out_shape += [jax.ShapeDtypeStruct((n_rows, LANES), I32),
                      jax.ShapeDtypeStruct((n_rows, LANES), F32)]
        args.append(router_pad)
    return pl.pallas_call(
        functools.partial(_outproj_kernel, moe=moe, n_lat=n_lat, has_ctx=has_ctx),
        grid=(n_tiles,),
        in_specs=in_specs,
        out_specs=out_specs,
        out_shape=out_shape,
        compiler_params=_cparams(1),
        name="outproj_moe" if moe else "outproj",
    )(*args)


def _swiglu_hidden(x, w1_ref, w3_ref, t_ref, d_ff):
    for c0 in range(0, d_ff, FF_CHUNK):
        c1 = min(c0 + FF_CHUNK, d_ff)
        g = jnp.dot(x, w1_ref[:, c0:c1], preferred_element_type=F32)
        u = jnp.dot(x, w3_ref[:, c0:c1], preferred_element_type=F32)
        t_ref[:, c0:c1] = (jax.nn.silu(g) * u).astype(BF16)


def _ffn_dense_kernel(a_ref, h_ref, mod_ref, w1_ref, w3_ref, w2_ref, lng_ref, lnb_ref, o_ref,
                      t_ref):
    _swiglu_hidden(a_ref[...], w1_ref, w3_ref, t_ref, D_FF_DENSE)
    f = jnp.dot(t_ref[...], w2_ref[...], preferred_element_type=F32)
    o_ref[...] = _post_norm(h_ref[...], f, mod_ref[5:6, :], lng_ref[1:2, :], lnb_ref[1:2, :])


def _ffn_dense(a2, h1, mods, w1, w3, w2, ln_g, ln_b, layer, n_rows):
    n_tiles = n_rows // TM
    tiles_per_batch = SEQ // TM
    n_lat = NL // TM
    idx = layer // 2
    resident = pl.Buffered(1)
    return pl.pallas_call(
        _ffn_dense_kernel,
        grid=(n_tiles,),
        in_specs=[
            pl.BlockSpec((TM, D), lambda i: (i, 0)),
            pl.BlockSpec((TM, D), lambda i: (i, 0)),
            pl.BlockSpec((None, None, 6, D),
                         lambda i: (layer, _mod_row(i, tiles_per_batch, n_lat), 0, 0)),
            pl.BlockSpec((None, D, D_FF_DENSE), lambda i: (idx, 0, 0), pipeline_mode=resident),
            pl.BlockSpec((None, D, D_FF_DENSE), lambda i: (idx, 0, 0), pipeline_mode=resident),
            pl.BlockSpec((None, D_FF_DENSE, D), lambda i: (idx, 0, 0), pipeline_mode=resident),
            pl.BlockSpec((None, 2, D), lambda i: (layer, 0, 0)),
            pl.BlockSpec((None, 2, D), lambda i: (layer, 0, 0)),
        ],
        out_specs=pl.BlockSpec((TM, D), lambda i: (i, 0)),
        out_shape=jax.ShapeDtypeStruct((n_rows, D), F32),
        scratch_shapes=[pltpu.VMEM((TM, D_FF_DENSE), BF16)],
        compiler_params=_cparams(1),
        name="ffn_dense",
    )(a2, h1, mods, w1, w3, w2, ln_g, ln_b)


def _stream_expert_weights(e, idx, w1_hbm, w3_hbm, w2_hbm, w1_ref, w3_ref, w2_ref,
                           st_col, st_row, wsem):
    n_c = D_FF_EXPERT // W_CHUNK
    jobs = ([(True, w1_hbm, w1_ref, c) for c in range(n_c)]
            + [(True, w3_hbm, w3_ref, c) for c in range(n_c)]
            + [(False, w2_hbm, w2_ref, c) for c in range(n_c)])

    def copy(j):
        by_col, src, _, c = jobs[j]
        slot = j % W_RING
        if by_col:
            return pltpu.make_async_copy(src.at[idx, e, :, pl.ds(c * W_CHUNK, W_CHUNK)],
                                         st_col.at[slot], wsem.at[slot])
        return pltpu.make_async_copy(src.at[idx, e, pl.ds(c * W_CHUNK, W_CHUNK), :],
                                     st_row.at[slot], wsem.at[slot])

    for j in range(W_RING - 1):
        copy(j).start()
    for j, (by_col, _, dst, c) in enumerate(jobs):
        if j + W_RING - 1 < len(jobs):
            copy(j + W_RING - 1).start()
        copy(j).wait()
        if by_col:
            dst[:, c * W_CHUNK:(c + 1) * W_CHUNK] = st_col[j % W_RING].astype(BF16)
        else:
            dst[c * W_CHUNK:(c + 1) * W_CHUNK, :] = st_row[j % W_RING].astype(BF16)


def _experts_kernel(pos_ref, te_ref, tv_ref, cnt_ref, off_ref, end_ref,
                    a_hbm, w1_hbm, w3_hbm, w2_hbm, o_ref, src_ref, xg_ref, t_ref,
                    w1_ref, w3_ref, w2_ref, st_col, st_row, sem, wsem,
                    *, n_assign, n_slots, idx):
    i = pl.program_id(0)
    n = pl.num_programs(0)

    def start_row(tile, slot, r):
        s = pl.multiple_of(src_ref[tile * MOE_TM + r] * ROW_TILE, ROW_TILE)
        d = r * ROW_TILE if isinstance(r, int) else pl.multiple_of(r * ROW_TILE, ROW_TILE)
        pltpu.make_async_copy(a_hbm.at[pl.ds(s, ROW_TILE)],
                              xg_ref.at[slot, pl.ds(d, ROW_TILE)], sem.at[slot]).start()

    def wait_tile(slot):
        pltpu.make_async_copy(a_hbm.at[pl.ds(0, MOE_TM * ROW_TILE)], xg_ref.at[slot],
                              sem.at[slot]).wait()

    def clear(s, carry):
        src_ref[s] = 0
        return carry

    @pl.when(i == 0)
    def _():
        for e in range(N_EXPERTS):
            lax.fori_loop(off_ref[e] + cnt_ref[e], end_ref[e], clear, 0)
        lax.fori_loop(end_ref[N_EXPERTS - 1], n_slots, clear, 0)

        def fill(a, carry):
            src_ref[pos_ref[a]] = lax.shift_right_logical(a, 1)
            return carry

        lax.fori_loop(0, n_assign, fill, 0, unroll=8)

        def first(r, carry):
            start_row(0, 0, r)
            return carry

        lax.fori_loop(0, MOE_TM, first, 0, unroll=8)

    valid = tv_ref[i] > 0
    slot = i % 2
    prev = jnp.maximum(i - 1, 0)

    @pl.when(jnp.logical_and(valid, jnp.logical_or(i == 0, te_ref[i] != te_ref[prev])))
    def _():
        _stream_expert_weights(te_ref[i], idx, w1_hbm, w3_hbm, w2_hbm, w1_ref, w3_ref, w2_ref,
                               st_col, st_row, wsem)

    @pl.when(jnp.logical_or(i == 0, tv_ref[prev] > 0))
    def _():
        wait_tile(slot)

    @pl.when(valid)
    def _():
        x = _load_row_tiles(xg_ref.at[slot], MOE_TM).astype(BF16)
        nxt_tile = lax.rem(i + 1, n)
        chunks = [(c0, min(c0 + FF_CHUNK, D_FF_EXPERT)) for c0 in range(0, D_FF_EXPERT, FF_CHUNK)]
        per_chunk = MOE_TM // len(chunks)
        for ci, (c0, c1) in enumerate(chunks):
            for r in range(ci * per_chunk, (ci + 1) * per_chunk):
                start_row(nxt_tile, 1 - slot, r)
            g = jnp.dot(x, w1_ref[:, c0:c1], preferred_element_type=F32)
            u = jnp.dot(x, w3_ref[:, c0:c1], preferred_element_type=F32)
            t_ref[:, c0:c1] = (jax.nn.silu(g) * u).astype(BF16)
        y = jnp.dot(t_ref[...], w2_ref[...], preferred_element_type=F32)
        _store_row_tiles(o_ref, y)

    @pl.when(jnp.logical_not(valid))
    def _():
        o_ref[...] = jnp.zeros_like(o_ref)

    @pl.when(jnp.logical_and(valid, i == n - 1))
    def _():
        wait_tile(1 - slot)


def _experts(plan, a2r, w1, w3, w2, idx, n_rows, n_slots):
    n_tiles = n_slots // MOE_TM
    return pl.pallas_call(
        functools.partial(_experts_kernel, n_assign=2 * n_rows, n_slots=n_slots, idx=idx),
        grid_spec=pltpu.PrefetchScalarGridSpec(
            num_scalar_prefetch=6,
            grid=(n_tiles,),
            in_specs=[pl.BlockSpec(memory_space=pl.ANY)] * 4,
            out_specs=pl.BlockSpec((MOE_TM * ROW_TILE, LANES), lambda i, *_: (i, 0)),
            scratch_shapes=[
                pltpu.SMEM((n_slots,), I32),
                pltpu.VMEM((2, MOE_TM * ROW_TILE, LANES), F32),
                pltpu.VMEM((MOE_TM, D_FF_EXPERT), BF16),
                pltpu.VMEM((D, D_FF_EXPERT), BF16),
                pltpu.VMEM((D, D_FF_EXPERT), BF16),
                pltpu.VMEM((D_FF_EXPERT, D), BF16),
                pltpu.VMEM((W_RING, D, W_CHUNK), F32),
                pltpu.VMEM((W_RING, W_CHUNK, D), F32),
                pltpu.SemaphoreType.DMA((2,)),
                pltpu.SemaphoreType.DMA((W_RING,)),
            ],
        ),
        out_shape=jax.ShapeDtypeStruct((n_slots * ROW_TILE, LANES), F32),
        compiler_params=_cparams(1),
        name="moe_experts",
    )(*plan, a2r, w1, w3, w2)


def _combine_kernel(pos_ref, y_hbm, h_ref, rw_ref, mod_ref, lng_ref, lnb_ref, o_ref, buf, sem):
    i = pl.program_id(0)
    n = pl.num_programs(0)

    def start_row(step, slot, r):
        d = r * ROW_TILE if isinstance(r, int) else pl.multiple_of(r * ROW_TILE, ROW_TILE)
        for k in range(2):
            p = pl.multiple_of(pos_ref[2 * (step * COMBINE_ROWS + r) + k] * ROW_TILE, ROW_TILE)
            pltpu.make_async_copy(y_hbm.at[pl.ds(p, ROW_TILE)],
                                  buf.at[slot, k, pl.ds(d, ROW_TILE)], sem.at[slot]).start()

    def wait_tile(slot):
        for k in range(2):
            pltpu.make_async_copy(y_hbm.at[pl.ds(0, COMBINE_ROWS * ROW_TILE)], buf.at[slot, k],
                                  sem.at[slot]).wait()

    ahead = COMBINE_SLOTS - 1

    @pl.when(i == 0)
    def _():
        for t in range(ahead):
            def first(r, carry):
                start_row(t, t, r)
                return carry

            lax.fori_loop(0, COMBINE_ROWS, first, 0, unroll=4)

    slot = lax.rem(i, COMBINE_SLOTS)
    wait_tile(slot)
    nxt = lax.rem(i + ahead, n)
    nxt_slot = lax.rem(i + ahead, COMBINE_SLOTS)
    chunk = COMBINE_ROWS // COMBINE_CHUNKS
    for c in range(COMBINE_CHUNKS):
        for r in range(c * chunk, (c + 1) * chunk):
            start_row(nxt, nxt_slot, r)
        rs = slice(c * chunk, (c + 1) * chunk)
        tiles = pl.ds(c * chunk * ROW_TILE, chunk * ROW_TILE)
        w = rw_ref[rs, :]
        f = (w[:, 0:1] * _load_row_tiles(buf.at[slot, 0, tiles], chunk)
             + w[:, 1:2] * _load_row_tiles(buf.at[slot, 1, tiles], chunk))
        o_ref[rs, :] = _post_norm(h_ref[rs, :], f, mod_ref[5:6, :], lng_ref[1:2, :],
                                  lnb_ref[1:2, :])

    @pl.when(i == n - 1)
    def _():
        for t in range(1, COMBINE_SLOTS):
            wait_tile(lax.rem(i + t, COMBINE_SLOTS))


def _combine(pos, y, h1, rw, mods, ln_g, ln_b, layer, n_rows):
    n_tiles = n_rows // COMBINE_ROWS
    tiles_per_batch = SEQ // COMBINE_ROWS
    n_lat = NL // COMBINE_ROWS
    return pl.pallas_call(
        _combine_kernel,
        grid_spec=pltpu.PrefetchScalarGridSpec(
            num_scalar_prefetch=1,
            grid=(n_tiles,),
            in_specs=[
                pl.BlockSpec(memory_space=pl.ANY),
                pl.BlockSpec((COMBINE_ROWS, D), lambda i, p: (i, 0)),
                pl.BlockSpec((COMBINE_ROWS, LANES), lambda i, p: (i, 0)),
                pl.BlockSpec((None, None, 6, D),
                             lambda i, p: (layer, _mod_row(i, tiles_per_batch, n_lat), 0, 0)),
                pl.BlockSpec((None, 2, D), lambda i, p: (layer, 0, 0)),
                pl.BlockSpec((None, 2, D), lambda i, p: (layer, 0, 0)),
            ],
            out_specs=pl.BlockSpec((COMBINE_ROWS, D), lambda i, p: (i, 0)),
            scratch_shapes=[pltpu.VMEM((COMBINE_SLOTS, 2, COMBINE_ROWS * ROW_TILE, LANES), F32),
                            pltpu.SemaphoreType.DMA((COMBINE_SLOTS,))],
        ),
        out_shape=jax.ShapeDtypeStruct((n_rows, D), F32),
        compiler_params=_cparams(1),
        name="moe_combine",
    )(pos, y, h1, rw, mods, ln_g, ln_b)


def _route_plan(eid, n_rows, n_slots):
    n_tiles = n_slots // MOE_TM
    e_flat = eid.reshape(-1)
    onehot = (e_flat[:, None] == jnp.arange(N_EXPERTS, dtype=I32)[None, :]).astype(I32)
    csum = jnp.cumsum(onehot, axis=0)
    rank = jnp.sum((csum - onehot) * onehot, axis=1)
    count = csum[-1]
    padded = ((count + MOE_TM - 1) // MOE_TM) * MOE_TM
    ends = jnp.cumsum(padded)
    offs = ends - padded
    pos = jnp.sum(onehot * offs[None, :], axis=1) + rank
    tile_start = jnp.arange(n_tiles, dtype=I32) * MOE_TM
    tile_e = jnp.minimum(jnp.sum((tile_start[:, None] >= ends[None, :]).astype(I32), axis=1),
                         N_EXPERTS - 1)
    tile_v = (tile_start < ends[-1]).astype(I32)
    return (pos.astype(I32), tile_e.astype(I32), tile_v, count.astype(I32), offs.astype(I32),
            ends.astype(I32))


def _moe(a2r, h1, ri, rw, mods, w1, w3, w2, ln_g, ln_b, layer, n_rows):
    n_slots = 2 * n_rows + N_EXPERTS * MOE_TM
    plan = _route_plan(ri[:, :2], n_rows, n_slots)
    ys = _experts(plan, a2r, w1, w3, w2, layer // 2, n_rows, n_slots)
    return _combine(plan[0], ys, h1, rw, mods, ln_g, ln_b, layer, n_rows)


def kernel(x, c, ctx, c_ctx, w_ada, b_ada, w_in, w_pool, pool_scale, sink, w_out, ln_g, ln_b,
           dense_w1, dense_w3, dense_w2, router, moe_w1, moe_w3, moe_w2):
    cond = jnp.concatenate([c, c_ctx[None, :], jnp.zeros((MOD_ROWS - B - 1, D), F32)], axis=0)
    mods = _ada_tables(cond, w_ada, b_ada)
    cos_t, sin_t = _rope_tables()

    w_in_b = w_in.astype(BF16)
    w_pool_b = w_pool.astype(BF16)
    w_out_b = w_out.astype(BF16)
    dw1, dw3, dw2 = dense_w1.astype(BF16), dense_w3.astype(BF16), dense_w2.astype(BF16)
    r_hi = router.astype(BF16)
    r_lo = (router - r_hi.astype(F32)).astype(BF16)
    lane_pad = ((0, 0), (0, 0), (0, LANES - N_EXPERTS))
    router_pad = jnp.concatenate([jnp.pad(r_hi, lane_pad), jnp.pad(r_lo, lane_pad)], axis=-1)
    pool_scale3 = pool_scale.reshape(DEPTH, 1, POOL_W)

    h = jnp.concatenate([x.reshape(NL, D), ctx.reshape(NC, D)], axis=0)
    for l in range(DEPTH):
        last = l == DEPTH - 1
        n_rows = NL if last else NT
        q, kk, vv, u = _inproj(h, mods, w_in_b, cos_t, sin_t, l)
        mix = _mixer_latent(q, kk, vv, u, sink[l], w_pool_b, pool_scale3, l)
        mix_c = mix if last else _mixer_context(q, kk, vv, u, sink[l], w_pool_b, pool_scale3, l)
        if l % 2 == 0:
            h1, a2 = _outproj(mix, mix_c, h, mods, w_out_b, ln_g, ln_b, None, l, n_rows)
            h = _ffn_dense(a2, h1, mods, dw1, dw3, dw2, ln_g, ln_b, l, n_rows)
        else:
            h1, a2, ri, rw = _outproj(mix, mix_c, h, mods, w_out_b, ln_g, ln_b, router_pad, l,
                                      n_rows)
            h = _moe(a2, h1, ri, rw, mods, moe_w1, moe_w3, moe_w2, ln_g, ln_b, l, n_rows)
    return h.reshape(B, SEQ, D)
```

```python
import functools

import jax
import jax.numpy as jnp
from jax import lax
from jax.experimental import pallas as pl
from jax.experimental.pallas import tpu as pltpu

F32 = jnp.float32
BF16 = jnp.bfloat16
I32 = jnp.int32

D = 1024
B = 8
SEQ = 2048
DEPTH = 4
CTX = 256
GRID_W = 64
HEAD_DIM = 64
N_Q_HEADS = 8
N_KV_HEADS = 2
ATTN_W = N_Q_HEADS * HEAD_DIM
KV_W = N_KV_HEADS * HEAD_DIM
POOL_WINDOWS = (2, 4, 8, 16)
POOL_W = D - ATTN_W
POOL_G = POOL_W // len(POOL_WINDOWS)
IN_W = ATTN_W + 2 * KV_W + POOL_W
WINDOW = 128
ROPE_BASE = 10000.0
ROPE_FREQS = HEAD_DIM // 4
D_FF_DENSE = 2816
N_EXPERTS = 8
D_FF_EXPERT = 3584
ALPHA = (2 * DEPTH) ** 0.25
LN_EPS = 1e-6
NEG_INF = -1e30
LOG2E = 1.4426950408889634

NL = B * SEQ
NC = B * CTX
NT = NL + NC
MOD_ROWS = 16
LANES = 128

TM = 512
TQ = 128
TQ_STEP = 256
ROW_SPLIT = 256
MOE_TM = 512
FF_CHUNK = 1024
W_CHUNK = 256
W_RING = 4
COMBINE_ROWS = 256
COMBINE_CHUNKS = 4
COMBINE_SLOTS = 3

VMEM_LIMIT = 56 * 1024 * 1024


def _cparams(n_axes, vmem=VMEM_LIMIT):
    return pltpu.CompilerParams(
        dimension_semantics=("arbitrary",) * n_axes, vmem_limit_bytes=vmem)


def _layernorm(x):
    mu = jnp.mean(x, axis=-1, keepdims=True)
    xc = x - mu
    var = jnp.mean(xc * xc, axis=-1, keepdims=True)
    return xc * lax.rsqrt(var + LN_EPS)


def _mod_row(tile, tiles_per_batch, n_latent_tiles):
    return jnp.where(tile < n_latent_tiles, tile // tiles_per_batch, B)


ADA_TN = 1536


def _ada_kernel(s_ref, w_ref, b_ref, o_ref):
    s = jax.nn.silu(s_ref[...]).astype(BF16)
    w = w_ref[...].astype(BF16)
    o_ref[...] = jnp.dot(s, w, preferred_element_type=F32) + b_ref[...]


def _ada_tables(cond, w_ada, b_ada):
    n_col = (6 * D) // ADA_TN
    out = pl.pallas_call(
        _ada_kernel,
        grid=(DEPTH, n_col),
        in_specs=[
            pl.BlockSpec((MOD_ROWS, D), lambda l, j: (0, 0)),
            pl.BlockSpec((None, D, ADA_TN), lambda l, j: (l, 0, j)),
            pl.BlockSpec((None, 1, ADA_TN), lambda l, j: (l, 0, j)),
        ],
        out_specs=pl.BlockSpec((None, MOD_ROWS, ADA_TN), lambda l, j: (l, 0, j)),
        out_shape=jax.ShapeDtypeStruct((DEPTH, MOD_ROWS, 6 * D), F32),
        compiler_params=_cparams(2),
        name="ada_tables",
    )(cond, w_ada, b_ada.reshape(DEPTH, 1, 6 * D))
    return out.reshape(DEPTH, MOD_ROWS, 6, D)


def _inproj_kernel(h_ref, mod_ref, w_ref, cos_ref, sin_ref,
                   q_ref, kk_ref, vv_ref, u_ref):
    lane = lax.broadcasted_iota(I32, (1, LANES), 1)
    first_half = (lane % 32) < 16
    low = lane < HEAD_DIM
    for r0 in range(0, TM, ROW_SPLIT):
        rs = slice(r0, r0 + ROW_SPLIT)
        a = _layernorm(h_ref[rs, :]) * (1.0 + mod_ref[1:2, :]) + mod_ref[0:1, :]
        p = jnp.dot(a.astype(BF16), w_ref[...], preferred_element_type=F32)
        cos = cos_ref[rs, :]
        sin = sin_ref[rs, :]

        def rope(x):
            swapped = jnp.where(first_half, pltpu.roll(x, LANES - 16, 1), pltpu.roll(x, 16, 1))
            return x * cos + swapped * sin

        for c in range(ATTN_W // LANES):
            qc = rope(p[:, c * LANES:(c + 1) * LANES]) * (HEAD_DIM ** -0.5 * LOG2E)
            q_ref[rs, c * LANES:(c + 1) * LANES] = qc.astype(BF16)
        k = rope(p[:, ATTN_W:ATTN_W + KV_W])
        k_sw = pltpu.roll(k, HEAD_DIM, 1)
        kk_ref[rs, 0:LANES] = jnp.where(low, k, k_sw).astype(BF16)
        kk_ref[rs, LANES:2 * LANES] = jnp.where(low, k_sw, k).astype(BF16)
        v = p[:, ATTN_W + KV_W:ATTN_W + 2 * KV_W]
        v_sw = pltpu.roll(v, HEAD_DIM, 1)
        vv_ref[rs, 0:LANES] = jnp.where(low, v, v_sw).astype(BF16)
        vv_ref[rs, LANES:2 * LANES] = jnp.where(low, v_sw, v).astype(BF16)
        u_ref[rs, :] = p[:, ATTN_W + 2 * KV_W:]


def _inproj(h, mods, w_in, cos_t, sin_t, layer):
    n_tiles = NT // TM
    tiles_per_batch = SEQ // TM
    n_lat = NL // TM

    def rope_idx(i):
        return (jnp.where(i < n_lat, i % tiles_per_batch, tiles_per_batch), 0)

    return pl.pallas_call(
        _inproj_kernel,
        grid=(n_tiles,),
        in_specs=[
            pl.BlockSpec((TM, D), lambda i: (i, 0)),
            pl.BlockSpec((None, None, 6, D),
                         lambda i: (layer, _mod_row(i, tiles_per_batch, n_lat), 0, 0)),
            pl.BlockSpec((None, D, IN_W), lambda i: (layer, 0, 0)),
            pl.BlockSpec((TM, LANES), rope_idx),
            pl.BlockSpec((TM, LANES), rope_idx),
        ],
        out_specs=[
            pl.BlockSpec((TM, ATTN_W), lambda i: (i, 0)),
            pl.BlockSpec((TM, 2 * KV_W), lambda i: (i, 0)),
            pl.BlockSpec((TM, 2 * KV_W), lambda i: (i, 0)),
            pl.BlockSpec((TM, POOL_W), lambda i: (i, 0)),
        ],
        out_shape=[
            jax.ShapeDtypeStruct((NT, ATTN_W), BF16),
            jax.ShapeDtypeStruct((NT, 2 * KV_W), BF16),
            jax.ShapeDtypeStruct((NT, 2 * KV_W), BF16),
            jax.ShapeDtypeStruct((NT, POOL_W), F32),
        ],
        compiler_params=_cparams(1),
        name="inproj",
    )(h, mods, w_in, cos_t, sin_t)


def _rope_tables():
    rows = SEQ // GRID_W
    row = jnp.repeat(jnp.arange(rows, dtype=F32), GRID_W)
    col = jnp.tile(jnp.arange(GRID_W, dtype=F32), rows)
    inv = ROPE_BASE ** (-jnp.arange(ROPE_FREQS, dtype=F32) / ROPE_FREQS)
    ang_r = row[:, None] * inv[None, :]
    ang_c = col[:, None] * inv[None, :]
    cr, sr, cc, sc = jnp.cos(ang_r), jnp.sin(ang_r), jnp.cos(ang_c), jnp.sin(ang_c)
    cos_h = jnp.concatenate([cr, cr, cc, cc], axis=-1)
    sin_h = jnp.concatenate([-sr, sr, -sc, sc], axis=-1)
    cos_t = jnp.concatenate([jnp.tile(cos_h, (1, 2)), jnp.ones((TM, LANES), F32)], axis=0)
    sin_t = jnp.concatenate([jnp.tile(sin_h, (1, 2)), jnp.zeros((TM, LANES), F32)], axis=0)
    return cos_t, sin_t


def _pool_group(top, mid, bot, t0, seq_len, width):
    rows = mid.shape[0]
    slab = jnp.concatenate([top, mid, bot], axis=0)
    half = width // 2
    s = slab
    span = 1
    while span < width:
        n = s.shape[0] - span
        s = s[0:n] + s[span:span + n]
        span *= 2
    start = 8 - half
    total = s[start:start + rows]
    pos = t0 + lax.broadcasted_iota(I32, (rows, 1), 0)
    hi = jnp.minimum(pos + half, seq_len)
    lo = jnp.maximum(pos - half, 0)
    cnt = (hi - lo).astype(F32)
    return total / cnt - mid


def _mixer_kernel(sink_ref, q_ref, kk_ref, vv_ref, kkc_ref, vvc_ref, u_ref, wp_ref, ps_ref,
                  o_ref, *, tq, sub, seq_len, band):
    for sb in range(tq // sub):
        _mixer_block(sink_ref, q_ref, kk_ref, vv_ref, kkc_ref, vvc_ref, u_ref, wp_ref, ps_ref,
                     o_ref, t0=pl.program_id(1) * tq + sb * sub, r0=sb * sub, tq=sub,
                     seq_len=seq_len, band=band)


def _mixer_block(sink_ref, q_ref, kk_ref, vv_ref, kkc_ref, vvc_ref, u_ref, wp_ref, ps_ref,
                 o_ref, *, t0, r0, tq, seq_len, band):
    lane = lax.broadcasted_iota(I32, (1, LANES), 1)
    low = lane < HEAD_DIM

    if band:
        n_band = tq + 2 * WINDOW
        ks = jnp.clip(t0 - WINDOW, 0, seq_len - n_band)
        ks = pl.multiple_of(ks, LANES)
        kk = jnp.concatenate([kk_ref[pl.ds(ks, n_band), :], kkc_ref[...]], axis=0)
        vv = jnp.concatenate([vv_ref[pl.ds(ks, n_band), :], vvc_ref[...]], axis=0)
        qpos = t0 + lax.broadcasted_iota(I32, (tq, n_band), 0)
        kpos = ks + lax.broadcasted_iota(I32, (tq, n_band), 1)
        bias = jnp.where(jnp.abs(kpos - qpos) <= WINDOW, 0.0, NEG_INF)
    else:
        n_band = 0
        kk = kkc_ref[...]
        vv = vvc_ref[...]
        bias = None
    nk = kk.shape[0]
    zero = jnp.zeros_like(kk[:, 0:LANES])
    low_f = jnp.broadcast_to(jnp.where(low, 1.0, 0.0), (nk, LANES))
    ones_st = jnp.concatenate([low_f, 1.0 - low_f], axis=0).astype(BF16)

    for g in range(N_KV_HEADS):
        kg = kk[:, g * LANES:(g + 1) * LANES]
        vg = vv[:, g * LANES:(g + 1) * LANES]
        k_st = jnp.concatenate([jnp.where(low, kg, zero), jnp.where(low, zero, kg)], axis=0)
        v_st = jnp.concatenate([jnp.where(low, vg, zero), jnp.where(low, zero, vg)], axis=0)
        v_st = jnp.concatenate([v_st, ones_st], axis=1)
        for pr in range(2):
            j = 2 * g + pr
            qp = q_ref[r0:r0 + tq, j * LANES:(j + 1) * LANES]
            s = lax.dot_general(qp, k_st, (((1,), (1,)), ((), ())), preferred_element_type=F32)
            ps = []
            sink_terms = []
            for hh in range(2):
                sink = sink_ref[2 * j + hh] * LOG2E
                sh = s[:, hh * nk:(hh + 1) * nk]
                if bias is not None:
                    parts = [sh[:, :n_band] + bias, sh[:, n_band:]]
                else:
                    parts = [sh]
                m = sink
                for part in parts:
                    m = jnp.maximum(jnp.max(part, axis=-1, keepdims=True), m)
                sink_terms.append(jnp.exp2(sink - m))
                ps += [jnp.exp2(part - m).astype(BF16) for part in parts]
            od = jnp.dot(jnp.concatenate(ps, axis=1), v_st, preferred_element_type=F32)
            den = od[:, LANES:] + jnp.where(low, sink_terms[0], sink_terms[1])
            o_ref[r0:r0 + tq, j * LANES:(j + 1) * LANES] = (od[:, :LANES] / den).astype(BF16)

    zeros8 = jnp.zeros((8, POOL_W), F32)
    if band:
        t0a = pl.multiple_of(t0, 8)
        mid = u_ref[pl.ds(t0a, tq), :]
        top_s = pl.multiple_of(jnp.maximum(t0 - 8, 0), 8)
        bot_s = pl.multiple_of(jnp.minimum(t0 + tq, seq_len - 8), 8)
        top = jnp.where(t0 > 0, u_ref[pl.ds(top_s, 8), :], zeros8)
        bot = jnp.where(t0 + tq < seq_len, u_ref[pl.ds(bot_s, 8), :], zeros8)
    else:
        mid = u_ref[...]
        top = zeros8
        bot = zeros8
    for g, width in enumerate(POOL_WINDOWS):
        sl = slice(g * POOL_G, (g + 1) * POOL_G)
        diff = _pool_group(top[:, sl], mid[:, sl], bot[:, sl], t0, seq_len, width)
        mixed = jnp.dot(diff.astype(BF16), wp_ref[g], preferred_element_type=F32)
        mixed = mixed * ps_ref[:, sl]
        o_ref[r0:r0 + tq, ATTN_W + g * POOL_G:ATTN_W + (g + 1) * POOL_G] = mixed.astype(BF16)


def _mixer_latent(q, kk, vv, u, sink_l, w_pool, pool_scale, layer):
    nq = SEQ // TQ_STEP
    ctx_blk = NL // CTX
    kern = functools.partial(_mixer_kernel, tq=TQ_STEP, sub=TQ, seq_len=SEQ, band=True)
    return pl.pallas_call(
        kern,
        grid=(B, nq),
        in_specs=[
            pl.BlockSpec(memory_space=pltpu.SMEM),
            pl.BlockSpec((TQ_STEP, ATTN_W), lambda b, t: (b * nq + t, 0)),
            pl.BlockSpec((SEQ, 2 * KV_W), lambda b, t: (b, 0)),
            pl.BlockSpec((SEQ, 2 * KV_W), lambda b, t: (b, 0)),
            pl.BlockSpec((CTX, 2 * KV_W), lambda b, t: (ctx_blk + b, 0)),
            pl.BlockSpec((CTX, 2 * KV_W), lambda b, t: (ctx_blk + b, 0)),
            pl.BlockSpec((SEQ, POOL_W), lambda b, t: (b, 0)),
            pl.BlockSpec((None, len(POOL_WINDOWS), POOL_G, POOL_G), lambda b, t: (layer, 0, 0, 0)),
            pl.BlockSpec((None, 1, POOL_W), lambda b, t: (layer, 0, 0)),
        ],
        out_specs=pl.BlockSpec((TQ_STEP, D), lambda b, t: (b * nq + t, 0)),
        out_shape=jax.ShapeDtypeStruct((NL, D), BF16),
        compiler_params=_cparams(2),
        name="mixer_latent",
    )(sink_l, q, kk, vv, kk, vv, u, w_pool, pool_scale)


def _mixer_context(q, kk, vv, u, sink_l, w_pool, pool_scale, layer):
    ctx_blk = NL // CTX
    kern = functools.partial(_mixer_kernel, tq=CTX, sub=CTX, seq_len=CTX, band=False)

    def body(sink_ref, q_ref, kkc_ref, vvc_ref, u_ref, wp_ref, ps_ref, o_ref):
        kern(sink_ref, q_ref, None, None, kkc_ref, vvc_ref, u_ref, wp_ref, ps_ref, o_ref)

    return pl.pallas_call(
        body,
        grid=(B, 1),
        in_specs=[
            pl.BlockSpec(memory_space=pltpu.SMEM),
            pl.BlockSpec((CTX, ATTN_W), lambda b, t: (ctx_blk + b, 0)),
            pl.BlockSpec((CTX, 2 * KV_W), lambda b, t: (ctx_blk + b, 0)),
            pl.BlockSpec((CTX, 2 * KV_W), lambda b, t: (ctx_blk + b, 0)),
            pl.BlockSpec((CTX, POOL_W), lambda b, t: (ctx_blk + b, 0)),
            pl.BlockSpec((None, len(POOL_WINDOWS), POOL_G, POOL_G), lambda b, t: (layer, 0, 0, 0)),
            pl.BlockSpec((None, 1, POOL_W), lambda b, t: (layer, 0, 0)),
        ],
        out_specs=pl.BlockSpec((CTX, D), lambda b, t: (b, 0)),
        out_shape=jax.ShapeDtypeStruct((NC, D), BF16),
        compiler_params=_cparams(2),
        name="mixer_context",
    )(sink_l, q, kk, vv, u, w_pool, pool_scale)


def _post_norm(h, upd, gate, g, b):
    return _layernorm(ALPHA * h + gate * upd) * g + b


ROW_TILE = D // LANES


def _store_row_tiles(ref, x):
    rows = x.shape[0]
    for c in range(ROW_TILE):
        ref[pl.ds(c, rows, stride=ROW_TILE), :] = x[:, c * LANES:(c + 1) * LANES]


def _load_row_tiles(ref, rows):
    return jnp.concatenate(
        [ref[pl.ds(c, rows, stride=ROW_TILE), :] for c in range(ROW_TILE)], axis=1)


def _route_top2(a, router_ref, ri_ref, rw_ref):
    a_hi = a.astype(BF16)
    a_lo = (a - a_hi.astype(F32)).astype(BF16)
    t_hi = jnp.dot(a_hi, router_ref[...], preferred_element_type=F32)
    t_lo = jnp.dot(a_lo, router_ref[:, :LANES], preferred_element_type=F32)
    logits = t_hi[:, :LANES] + t_hi[:, LANES:] + t_lo
    lane = lax.broadcasted_iota(I32, logits.shape, 1)
    lane_f = lane.astype(F32)
    logits = jnp.where(lane < N_EXPERTS, logits, -jnp.inf)
    m1 = jnp.max(logits, axis=-1, keepdims=True)
    i1 = jnp.min(jnp.where(logits == m1, lane_f, float(LANES)), axis=-1, keepdims=True)
    rest = jnp.where(lane_f == i1, -jnp.inf, logits)
    m2 = jnp.max(rest, axis=-1, keepdims=True)
    i2 = jnp.min(jnp.where(rest == m2, lane_f, float(LANES)), axis=-1, keepdims=True)
    e = jnp.exp(m2 - m1)
    w1 = 1.0 / (1.0 + e)
    w2 = e / (1.0 + e)
    ri_ref[...] = jnp.where(lane == 0, i1, jnp.where(lane == 1, i2, 0.0)).astype(I32)
    rw_ref[...] = jnp.where(lane == 0, w1, jnp.where(lane == 1, w2, 0.0))


def _outproj_kernel(mixl_ref, mixc_ref, h_ref, mod_ref, w_ref, lng_ref, lnb_ref, *rest,
                    moe, n_lat, has_ctx):
    if moe:
        router_ref, h1_ref, a2_ref, ri_ref, rw_ref = rest
    else:
        h1_ref, a2_ref = rest
    split = TM if moe else ROW_SPLIT
    for r0 in range(0, TM, split):
        rs = slice(r0, r0 + split)
        mix = mixl_ref[rs, :]
        if has_ctx:
            mix = jnp.where(pl.program_id(0) >= n_lat, mixc_ref[rs, :], mix)
        y = jnp.dot(mix, w_ref[...], preferred_element_type=F32)
        h1 = _post_norm(h_ref[rs, :], y, mod_ref[2:3, :], lng_ref[0:1, :], lnb_ref[0:1, :])
        h1_ref[rs, :] = h1
        a2 = _layernorm(h1) * (1.0 + mod_ref[4:5, :]) + mod_ref[3:4, :]
        if moe:
            _store_row_tiles(a2_ref.at[pl.ds(r0 * ROW_TILE, split * ROW_TILE)], a2)
            _route_top2(a2, router_ref, ri_ref.at[pl.ds(r0, split)], rw_ref.at[pl.ds(r0, split)])
        else:
            a2_ref[rs, :] = a2.astype(BF16)


def _outproj(mix_lat, mix_ctx, h, mods, w_out, ln_g, ln_b, router_pad, layer, n_rows):
    n_tiles = n_rows // TM
    tiles_per_batch = SEQ // TM
    n_lat = NL // TM
    moe = router_pad is not None
    has_ctx = n_rows > NL
    in_specs = [
        pl.BlockSpec((TM, D), lambda i: (jnp.minimum(i, n_lat - 1), 0)),
        pl.BlockSpec((TM, D), lambda i: (jnp.maximum(i - n_lat, 0), 0)),
        pl.BlockSpec((TM, D), lambda i: (i, 0)),
        pl.BlockSpec((None, None, 6, D),
                     lambda i: (layer, _mod_row(i, tiles_per_batch, n_lat), 0, 0)),
        pl.BlockSpec((None, D, D), lambda i: (layer, 0, 0)),
        pl.BlockSpec((None, 2, D), lambda i: (layer, 0, 0)),
        pl.BlockSpec((None, 2, D), lambda i: (layer, 0, 0)),
    ]
    out_specs = [pl.BlockSpec((TM, D), lambda i: (i, 0)), pl.BlockSpec((TM, D), lambda i: (i, 0))]
    out_shape = [jax.ShapeDtypeStruct((n_rows, D), F32), jax.ShapeDtypeStruct((n_rows, D), BF16)]
    args = [mix_lat, mix_ctx, h, mods, w_out, ln_g, ln_b]
    if moe:
        in_specs.append(pl.BlockSpec((None, D, 2 * LANES), lambda i: (layer // 2, 0, 0)))
        out_specs[1] = pl.BlockSpec((TM * ROW_TILE, LANES), lambda i: (i, 0))
        out_shape[1] = jax.ShapeDtypeStruct((n_rows * ROW_TILE, LANES), F32)
        out_specs += [pl.BlockSpec((TM, LANES), lambda i: (i, 0)),
                      pl.BlockSpec((TM, LANES), lambda i: (i, 0))]
        out_shape += [jax.ShapeDtypeStruct((n_rows, LANES), I32),
                      jax.ShapeDtypeStruct((n_rows, LANES), F32)]
        args.append(router_pad)
    return pl.pallas_call(
        functools.partial(_outproj_kernel, moe=moe, n_lat=n_lat, has_ctx=has_ctx),
        grid=(n_tiles,),
        in_specs=in_specs,
        out_specs=out_specs,
        out_shape=out_shape,
        compiler_params=_cparams(1),
        name="outproj_moe" if moe else "outproj",
    )(*args)


def _swiglu_hidden(x, w1_ref, w3_ref, t_ref, d_ff):
    for c0 in range(0, d_ff, FF_CHUNK):
        c1 = min(c0 + FF_CHUNK, d_ff)
        g = jnp.dot(x, w1_ref[:, c0:c1], preferred_element_type=F32)
        u = jnp.dot(x, w3_ref[:, c0:c1], preferred_element_type=F32)
        t_ref[:, c0:c1] = (jax.nn.silu(g) * u).astype(BF16)


def _ffn_dense_kernel(a_ref, h_ref, mod_ref, w1_ref, w3_ref, w2_ref, lng_ref, lnb_ref, o_ref,
                      t_ref):
    _swiglu_hidden(a_ref[...], w1_ref, w3_ref, t_ref, D_FF_DENSE)
    f = jnp.dot(t_ref[...], w2_ref[...], preferred_element_type=F32)
    o_ref[...] = _post_norm(h_ref[...], f, mod_ref[5:6, :], lng_ref[1:2, :], lnb_ref[1:2, :])


def _ffn_dense(a2, h1, mods, w1, w3, w2, ln_g, ln_b, layer, n_rows):
    n_tiles = n_rows // TM
    tiles_per_batch = SEQ // TM
    n_lat = NL // TM
    idx = layer // 2
    resident = pl.Buffered(1)
    return pl.pallas_call(
        _ffn_dense_kernel,
        grid=(n_tiles,),
        in_specs=[
            pl.BlockSpec((TM, D), lambda i: (i, 0)),
            pl.BlockSpec((TM, D), lambda i: (i, 0)),
            pl.BlockSpec((None, None, 6, D),
                         lambda i: (layer, _mod_row(i, tiles_per_batch, n_lat), 0, 0)),
            pl.BlockSpec((None, D, D_FF_DENSE), lambda i: (idx, 0, 0), pipeline_mode=resident),
            pl.BlockSpec((None, D, D_FF_DENSE), lambda i: (idx, 0, 0), pipeline_mode=resident),
            pl.BlockSpec((None, D_FF_DENSE, D), lambda i: (idx, 0, 0), pipeline_mode=resident),
            pl.BlockSpec((None, 2, D), lambda i: (layer, 0, 0)),
            pl.BlockSpec((None, 2, D), lambda i: (layer, 0, 0)),
        ],
        out_specs=pl.BlockSpec((TM, D), lambda i: (i, 0)),
        out_shape=jax.ShapeDtypeStruct((n_rows, D), F32),
        scratch_shapes=[pltpu.VMEM((TM, D_FF_DENSE), BF16)],
        compiler_params=_cparams(1),
        name="ffn_dense",
    )(a2, h1, mods, w1, w3, w2, ln_g, ln_b)


def _expert_first_tile(e, idx, x, issue_rows, w1_hbm, w3_hbm, w2_hbm, w1_ref, w3_ref, w2_ref,
                       st_col, st_row, wsem, t_ref, o_ref):
    n_c = D_FF_EXPERT // W_CHUNK
    jobs = []
    for c in range(n_c):
        jobs += [(True, w1_hbm, w1_ref, c), (True, w3_hbm, w3_ref, c)]
    jobs += [(False, w2_hbm, w2_ref, c) for c in range(n_c)]
    n_col = 2 * n_c

    def ring_slot(j):
        return j % W_RING if j < n_col else (j - n_col) % W_RING

    def copy(j):
        by_col, src, _, c = jobs[j]
        s = ring_slot(j)
        if by_col:
            return pltpu.make_async_copy(src.at[idx, e, :, pl.ds(c * W_CHUNK, W_CHUNK)],
                                         st_col.at[s], wsem.at[s])
        return pltpu.make_async_copy(src.at[idx, e, pl.ds(c * W_CHUNK, W_CHUNK), :],
                                     st_row.at[s], wsem.at[W_RING + s])

    def consume(j):
        if j + W_RING - 1 < len(jobs):
            copy(j + W_RING - 1).start()
        copy(j).wait()
        by_col, _, dst, c = jobs[j]
        if by_col:
            dst[:, c * W_CHUNK:(c + 1) * W_CHUNK] = st_col[ring_slot(j)].astype(BF16)
        else:
            dst[c * W_CHUNK:(c + 1) * W_CHUNK, :] = st_row[ring_slot(j)].astype(BF16)

    for j in range(W_RING - 1):
        copy(j).start()
    for c in range(n_c):
        issue_rows(c * MOE_TM // n_c, (c + 1) * MOE_TM // n_c)
        consume(2 * c)
        consume(2 * c + 1)
        cs = slice(c * W_CHUNK, (c + 1) * W_CHUNK)
        g = jnp.dot(x, w1_ref[:, cs], preferred_element_type=F32)
        u = jnp.dot(x, w3_ref[:, cs], preferred_element_type=F32)
        t_ref[:, cs] = (jax.nn.silu(g) * u).astype(BF16)
    y = None
    for c0 in range(0, n_c, 2):
        consume(n_col + c0)
        consume(n_col + c0 + 1)
        ks = slice(c0 * W_CHUNK, (c0 + 2) * W_CHUNK)
        part = jnp.dot(t_ref[:, ks], w2_ref[ks, :], preferred_element_type=F32)
        y = part if y is None else y + part
    _store_row_tiles(o_ref, y)


def _experts_kernel(pos_ref, te_ref, tv_ref, cnt_ref, off_ref, end_ref,
                    a_hbm, w1_hbm, w3_hbm, w2_hbm, o_ref, src_ref, xg_ref, t_ref,
                    w1_ref, w3_ref, w2_ref, st_col, st_row, sem, wsem,
                    *, n_assign, n_slots, idx):
    i = pl.program_id(0)
    n = pl.num_programs(0)

    def start_row(tile, slot, r):
        s = pl.multiple_of(src_ref[tile * MOE_TM + r] * ROW_TILE, ROW_TILE)
        d = r * ROW_TILE if isinstance(r, int) else pl.multiple_of(r * ROW_TILE, ROW_TILE)
        pltpu.make_async_copy(a_hbm.at[pl.ds(s, ROW_TILE)],
                              xg_ref.at[slot, pl.ds(d, ROW_TILE)], sem.at[slot]).start()

    def wait_tile(slot):
        pltpu.make_async_copy(a_hbm.at[pl.ds(0, MOE_TM * ROW_TILE)], xg_ref.at[slot],
                              sem.at[slot]).wait()

    def clear(s, carry):
        src_ref[s] = 0
        return carry

    @pl.when(i == 0)
    def _():
        for e in range(N_EXPERTS):
            lax.fori_loop(off_ref[e] + cnt_ref[e], end_ref[e], clear, 0)
        lax.fori_loop(end_ref[N_EXPERTS - 1], n_slots, clear, 0)

        def fill(a, carry):
            src_ref[pos_ref[a]] = lax.shift_right_logical(a, 1)
            return carry

        lax.fori_loop(0, n_assign, fill, 0, unroll=8)

        def first(r, carry):
            start_row(0, 0, r)
            return carry

        lax.fori_loop(0, MOE_TM, first, 0, unroll=8)

    valid = tv_ref[i] > 0
    slot = i % 2
    prev = jnp.maximum(i - 1, 0)

    @pl.when(jnp.logical_or(i == 0, tv_ref[prev] > 0))
    def _():
        wait_tile(slot)

    nxt_tile = lax.rem(i + 1, n)

    def issue_rows(lo, hi):
        for r in range(lo, hi):
            start_row(nxt_tile, 1 - slot, r)

    group_start = jnp.logical_or(i == 0, te_ref[i] != te_ref[prev])

    @pl.when(jnp.logical_and(valid, group_start))
    def _():
        x = _load_row_tiles(xg_ref.at[slot], MOE_TM).astype(BF16)
        _expert_first_tile(te_ref[i], idx, x, issue_rows, w1_hbm, w3_hbm, w2_hbm,
                           w1_ref, w3_ref, w2_ref, st_col, st_row, wsem, t_ref, o_ref)

    @pl.when(jnp.logical_and(valid, jnp.logical_not(group_start)))
    def _():
        x = _load_row_tiles(xg_ref.at[slot], MOE_TM).astype(BF16)
        chunks = [(c0, min(c0 + FF_CHUNK, D_FF_EXPERT)) for c0 in range(0, D_FF_EXPERT, FF_CHUNK)]
        per_chunk = MOE_TM // len(chunks)
        for ci, (c0, c1) in enumerate(chunks):
            issue_rows(ci * per_chunk, (ci + 1) * per_chunk)
            g = jnp.dot(x, w1_ref[:, c0:c1], preferred_element_type=F32)
            u = jnp.dot(x, w3_ref[:, c0:c1], preferred_element_type=F32)
            t_ref[:, c0:c1] = (jax.nn.silu(g) * u).astype(BF16)
        y = jnp.dot(t_ref[...], w2_ref[...], preferred_element_type=F32)
        _store_row_tiles(o_ref, y)

    @pl.when(jnp.logical_not(valid))
    def _():
        o_ref[...] = jnp.zeros_like(o_ref)

    @pl.when(jnp.logical_and(valid, i == n - 1))
    def _():
        wait_tile(1 - slot)


def _experts(plan, a2r, w1, w3, w2, idx, n_rows, n_slots):
    n_tiles = n_slots // MOE_TM
    return pl.pallas_call(
        functools.partial(_experts_kernel, n_assign=2 * n_rows, n_slots=n_slots, idx=idx),
        grid_spec=pltpu.PrefetchScalarGridSpec(
            num_scalar_prefetch=6,
            grid=(n_tiles,),
            in_specs=[pl.BlockSpec(memory_space=pl.ANY)] * 4,
            out_specs=pl.BlockSpec((MOE_TM * ROW_TILE, LANES), lambda i, *_: (i, 0)),
            scratch_shapes=[
                pltpu.SMEM((n_slots,), I32),
                pltpu.VMEM((2, MOE_TM * ROW_TILE, LANES), F32),
                pltpu.VMEM((MOE_TM, D_FF_EXPERT), BF16),
                pltpu.VMEM((D, D_FF_EXPERT), BF16),
                pltpu.VMEM((D, D_FF_EXPERT), BF16),
                pltpu.VMEM((D_FF_EXPERT, D), BF16),
                pltpu.VMEM((W_RING, D, W_CHUNK), F32),
                pltpu.VMEM((W_RING, W_CHUNK, D), F32),
                pltpu.SemaphoreType.DMA((2,)),
                pltpu.SemaphoreType.DMA((2 * W_RING,)),
            ],
        ),
        out_shape=jax.ShapeDtypeStruct((n_slots * ROW_TILE, LANES), F32),
        compiler_params=_cparams(1),
        name="moe_experts",
    )(*plan, a2r, w1, w3, w2)


def _combine_kernel(pos_ref, y_hbm, h_ref, rw_ref, mod_ref, lng_ref, lnb_ref, o_ref, buf, sem):
    i = pl.program_id(0)
    n = pl.num_programs(0)

    def start_row(step, slot, r):
        d = r * ROW_TILE if isinstance(r, int) else pl.multiple_of(r * ROW_TILE, ROW_TILE)
        for k in range(2):
            p = pl.multiple_of(pos_ref[2 * (step * COMBINE_ROWS + r) + k] * ROW_TILE, ROW_TILE)
            pltpu.make_async_copy(y_hbm.at[pl.ds(p, ROW_TILE)],
                                  buf.at[slot, k, pl.ds(d, ROW_TILE)], sem.at[slot]).start()

    def wait_tile(slot):
        for k in range(2):
            pltpu.make_async_copy(y_hbm.at[pl.ds(0, COMBINE_ROWS * ROW_TILE)], buf.at[slot, k],
                                  sem.at[slot]).wait()

    ahead = COMBINE_SLOTS - 1

    @pl.when(i == 0)
    def _():
        for t in range(ahead):
            def first(r, carry):
                start_row(t, t, r)
                return carry

            lax.fori_loop(0, COMBINE_ROWS, first, 0, unroll=4)

    slot = lax.rem(i, COMBINE_SLOTS)
    wait_tile(slot)
    nxt = lax.rem(i + ahead, n)
    nxt_slot = lax.rem(i + ahead, COMBINE_SLOTS)
    chunk = COMBINE_ROWS // COMBINE_CHUNKS
    for c in range(COMBINE_CHUNKS):
        for r in range(c * chunk, (c + 1) * chunk):
            start_row(nxt, nxt_slot, r)
        rs = slice(c * chunk, (c + 1) * chunk)
        tiles = pl.ds(c * chunk * ROW_TILE, chunk * ROW_TILE)
        w = rw_ref[rs, :]
        f = (w[:, 0:1] * _load_row_tiles(buf.at[slot, 0, tiles], chunk)
             + w[:, 1:2] * _load_row_tiles(buf.at[slot, 1, tiles], chunk))
        o_ref[rs, :] = _post_norm(h_ref[rs, :], f, mod_ref[5:6, :], lng_ref[1:2, :],
                                  lnb_ref[1:2, :])

    @pl.when(i == n - 1)
    def _():
        for t in range(1, COMBINE_SLOTS):
            wait_tile(lax.rem(i + t, COMBINE_SLOTS))


def _combine(pos, y, h1, rw, mods, ln_g, ln_b, layer, n_rows):
    n_tiles = n_rows // COMBINE_ROWS
    tiles_per_batch = SEQ // COMBINE_ROWS
    n_lat = NL // COMBINE_ROWS
    return pl.pallas_call(
        _combine_kernel,
        grid_spec=pltpu.PrefetchScalarGridSpec(
            num_scalar_prefetch=1,
            grid=(n_tiles,),
            in_specs=[
                pl.BlockSpec(memory_space=pl.ANY),
                pl.BlockSpec((COMBINE_ROWS, D), lambda i, p: (i, 0)),
                pl.BlockSpec((COMBINE_ROWS, LANES), lambda i, p: (i, 0)),
                pl.BlockSpec((None, None, 6, D),
                             lambda i, p: (layer, _mod_row(i, tiles_per_batch, n_lat), 0, 0)),
                pl.BlockSpec((None, 2, D), lambda i, p: (layer, 0, 0)),
                pl.BlockSpec((None, 2, D), lambda i, p: (layer, 0, 0)),
            ],
            out_specs=pl.BlockSpec((COMBINE_ROWS, D), lambda i, p: (i, 0)),
            scratch_shapes=[pltpu.VMEM((COMBINE_SLOTS, 2, COMBINE_ROWS * ROW_TILE, LANES), F32),
                            pltpu.SemaphoreType.DMA((COMBINE_SLOTS,))],
        ),
        out_shape=jax.ShapeDtypeStruct((n_rows, D), F32),
        compiler_params=_cparams(1),
        name="moe_combine",
    )(pos, y, h1, rw, mods, ln_g, ln_b)


def _route_plan(eid, n_rows, n_slots):
    n_tiles = n_slots // MOE_TM
    e_flat = eid.reshape(-1)
    onehot = (e_flat[:, None] == jnp.arange(N_EXPERTS, dtype=I32)[None, :]).astype(I32)
    csum = jnp.cumsum(onehot, axis=0)
    rank = jnp.sum((csum - onehot) * onehot, axis=1)
    count = csum[-1]
    padded = ((count + MOE_TM - 1) // MOE_TM) * MOE_TM
    ends = jnp.cumsum(padded)
    offs = ends - padded
    pos = jnp.sum(onehot * offs[None, :], axis=1) + rank
    tile_start = jnp.arange(n_tiles, dtype=I32) * MOE_TM
    tile_e = jnp.minimum(jnp.sum((tile_start[:, None] >= ends[None, :]).astype(I32), axis=1),
                         N_EXPERTS - 1)
    tile_v = (tile_start < ends[-1]).astype(I32)
    return (pos.astype(I32), tile_e.astype(I32), tile_v, count.astype(I32), offs.astype(I32),
            ends.astype(I32))


def _moe(a2r, h1, ri, rw, mods, w1, w3, w2, ln_g, ln_b, layer, n_rows):
    n_slots = 2 * n_rows + N_EXPERTS * MOE_TM
    plan = _route_plan(ri[:, :2], n_rows, n_slots)
    ys = _experts(plan, a2r, w1, w3, w2, layer // 2, n_rows, n_slots)
    return _combine(plan[0], ys, h1, rw, mods, ln_g, ln_b, layer, n_rows)


def kernel(x, c, ctx, c_ctx, w_ada, b_ada, w_in, w_pool, pool_scale, sink, w_out, ln_g, ln_b,
           dense_w1, dense_w3, dense_w2, router, moe_w1, moe_w3, moe_w2):
    cond = jnp.concatenate([c, c_ctx[None, :], jnp.zeros((MOD_ROWS - B - 1, D), F32)], axis=0)
    mods = _ada_tables(cond, w_ada, b_ada)
    cos_t, sin_t = _rope_tables()

    w_in_b = w_in.astype(BF16)
    w_pool_b = w_pool.astype(BF16)
    w_out_b = w_out.astype(BF16)
    dw1, dw3, dw2 = dense_w1.astype(BF16), dense_w3.astype(BF16), dense_w2.astype(BF16)
    r_hi = router.astype(BF16)
    r_lo = (router - r_hi.astype(F32)).astype(BF16)
    lane_pad = ((0, 0), (0, 0), (0, LANES - N_EXPERTS))
    router_pad = jnp.concatenate([jnp.pad(r_hi, lane_pad), jnp.pad(r_lo, lane_pad)], axis=-1)
    pool_scale3 = pool_scale.reshape(DEPTH, 1, POOL_W)

    h = jnp.concatenate([x.reshape(NL, D), ctx.reshape(NC, D)], axis=0)
    for l in range(DEPTH):
        last = l == DEPTH - 1
        n_rows = NL if last else NT
        q, kk, vv, u = _inproj(h, mods, w_in_b, cos_t, sin_t, l)
        mix = _mixer_latent(q, kk, vv, u, sink[l], w_pool_b, pool_scale3, l)
        mix_c = mix if last else _mixer_context(q, kk, vv, u, sink[l], w_pool_b, pool_scale3, l)
        if l % 2 == 0:
            h1, a2 = _outproj(mix, mix_c, h, mods, w_out_b, ln_g, ln_b, None, l, n_rows)
            h = _ffn_dense(a2, h1, mods, dw1, dw3, dw2, ln_g, ln_b, l, n_rows)
        else:
            h1, a2, ri, rw = _outproj(mix, mix_c, h, mods, w_out_b, ln_g, ln_b, router_pad, l,
                                      n_rows)
            h = _moe(a2, h1, ri, rw, mods, moe_w1, moe_w3, moe_w2, ln_g, ln_b, l, n_rows)
    return h.reshape(B, SEQ, D)
```

```python
import functools

import jax
import jax.numpy as jnp
from jax import lax
from jax.experimental import pallas as pl
from jax.experimental.pallas import tpu as pltpu

F32 = jnp.float32
BF16 = jnp.bfloat16
I32 = jnp.int32

D = 1024
B = 8
SEQ = 2048
DEPTH = 4
CTX = 256
GRID_W = 64
HEAD_DIM = 64
N_Q_HEADS = 8
N_KV_HEADS = 2
ATTN_W = N_Q_HEADS * HEAD_DIM
KV_W = N_KV_HEADS * HEAD_DIM
POOL_WINDOWS = (2, 4, 8, 16)
POOL_W = D - ATTN_W
POOL_G = POOL_W // len(POOL_WINDOWS)
IN_W = ATTN_W + 2 * KV_W + POOL_W
WINDOW = 128
ROPE_BASE = 10000.0
ROPE_FREQS = HEAD_DIM // 4
D_FF_DENSE = 2816
N_EXPERTS = 8
D_FF_EXPERT = 3584
ALPHA = (2 * DEPTH) ** 0.25
LN_EPS = 1e-6
NEG_INF = -1e30
LOG2E = 1.4426950408889634

NL = B * SEQ
NC = B * CTX
NT = NL + NC
MOD_ROWS = 16
LANES = 128

TM = 512
TQ = 128
TQ_STEP = 256
ROW_SPLIT = 256
MOE_TM = 512
FF_CHUNK = 1024
W_CHUNK = 256
W_RING = 4
COMBINE_ROWS = 256
COMBINE_CHUNKS = 4
COMBINE_SLOTS = 3

VMEM_LIMIT = 56 * 1024 * 1024


def _cparams(n_axes, vmem=VMEM_LIMIT):
    return pltpu.CompilerParams(
        dimension_semantics=("arbitrary",) * n_axes, vmem_limit_bytes=vmem)


def _layernorm(x):
    mu = jnp.mean(x, axis=-1, keepdims=True)
    xc = x - mu
    var = jnp.mean(xc * xc, axis=-1, keepdims=True)
    return xc * lax.rsqrt(var + LN_EPS)


def _mod_row(tile, tiles_per_batch, n_latent_tiles):
    return jnp.where(tile < n_latent_tiles, tile // tiles_per_batch, B)


ADA_TN = 1536


def _ada_kernel(s_ref, w_ref, b_ref, o_ref):
    s = jax.nn.silu(s_ref[...]).astype(BF16)
    w = w_ref[...].astype(BF16)
    o_ref[...] = jnp.dot(s, w, preferred_element_type=F32) + b_ref[...]


def _ada_tables(cond, w_ada, b_ada):
    n_col = (6 * D) // ADA_TN
    out = pl.pallas_call(
        _ada_kernel,
        grid=(DEPTH, n_col),
        in_specs=[
            pl.BlockSpec((MOD_ROWS, D), lambda l, j: (0, 0)),
            pl.BlockSpec((None, D, ADA_TN), lambda l, j: (l, 0, j)),
            pl.BlockSpec((None, 1, ADA_TN), lambda l, j: (l, 0, j)),
        ],
        out_specs=pl.BlockSpec((None, MOD_ROWS, ADA_TN), lambda l, j: (l, 0, j)),
        out_shape=jax.ShapeDtypeStruct((DEPTH, MOD_ROWS, 6 * D), F32),
        compiler_params=_cparams(2),
        name="ada_tables",
    )(cond, w_ada, b_ada.reshape(DEPTH, 1, 6 * D))
    return out.reshape(DEPTH, MOD_ROWS, 6, D)


def _inproj_kernel(h_ref, mod_ref, w_ref, cos_ref, sin_ref,
                   q_ref, kk_ref, vv_ref, u_ref):
    lane = lax.broadcasted_iota(I32, (1, LANES), 1)
    first_half = (lane % 32) < 16
    low = lane < HEAD_DIM
    for r0 in range(0, TM, ROW_SPLIT):
        rs = slice(r0, r0 + ROW_SPLIT)
        a = _layernorm(h_ref[rs, :]) * (1.0 + mod_ref[1:2, :]) + mod_ref[0:1, :]
        p = jnp.dot(a.astype(BF16), w_ref[...], preferred_element_type=F32)
        cos = cos_ref[rs, :]
        sin = sin_ref[rs, :]

        def rope(x):
            swapped = jnp.where(first_half, pltpu.roll(x, LANES - 16, 1), pltpu.roll(x, 16, 1))
            return x * cos + swapped * sin

        for c in range(ATTN_W // LANES):
            qc = rope(p[:, c * LANES:(c + 1) * LANES]) * (HEAD_DIM ** -0.5 * LOG2E)
            q_ref[rs, c * LANES:(c + 1) * LANES] = qc.astype(BF16)
        k = rope(p[:, ATTN_W:ATTN_W + KV_W])
        k_sw = pltpu.roll(k, HEAD_DIM, 1)
        kk_ref[rs, 0:LANES] = jnp.where(low, k, k_sw).astype(BF16)
        kk_ref[rs, LANES:2 * LANES] = jnp.where(low, k_sw, k).astype(BF16)
        v = p[:, ATTN_W + KV_W:ATTN_W + 2 * KV_W]
        v_sw = pltpu.roll(v, HEAD_DIM, 1)
        vv_ref[rs, 0:LANES] = jnp.where(low, v, v_sw).astype(BF16)
        vv_ref[rs, LANES:2 * LANES] = jnp.where(low, v_sw, v).astype(BF16)
        u_ref[rs, :] = p[:, ATTN_W + 2 * KV_W:]


def _inproj(h, mods, w_in, cos_t, sin_t, layer):
    n_tiles = NT // TM
    tiles_per_batch = SEQ // TM
    n_lat = NL // TM

    def rope_idx(i):
        return (jnp.where(i < n_lat, i % tiles_per_batch, tiles_per_batch), 0)

    return pl.pallas_call(
        _inproj_kernel,
        grid=(n_tiles,),
        in_specs=[
            pl.BlockSpec((TM, D), lambda i: (i, 0)),
            pl.BlockSpec((None, None, 6, D),
                         lambda i: (layer, _mod_row(i, tiles_per_batch, n_lat), 0, 0)),
            pl.BlockSpec((None, D, IN_W), lambda i: (layer, 0, 0)),
            pl.BlockSpec((TM, LANES), rope_idx),
            pl.BlockSpec((TM, LANES), rope_idx),
        ],
        out_specs=[
            pl.BlockSpec((TM, ATTN_W), lambda i: (i, 0)),
            pl.BlockSpec((TM, 2 * KV_W), lambda i: (i, 0)),
            pl.BlockSpec((TM, 2 * KV_W), lambda i: (i, 0)),
            pl.BlockSpec((TM, POOL_W), lambda i: (i, 0)),
        ],
        out_shape=[
            jax.ShapeDtypeStruct((NT, ATTN_W), BF16),
            jax.ShapeDtypeStruct((NT, 2 * KV_W), BF16),
            jax.ShapeDtypeStruct((NT, 2 * KV_W), BF16),
            jax.ShapeDtypeStruct((NT, POOL_W), F32),
        ],
        compiler_params=_cparams(1),
        name="inproj",
    )(h, mods, w_in, cos_t, sin_t)


def _rope_tables():
    rows = SEQ // GRID_W
    row = jnp.repeat(jnp.arange(rows, dtype=F32), GRID_W)
    col = jnp.tile(jnp.arange(GRID_W, dtype=F32), rows)
    inv = ROPE_BASE ** (-jnp.arange(ROPE_FREQS, dtype=F32) / ROPE_FREQS)
    ang_r = row[:, None] * inv[None, :]
    ang_c = col[:, None] * inv[None, :]
    cr, sr, cc, sc = jnp.cos(ang_r), jnp.sin(ang_r), jnp.cos(ang_c), jnp.sin(ang_c)
    cos_h = jnp.concatenate([cr, cr, cc, cc], axis=-1)
    sin_h = jnp.concatenate([-sr, sr, -sc, sc], axis=-1)
    cos_t = jnp.concatenate([jnp.tile(cos_h, (1, 2)), jnp.ones((TM, LANES), F32)], axis=0)
    sin_t = jnp.concatenate([jnp.tile(sin_h, (1, 2)), jnp.zeros((TM, LANES), F32)], axis=0)
    return cos_t, sin_t


def _pool_group(top, mid, bot, t0, seq_len, width):
    rows = mid.shape[0]
    slab = jnp.concatenate([top, mid, bot], axis=0)
    half = width // 2
    s = slab
    span = 1
    while span < width:
        n = s.shape[0] - span
        s = s[0:n] + s[span:span + n]
        span *= 2
    start = 8 - half
    total = s[start:start + rows]
    pos = t0 + lax.broadcasted_iota(I32, (rows, 1), 0)
    hi = jnp.minimum(pos + half, seq_len)
    lo = jnp.maximum(pos - half, 0)
    cnt = (hi - lo).astype(F32)
    return total / cnt - mid


def _mixer_block(sink_ref, q_ref, kk_ref, vv_ref, kkc_ref, vvc_ref, u_ref, wp_ref, ps_ref,
                 o_ref, *, t0, r0, tq, seq_len, band):
    lane = lax.broadcasted_iota(I32, (1, LANES), 1)
    low = lane < HEAD_DIM

    if band:
        n_band = tq + 2 * WINDOW
        ks = jnp.clip(t0 - WINDOW, 0, seq_len - n_band)
        ks = pl.multiple_of(ks, LANES)
        kk = jnp.concatenate([kk_ref[pl.ds(ks, n_band), :], kkc_ref[...]], axis=0)
        vv = jnp.concatenate([vv_ref[pl.ds(ks, n_band), :], vvc_ref[...]], axis=0)
        qpos = t0 + lax.broadcasted_iota(I32, (tq, n_band), 0)
        kpos = ks + lax.broadcasted_iota(I32, (tq, n_band), 1)
        bias = jnp.where(jnp.abs(kpos - qpos) <= WINDOW, 0.0, NEG_INF)
    else:
        n_band = 0
        kk = kkc_ref[...]
        vv = vvc_ref[...]
        bias = None
    nk = kk.shape[0]
    zero = jnp.zeros_like(kk[:, 0:LANES])
    low_f = jnp.broadcast_to(jnp.where(low, 1.0, 0.0), (nk, LANES))
    ones_st = jnp.concatenate([low_f, 1.0 - low_f], axis=0).astype(BF16)

    for g in range(N_KV_HEADS):
        kg = kk[:, g * LANES:(g + 1) * LANES]
        vg = vv[:, g * LANES:(g + 1) * LANES]
        k_st = jnp.concatenate([jnp.where(low, kg, zero), jnp.where(low, zero, kg)], axis=0)
        v_st = jnp.concatenate([jnp.where(low, vg, zero), jnp.where(low, zero, vg)], axis=0)
        v_st = jnp.concatenate([v_st, ones_st], axis=1)
        for pr in range(2):
            j = 2 * g + pr
            qp = q_ref[r0:r0 + tq, j * LANES:(j + 1) * LANES]
            s = lax.dot_general(qp, k_st, (((1,), (1,)), ((), ())), preferred_element_type=F32)
            ps = []
            sink_terms = []
            for hh in range(2):
                sink = sink_ref[2 * j + hh] * LOG2E
                sh = s[:, hh * nk:(hh + 1) * nk]
                if bias is not None:
                    parts = [sh[:, :n_band] + bias, sh[:, n_band:]]
                else:
                    parts = [sh]
                m = sink
                for part in parts:
                    m = jnp.maximum(jnp.max(part, axis=-1, keepdims=True), m)
                sink_terms.append(jnp.exp2(sink - m))
                ps += [jnp.exp2(part - m).astype(BF16) for part in parts]
            od = jnp.dot(jnp.concatenate(ps, axis=1), v_st, preferred_element_type=F32)
            den = od[:, LANES:] + jnp.where(low, sink_terms[0], sink_terms[1])
            o_ref[r0:r0 + tq, j * LANES:(j + 1) * LANES] = (od[:, :LANES] / den).astype(BF16)

    zeros8 = jnp.zeros((8, POOL_W), F32)
    if band:
        t0a = pl.multiple_of(t0, 8)
        mid = u_ref[pl.ds(t0a, tq), :]
        top_s = pl.multiple_of(jnp.maximum(t0 - 8, 0), 8)
        bot_s = pl.multiple_of(jnp.minimum(t0 + tq, seq_len - 8), 8)
        top = jnp.where(t0 > 0, u_ref[pl.ds(top_s, 8), :], zeros8)
        bot = jnp.where(t0 + tq < seq_len, u_ref[pl.ds(bot_s, 8), :], zeros8)
    else:
        mid = u_ref[...]
        top = zeros8
        bot = zeros8
    for g, width in enumerate(POOL_WINDOWS):
        sl = slice(g * POOL_G, (g + 1) * POOL_G)
        diff = _pool_group(top[:, sl], mid[:, sl], bot[:, sl], t0, seq_len, width)
        mixed = jnp.dot(diff.astype(BF16), wp_ref[g], preferred_element_type=F32)
        mixed = mixed * ps_ref[:, sl]
        o_ref[r0:r0 + tq, ATTN_W + g * POOL_G:ATTN_W + (g + 1) * POOL_G] = mixed.astype(BF16)


def _post_norm(h, upd, gate, g, b):
    return _layernorm(ALPHA * h + gate * upd) * g + b


ROW_TILE = D // LANES


def _store_row_tiles(ref, x):
    rows = x.shape[0]
    for c in range(ROW_TILE):
        ref[pl.ds(c, rows, stride=ROW_TILE), :] = x[:, c * LANES:(c + 1) * LANES]


def _load_row_tiles(ref, rows):
    return jnp.concatenate(
        [ref[pl.ds(c, rows, stride=ROW_TILE), :] for c in range(ROW_TILE)], axis=1)


def _route_top2(a, router_ref, ri_ref, rw_ref):
    a_hi = a.astype(BF16)
    a_lo = (a - a_hi.astype(F32)).astype(BF16)
    t_hi = jnp.dot(a_hi, router_ref[...], preferred_element_type=F32)
    t_lo = jnp.dot(a_lo, router_ref[:, :LANES], preferred_element_type=F32)
    logits = t_hi[:, :LANES] + t_hi[:, LANES:] + t_lo
    lane = lax.broadcasted_iota(I32, logits.shape, 1)
    lane_f = lane.astype(F32)
    logits = jnp.where(lane < N_EXPERTS, logits, -jnp.inf)
    m1 = jnp.max(logits, axis=-1, keepdims=True)
    i1 = jnp.min(jnp.where(logits == m1, lane_f, float(LANES)), axis=-1, keepdims=True)
    rest = jnp.where(lane_f == i1, -jnp.inf, logits)
    m2 = jnp.max(rest, axis=-1, keepdims=True)
    i2 = jnp.min(jnp.where(rest == m2, lane_f, float(LANES)), axis=-1, keepdims=True)
    e = jnp.exp(m2 - m1)
    w1 = 1.0 / (1.0 + e)
    w2 = e / (1.0 + e)
    ri_ref[...] = jnp.where(lane == 0, i1, jnp.where(lane == 1, i2, 0.0)).astype(I32)
    rw_ref[...] = jnp.where(lane == 0, w1, jnp.where(lane == 1, w2, 0.0))


def _mixer_out_kernel(sink_ref, q_ref, kk_ref, vv_ref, kkc_ref, vvc_ref, ul_ref, uc_ref, wp_ref,
                      ps_ref, h_ref, mod_ref, w_ref, lng_ref, lnb_ref, *rest,
                      moe, n_lat_steps, has_ctx):
    if moe:
        router_ref, h1_ref, a2_ref, ri_ref, rw_ref, mix_ref = rest
    else:
        h1_ref, a2_ref, mix_ref = rest
    step = pl.program_id(0)

    def project(r0, rows):
        rs = slice(r0, r0 + rows)
        y = jnp.dot(mix_ref[rs, :], w_ref[...], preferred_element_type=F32)
        h1 = _post_norm(h_ref[rs, :], y, mod_ref[2:3, :], lng_ref[0:1, :], lnb_ref[0:1, :])
        h1_ref[rs, :] = h1
        a2 = _layernorm(h1) * (1.0 + mod_ref[4:5, :]) + mod_ref[3:4, :]
        if moe:
            _store_row_tiles(a2_ref.at[pl.ds(r0 * ROW_TILE, rows * ROW_TILE)], a2)
            _route_top2(a2, router_ref, ri_ref.at[pl.ds(r0, rows)], rw_ref.at[pl.ds(r0, rows)])
        else:
            a2_ref[rs, :] = a2.astype(BF16)

    def latent():
        t_base = (step % (SEQ // TQ_STEP)) * TQ_STEP
        for sb in range(TQ_STEP // TQ):
            _mixer_block(sink_ref, q_ref, kk_ref, vv_ref, kkc_ref, vvc_ref, ul_ref, wp_ref, ps_ref,
                         mix_ref, t0=t_base + sb * TQ, r0=sb * TQ, tq=TQ, seq_len=SEQ, band=True)
        project(0, TQ_STEP)

    def context():
        _mixer_block(sink_ref, q_ref, None, None, kkc_ref, vvc_ref, uc_ref, wp_ref, ps_ref,
                     mix_ref, t0=0, r0=0, tq=CTX, seq_len=CTX, band=False)
        project(0, CTX)

    if has_ctx:
        pl.when(step < n_lat_steps)(latent)
        pl.when(step >= n_lat_steps)(context)
    else:
        latent()


def _mixer_out(q, kk, vv, u, h, mods, sink_l, w_pool, pool_scale, w_out, ln_g, ln_b, router_pad,
               layer, n_rows):
    assert TQ_STEP == CTX
    n_steps = n_rows // TQ_STEP
    n_lat_steps = NL // TQ_STEP
    per_batch = SEQ // TQ_STEP
    ctx_blk = NL // CTX
    moe = router_pad is not None
    has_ctx = n_rows > NL

    def b_lat(s):
        return jnp.minimum(s // per_batch, B - 1)

    def b_any(s):
        return jnp.where(s < n_lat_steps, s // per_batch, s - n_lat_steps)

    def b_ctx(s):
        return jnp.maximum(s - n_lat_steps, 0)

    in_specs = [
        pl.BlockSpec(memory_space=pltpu.SMEM),
        pl.BlockSpec((TQ_STEP, ATTN_W), lambda s: (s, 0)),
        pl.BlockSpec((SEQ, 2 * KV_W), lambda s: (b_lat(s), 0)),
        pl.BlockSpec((SEQ, 2 * KV_W), lambda s: (b_lat(s), 0)),
        pl.BlockSpec((CTX, 2 * KV_W), lambda s: (ctx_blk + b_any(s), 0)),
        pl.BlockSpec((CTX, 2 * KV_W), lambda s: (ctx_blk + b_any(s), 0)),
        pl.BlockSpec((SEQ, POOL_W), lambda s: (b_lat(s), 0)),
        pl.BlockSpec((CTX, POOL_W), lambda s: (ctx_blk + b_ctx(s), 0)),
        pl.BlockSpec((None, len(POOL_WINDOWS), POOL_G, POOL_G), lambda s: (layer, 0, 0, 0)),
        pl.BlockSpec((None, 1, POOL_W), lambda s: (layer, 0, 0)),
        pl.BlockSpec((TQ_STEP, D), lambda s: (s, 0)),
        pl.BlockSpec((None, None, 6, D),
                     lambda s: (layer, jnp.where(s < n_lat_steps, s // per_batch, B), 0, 0)),
        pl.BlockSpec((None, D, D), lambda s: (layer, 0, 0)),
        pl.BlockSpec((None, 2, D), lambda s: (layer, 0, 0)),
        pl.BlockSpec((None, 2, D), lambda s: (layer, 0, 0)),
    ]
    out_specs = [pl.BlockSpec((TQ_STEP, D), lambda s: (s, 0)),
                 pl.BlockSpec((TQ_STEP, D), lambda s: (s, 0))]
    out_shape = [jax.ShapeDtypeStruct((n_rows, D), F32), jax.ShapeDtypeStruct((n_rows, D), BF16)]
    args = [sink_l, q, kk, vv, kk, vv, u, u, w_pool, pool_scale, h, mods, w_out, ln_g, ln_b]
    if moe:
        in_specs.append(pl.BlockSpec((None, D, 2 * LANES), lambda s: (layer // 2, 0, 0)))
        out_specs[1] = pl.BlockSpec((TQ_STEP * ROW_TILE, LANES), lambda s: (s, 0))
        out_shape[1] = jax.ShapeDtypeStruct((n_rows * ROW_TILE, LANES), F32)
        out_specs += [pl.BlockSpec((TQ_STEP, LANES), lambda s: (s, 0)),
                      pl.BlockSpec((TQ_STEP, LANES), lambda s: (s, 0))]
        out_shape += [jax.ShapeDtypeStruct((n_rows, LANES), I32),
                      jax.ShapeDtypeStruct((n_rows, LANES), F32)]
        args.append(router_pad)
    return pl.pallas_call(
        functools.partial(_mixer_out_kernel, moe=moe, n_lat_steps=n_lat_steps, has_ctx=has_ctx),
        grid=(n_steps,),
        in_specs=in_specs,
        out_specs=out_specs,
        out_shape=out_shape,
        scratch_shapes=[pltpu.VMEM((TQ_STEP, D), BF16)],
        compiler_params=_cparams(1),
        name="mixer_out_moe" if moe else "mixer_out",
    )(*args)


def _swiglu_hidden(x, w1_ref, w3_ref, t_ref, d_ff):
    for c0 in range(0, d_ff, FF_CHUNK):
        c1 = min(c0 + FF_CHUNK, d_ff)
        g = jnp.dot(x, w1_ref[:, c0:c1], preferred_element_type=F32)
        u = jnp.dot(x, w3_ref[:, c0:c1], preferred_element_type=F32)
        t_ref[:, c0:c1] = (jax.nn.silu(g) * u).astype(BF16)


def _ffn_dense_kernel(a_ref, h_ref, mod_ref, w1_ref, w3_ref, w2_ref, lng_ref, lnb_ref, o_ref,
                      t_ref):
    _swiglu_hidden(a_ref[...], w1_ref, w3_ref, t_ref, D_FF_DENSE)
    f = jnp.dot(t_ref[...], w2_ref[...], preferred_element_type=F32)
    o_ref[...] = _post_norm(h_ref[...], f, mod_ref[5:6, :], lng_ref[1:2, :], lnb_ref[1:2, :])


def _ffn_dense(a2, h1, mods, w1, w3, w2, ln_g, ln_b, layer, n_rows):
    n_tiles = n_rows // TM
    tiles_per_batch = SEQ // TM
    n_lat = NL // TM
    idx = layer // 2
    resident = pl.Buffered(1)
    return pl.pallas_call(
        _ffn_dense_kernel,
        grid=(n_tiles,),
        in_specs=[
            pl.BlockSpec((TM, D), lambda i: (i, 0)),
            pl.BlockSpec((TM, D), lambda i: (i, 0)),
            pl.BlockSpec((None, None, 6, D),
                         lambda i: (layer, _mod_row(i, tiles_per_batch, n_lat), 0, 0)),
            pl.BlockSpec((None, D, D_FF_DENSE), lambda i: (idx, 0, 0), pipeline_mode=resident),
            pl.BlockSpec((None, D, D_FF_DENSE), lambda i: (idx, 0, 0), pipeline_mode=resident),
            pl.BlockSpec((None, D_FF_DENSE, D), lambda i: (idx, 0, 0), pipeline_mode=resident),
            pl.BlockSpec((None, 2, D), lambda i: (layer, 0, 0)),
            pl.BlockSpec((None, 2, D), lambda i: (layer, 0, 0)),
        ],
        out_specs=pl.BlockSpec((TM, D), lambda i: (i, 0)),
        out_shape=jax.ShapeDtypeStruct((n_rows, D), F32),
        scratch_shapes=[pltpu.VMEM((TM, D_FF_DENSE), BF16)],
        compiler_params=_cparams(1),
        name="ffn_dense",
    )(a2, h1, mods, w1, w3, w2, ln_g, ln_b)


def _expert_first_tile(e, idx, x, issue_rows, w1_hbm, w3_hbm, w2_hbm, w1_ref, w3_ref, w2_ref,
                       st_col, st_row, wsem, t_ref, o_ref):
    n_c = D_FF_EXPERT // W_CHUNK
    jobs = []
    for c in range(n_c):
        jobs += [(True, w1_hbm, w1_ref, c), (True, w3_hbm, w3_ref, c)]
    jobs += [(False, w2_hbm, w2_ref, c) for c in range(n_c)]
    n_col = 2 * n_c

    def ring_slot(j):
        return j % W_RING if j < n_col else (j - n_col) % W_RING

    def copy(j):
        by_col, src, _, c = jobs[j]
        s = ring_slot(j)
        if by_col:
            return pltpu.make_async_copy(src.at[idx, e, :, pl.ds(c * W_CHUNK, W_CHUNK)],
                                         st_col.at[s], wsem.at[s])
        return pltpu.make_async_copy(src.at[idx, e, pl.ds(c * W_CHUNK, W_CHUNK), :],
                                     st_row.at[s], wsem.at[W_RING + s])

    def consume(j):
        if j + W_RING - 1 < len(jobs):
            copy(j + W_RING - 1).start()
        copy(j).wait()
        by_col, _, dst, c = jobs[j]
        if by_col:
            dst[:, c * W_CHUNK:(c + 1) * W_CHUNK] = st_col[ring_slot(j)].astype(BF16)
        else:
            dst[c * W_CHUNK:(c + 1) * W_CHUNK, :] = st_row[ring_slot(j)].astype(BF16)

    for j in range(W_RING - 1):
        copy(j).start()
    for c in range(n_c):
        issue_rows(c * MOE_TM // n_c, (c + 1) * MOE_TM // n_c)
        consume(2 * c)
        consume(2 * c + 1)
        cs = slice(c * W_CHUNK, (c + 1) * W_CHUNK)
        g = jnp.dot(x, w1_ref[:, cs], preferred_element_type=F32)
        u = jnp.dot(x, w3_ref[:, cs], preferred_element_type=F32)
        t_ref[:, cs] = (jax.nn.silu(g) * u).astype(BF16)
    y = None
    for c0 in range(0, n_c, 2):
        consume(n_col + c0)
        consume(n_col + c0 + 1)
        ks = slice(c0 * W_CHUNK, (c0 + 2) * W_CHUNK)
        part = jnp.dot(t_ref[:, ks], w2_ref[ks, :], preferred_element_type=F32)
        y = part if y is None else y + part
    _store_row_tiles(o_ref, y)


def _experts_kernel(pos_ref, te_ref, tv_ref, cnt_ref, off_ref, end_ref,
                    a_hbm, w1_hbm, w3_hbm, w2_hbm, o_ref, src_ref, xg_ref, t_ref,
                    w1_ref, w3_ref, w2_ref, st_col, st_row, sem, wsem,
                    *, n_assign, n_slots, idx):
    i = pl.program_id(0)
    n = pl.num_programs(0)

    def start_row(tile, slot, r):
        s = pl.multiple_of(src_ref[tile * MOE_TM + r] * ROW_TILE, ROW_TILE)
        d = r * ROW_TILE if isinstance(r, int) else pl.multiple_of(r * ROW_TILE, ROW_TILE)
        pltpu.make_async_copy(a_hbm.at[pl.ds(s, ROW_TILE)],
                              xg_ref.at[slot, pl.ds(d, ROW_TILE)], sem.at[slot]).start()

    def wait_tile(slot):
        pltpu.make_async_copy(a_hbm.at[pl.ds(0, MOE_TM * ROW_TILE)], xg_ref.at[slot],
                              sem.at[slot]).wait()

    def clear(s, carry):
        src_ref[s] = 0
        return carry

    @pl.when(i == 0)
    def _():
        for e in range(N_EXPERTS):
            lax.fori_loop(off_ref[e] + cnt_ref[e], end_ref[e], clear, 0)
        lax.fori_loop(end_ref[N_EXPERTS - 1], n_slots, clear, 0)

        def fill(a, carry):
            src_ref[pos_ref[a]] = lax.shift_right_logical(a, 1)
            return carry

        lax.fori_loop(0, n_assign, fill, 0, unroll=8)

        def first(r, carry):
            start_row(0, 0, r)
            return carry

        lax.fori_loop(0, MOE_TM, first, 0, unroll=8)

    valid = tv_ref[i] > 0
    slot = i % 2
    prev = jnp.maximum(i - 1, 0)

    @pl.when(jnp.logical_or(i == 0, tv_ref[prev] > 0))
    def _():
        wait_tile(slot)

    nxt_tile = lax.rem(i + 1, n)

    def issue_rows(lo, hi):
        for r in range(lo, hi):
            start_row(nxt_tile, 1 - slot, r)

    group_start = jnp.logical_or(i == 0, te_ref[i] != te_ref[prev])

    @pl.when(jnp.logical_and(valid, group_start))
    def _():
        x = _load_row_tiles(xg_ref.at[slot], MOE_TM).astype(BF16)
        _expert_first_tile(te_ref[i], idx, x, issue_rows, w1_hbm, w3_hbm, w2_hbm,
                           w1_ref, w3_ref, w2_ref, st_col, st_row, wsem, t_ref, o_ref)

    @pl.when(jnp.logical_and(valid, jnp.logical_not(group_start)))
    def _():
        x = _load_row_tiles(xg_ref.at[slot], MOE_TM).astype(BF16)
        chunks = [(c0, min(c0 + FF_CHUNK, D_FF_EXPERT)) for c0 in range(0, D_FF_EXPERT, FF_CHUNK)]
        per_chunk = MOE_TM // len(chunks)
        for ci, (c0, c1) in enumerate(chunks):
            issue_rows(ci * per_chunk, (ci + 1) * per_chunk)
            g = jnp.dot(x, w1_ref[:, c0:c1], preferred_element_type=F32)
            u = jnp.dot(x, w3_ref[:, c0:c1], preferred_element_type=F32)
            t_ref[:, c0:c1] = (jax.nn.silu(g) * u).astype(BF16)
        y = jnp.dot(t_ref[...], w2_ref[...], preferred_element_type=F32)
        _store_row_tiles(o_ref, y)

    @pl.when(jnp.logical_not(valid))
    def _():
        o_ref[...] = jnp.zeros_like(o_ref)

    @pl.when(jnp.logical_and(valid, i == n - 1))
    def _():
        wait_tile(1 - slot)


def _experts(plan, a2r, w1, w3, w2, idx, n_rows, n_slots):
    n_tiles = n_slots // MOE_TM
    return pl.pallas_call(
        functools.partial(_experts_kernel, n_assign=2 * n_rows, n_slots=n_slots, idx=idx),
        grid_spec=pltpu.PrefetchScalarGridSpec(
            num_scalar_prefetch=6,
            grid=(n_tiles,),
            in_specs=[pl.BlockSpec(memory_space=pl.ANY)] * 4,
            out_specs=pl.BlockSpec((MOE_TM * ROW_TILE, LANES), lambda i, *_: (i, 0)),
            scratch_shapes=[
                pltpu.SMEM((n_slots,), I32),
                pltpu.VMEM((2, MOE_TM * ROW_TILE, LANES), F32),
                pltpu.VMEM((MOE_TM, D_FF_EXPERT), BF16),
                pltpu.VMEM((D, D_FF_EXPERT), BF16),
                pltpu.VMEM((D, D_FF_EXPERT), BF16),
                pltpu.VMEM((D_FF_EXPERT, D), BF16),
                pltpu.VMEM((W_RING, D, W_CHUNK), F32),
                pltpu.VMEM((W_RING, W_CHUNK, D), F32),
                pltpu.SemaphoreType.DMA((2,)),
                pltpu.SemaphoreType.DMA((2 * W_RING,)),
            ],
        ),
        out_shape=jax.ShapeDtypeStruct((n_slots * ROW_TILE, LANES), F32),
        compiler_params=_cparams(1),
        name="moe_experts",
    )(*plan, a2r, w1, w3, w2)


def _combine_kernel(pos_ref, y_hbm, h_ref, rw_ref, mod_ref, lng_ref, lnb_ref, o_ref, buf, sem):
    i = pl.program_id(0)
    n = pl.num_programs(0)

    def start_row(step, slot, r):
        d = r * ROW_TILE if isinstance(r, int) else pl.multiple_of(r * ROW_TILE, ROW_TILE)
        for k in range(2):
            p = pl.multiple_of(pos_ref[2 * (step * COMBINE_ROWS + r) + k] * ROW_TILE, ROW_TILE)
            pltpu.make_async_copy(y_hbm.at[pl.ds(p, ROW_TILE)],
                                  buf.at[slot, k, pl.ds(d, ROW_TILE)], sem.at[slot]).start()

    def wait_tile(slot):
        for k in range(2):
            pltpu.make_async_copy(y_hbm.at[pl.ds(0, COMBINE_ROWS * ROW_TILE)], buf.at[slot, k],
                                  sem.at[slot]).wait()

    ahead = COMBINE_SLOTS - 1

    @pl.when(i == 0)
    def _():
        for t in range(ahead):
            def first(r, carry):
                start_row(t, t, r)
                return carry

            lax.fori_loop(0, COMBINE_ROWS, first, 0, unroll=4)

    slot = lax.rem(i, COMBINE_SLOTS)
    wait_tile(slot)
    nxt = lax.rem(i + ahead, n)
    nxt_slot = lax.rem(i + ahead, COMBINE_SLOTS)
    chunk = COMBINE_ROWS // COMBINE_CHUNKS
    for c in range(COMBINE_CHUNKS):
        for r in range(c * chunk, (c + 1) * chunk):
            start_row(nxt, nxt_slot, r)
        rs = slice(c * chunk, (c + 1) * chunk)
        tiles = pl.ds(c * chunk * ROW_TILE, chunk * ROW_TILE)
        w = rw_ref[rs, :]
        f = (w[:, 0:1] * _load_row_tiles(buf.at[slot, 0, tiles], chunk)
             + w[:, 1:2] * _load_row_tiles(buf.at[slot, 1, tiles], chunk))
        o_ref[rs, :] = _post_norm(h_ref[rs, :], f, mod_ref[5:6, :], lng_ref[1:2, :],
                                  lnb_ref[1:2, :])

    @pl.when(i == n - 1)
    def _():
        for t in range(1, COMBINE_SLOTS):
            wait_tile(lax.rem(i + t, COMBINE_SLOTS))


def _combine(pos, y, h1, rw, mods, ln_g, ln_b, layer, n_rows):
    n_tiles = n_rows // COMBINE_ROWS
    tiles_per_batch = SEQ // COMBINE_ROWS
    n_lat = NL // COMBINE_ROWS
    return pl.pallas_call(
        _combine_kernel,
        grid_spec=pltpu.PrefetchScalarGridSpec(
            num_scalar_prefetch=1,
            grid=(n_tiles,),
            in_specs=[
                pl.BlockSpec(memory_space=pl.ANY),
                pl.BlockSpec((COMBINE_ROWS, D), lambda i, p: (i, 0)),
                pl.BlockSpec((COMBINE_ROWS, LANES), lambda i, p: (i, 0)),
                pl.BlockSpec((None, None, 6, D),
                             lambda i, p: (layer, _mod_row(i, tiles_per_batch, n_lat), 0, 0)),
                pl.BlockSpec((None, 2, D), lambda i, p: (layer, 0, 0)),
                pl.BlockSpec((None, 2, D), lambda i, p: (layer, 0, 0)),
            ],
            out_specs=pl.BlockSpec((COMBINE_ROWS, D), lambda i, p: (i, 0)),
            scratch_shapes=[pltpu.VMEM((COMBINE_SLOTS, 2, COMBINE_ROWS * ROW_TILE, LANES), F32),
                            pltpu.SemaphoreType.DMA((COMBINE_SLOTS,))],
        ),
        out_shape=jax.ShapeDtypeStruct((n_rows, D), F32),
        compiler_params=_cparams(1),
        name="moe_combine",
    )(pos, y, h1, rw, mods, ln_g, ln_b)


def _route_plan(eid, n_rows, n_slots):
    n_tiles = n_slots // MOE_TM
    e_flat = eid.reshape(-1)
    onehot = (e_flat[:, None] == jnp.arange(N_EXPERTS, dtype=I32)[None, :]).astype(I32)
    csum = jnp.cumsum(onehot, axis=0)
    rank = jnp.sum((csum - onehot) * onehot, axis=1)
    count = csum[-1]
    padded = ((count + MOE_TM - 1) // MOE_TM) * MOE_TM
    ends = jnp.cumsum(padded)
    offs = ends - padded
    pos = jnp.sum(onehot * offs[None, :], axis=1) + rank
    tile_start = jnp.arange(n_tiles, dtype=I32) * MOE_TM
    tile_e = jnp.minimum(jnp.sum((tile_start[:, None] >= ends[None, :]).astype(I32), axis=1),
                         N_EXPERTS - 1)
    tile_v = (tile_start < ends[-1]).astype(I32)
    return (pos.astype(I32), tile_e.astype(I32), tile_v, count.astype(I32), offs.astype(I32),
            ends.astype(I32))


def _moe(a2r, h1, ri, rw, mods, w1, w3, w2, ln_g, ln_b, layer, n_rows):
    n_slots = 2 * n_rows + N_EXPERTS * MOE_TM
    plan = _route_plan(ri[:, :2], n_rows, n_slots)
    ys = _experts(plan, a2r, w1, w3, w2, layer // 2, n_rows, n_slots)
    return _combine(plan[0], ys, h1, rw, mods, ln_g, ln_b, layer, n_rows)


def kernel(x, c, ctx, c_ctx, w_ada, b_ada, w_in, w_pool, pool_scale, sink, w_out, ln_g, ln_b,
           dense_w1, dense_w3, dense_w2, router, moe_w1, moe_w3, moe_w2):
    cond = jnp.concatenate([c, c_ctx[None, :], jnp.zeros((MOD_ROWS - B - 1, D), F32)], axis=0)
    mods = _ada_tables(cond, w_ada, b_ada)
    cos_t, sin_t = _rope_tables()

    w_in_b = w_in.astype(BF16)
    w_pool_b = w_pool.astype(BF16)
    w_out_b = w_out.astype(BF16)
    dw1, dw3, dw2 = dense_w1.astype(BF16), dense_w3.astype(BF16), dense_w2.astype(BF16)
    r_hi = router.astype(BF16)
    r_lo = (router - r_hi.astype(F32)).astype(BF16)
    lane_pad = ((0, 0), (0, 0), (0, LANES - N_EXPERTS))
    router_pad = jnp.concatenate([jnp.pad(r_hi, lane_pad), jnp.pad(r_lo, lane_pad)], axis=-1)
    pool_scale3 = pool_scale.reshape(DEPTH, 1, POOL_W)

    h = jnp.concatenate([x.reshape(NL, D), ctx.reshape(NC, D)], axis=0)
    for l in range(DEPTH):
        last = l == DEPTH - 1
        n_rows = NL if last else NT
        q, kk, vv, u = _inproj(h, mods, w_in_b, cos_t, sin_t, l)
        if l % 2 == 0:
            h1, a2 = _mixer_out(q, kk, vv, u, h, mods, sink[l], w_pool_b, pool_scale3, w_out_b,
                                ln_g, ln_b, None, l, n_rows)
            h = _ffn_dense(a2, h1, mods, dw1, dw3, dw2, ln_g, ln_b, l, n_rows)
        else:
            h1, a2, ri, rw = _mixer_out(q, kk, vv, u, h, mods, sink[l], w_pool_b, pool_scale3,
                                        w_out_b, ln_g, ln_b, router_pad, l, n_rows)
            h = _moe(a2, h1, ri, rw, mods, moe_w1, moe_w3, moe_w2, ln_g, ln_b, l, n_rows)
    return h.reshape(B, SEQ, D)
```

```python
import functools

import jax
import jax.numpy as jnp
from jax import lax
from jax.experimental import pallas as pl
from jax.experimental.pallas import tpu as pltpu

F32 = jnp.float32
BF16 = jnp.bfloat16
I32 = jnp.int32

D = 1024
B = 8
SEQ = 2048
DEPTH = 4
CTX = 256
GRID_W = 64
HEAD_DIM = 64
N_Q_HEADS = 8
N_KV_HEADS = 2
ATTN_W = N_Q_HEADS * HEAD_DIM
KV_W = N_KV_HEADS * HEAD_DIM
POOL_WINDOWS = (2, 4, 8, 16)
POOL_W = D - ATTN_W
POOL_G = POOL_W // len(POOL_WINDOWS)
IN_W = ATTN_W + 2 * KV_W + POOL_W
WINDOW = 128
ROPE_BASE = 10000.0
ROPE_FREQS = HEAD_DIM // 4
D_FF_DENSE = 2816
N_EXPERTS = 8
D_FF_EXPERT = 3584
ALPHA = (2 * DEPTH) ** 0.25
LN_EPS = 1e-6
NEG_INF = -1e30
LOG2E = 1.4426950408889634

NL = B * SEQ
NC = B * CTX
NT = NL + NC
MOD_ROWS = 16
LANES = 128

TM = 512
TQ = 128
TQ_STEP = 512
ROW_SPLIT = 256
MOE_TM = 512
FF_CHUNK = 1024
W_CHUNK = 256
W_RING = 5
COMBINE_ROWS = 256
COMBINE_CHUNKS = 4
COMBINE_SLOTS = 3

VMEM_LIMIT = 56 * 1024 * 1024
VMEM_LIMIT_EXPERTS = 60 * 1024 * 1024


def _cparams(n_axes, vmem=VMEM_LIMIT):
    return pltpu.CompilerParams(
        dimension_semantics=("arbitrary",) * n_axes, vmem_limit_bytes=vmem)


def _layernorm(x):
    mu = jnp.mean(x, axis=-1, keepdims=True)
    xc = x - mu
    var = jnp.mean(xc * xc, axis=-1, keepdims=True)
    return xc * lax.rsqrt(var + LN_EPS)


def _mod_row(tile, tiles_per_batch, n_latent_tiles):
    return jnp.where(tile < n_latent_tiles, tile // tiles_per_batch, B)


def _stream_specs(h, n_lat):
    if len(h) == 1:
        return [pl.BlockSpec((TM, D), lambda i: (i, 0))]
    return [pl.BlockSpec((TM, D), lambda i: (jnp.minimum(i, n_lat - 1), 0)),
            pl.BlockSpec((TM, D), lambda i: (jnp.maximum(i - n_lat, 0), 0))]


ADA_TN = 1536


def _ada_kernel(s_ref, w_ref, b_ref, o_ref):
    s = jax.nn.silu(s_ref[...]).astype(BF16)
    w = w_ref[...].astype(BF16)
    o_ref[...] = jnp.dot(s, w, preferred_element_type=F32) + b_ref[...]


def _ada_tables(cond, w_ada, b_ada):
    n_col = (6 * D) // ADA_TN
    out = pl.pallas_call(
        _ada_kernel,
        grid=(DEPTH, n_col),
        in_specs=[
            pl.BlockSpec((MOD_ROWS, D), lambda l, j: (0, 0)),
            pl.BlockSpec((None, D, ADA_TN), lambda l, j: (l, 0, j)),
            pl.BlockSpec((None, 1, ADA_TN), lambda l, j: (l, 0, j)),
        ],
        out_specs=pl.BlockSpec((None, MOD_ROWS, ADA_TN), lambda l, j: (l, 0, j)),
        out_shape=jax.ShapeDtypeStruct((DEPTH, MOD_ROWS, 6 * D), F32),
        compiler_params=_cparams(2),
        name="ada_tables",
    )(cond, w_ada, b_ada.reshape(DEPTH, 1, 6 * D))
    return out.reshape(DEPTH, MOD_ROWS, 6, D)


def _inproj_kernel(*refs, n_lat, two_streams):
    if two_streams:
        hl_ref, hc_ref, mod_ref, w_ref, cos_ref, sin_ref, q_ref, kk_ref, vv_ref, u_ref = refs
    else:
        hl_ref, mod_ref, w_ref, cos_ref, sin_ref, q_ref, kk_ref, vv_ref, u_ref = refs
    lane = lax.broadcasted_iota(I32, (1, LANES), 1)
    first_half = (lane % 32) < 16
    low = lane < HEAD_DIM
    for r0 in range(0, TM, ROW_SPLIT):
        rs = slice(r0, r0 + ROW_SPLIT)
        h = hl_ref[rs, :]
        if two_streams:
            h = jnp.where(pl.program_id(0) >= n_lat, hc_ref[rs, :], h)
        a = _layernorm(h) * (1.0 + mod_ref[1:2, :]) + mod_ref[0:1, :]
        p = jnp.dot(a.astype(BF16), w_ref[...], preferred_element_type=F32)
        cos = cos_ref[rs, :]
        sin = sin_ref[rs, :]

        def rope(x):
            swapped = jnp.where(first_half, pltpu.roll(x, LANES - 16, 1), pltpu.roll(x, 16, 1))
            return x * cos + swapped * sin

        for c in range(ATTN_W // LANES):
            qc = rope(p[:, c * LANES:(c + 1) * LANES]) * (HEAD_DIM ** -0.5 * LOG2E)
            q_ref[rs, c * LANES:(c + 1) * LANES] = qc.astype(BF16)
        k = rope(p[:, ATTN_W:ATTN_W + KV_W])
        k_sw = pltpu.roll(k, HEAD_DIM, 1)
        kk_ref[rs, 0:LANES] = jnp.where(low, k, k_sw).astype(BF16)
        kk_ref[rs, LANES:2 * LANES] = jnp.where(low, k_sw, k).astype(BF16)
        v = p[:, ATTN_W + KV_W:ATTN_W + 2 * KV_W]
        v_sw = pltpu.roll(v, HEAD_DIM, 1)
        vv_ref[rs, 0:LANES] = jnp.where(low, v, v_sw).astype(BF16)
        vv_ref[rs, LANES:2 * LANES] = jnp.where(low, v_sw, v).astype(BF16)
        u_ref[rs, :] = p[:, ATTN_W + 2 * KV_W:]


def _inproj(h, mods, w_in, cos_t, sin_t, layer):
    n_tiles = NT // TM
    tiles_per_batch = SEQ // TM
    n_lat = NL // TM

    def rope_idx(i):
        return (jnp.where(i < n_lat, i % tiles_per_batch, tiles_per_batch), 0)

    return pl.pallas_call(
        functools.partial(_inproj_kernel, n_lat=n_lat, two_streams=len(h) == 2),
        grid=(n_tiles,),
        in_specs=_stream_specs(h, n_lat) + [
            pl.BlockSpec((None, None, 6, D),
                         lambda i: (layer, _mod_row(i, tiles_per_batch, n_lat), 0, 0)),
            pl.BlockSpec((None, D, IN_W), lambda i: (layer, 0, 0)),
            pl.BlockSpec((TM, LANES), rope_idx),
            pl.BlockSpec((TM, LANES), rope_idx),
        ],
        out_specs=[
            pl.BlockSpec((TM, ATTN_W), lambda i: (i, 0)),
            pl.BlockSpec((TM, 2 * KV_W), lambda i: (i, 0)),
            pl.BlockSpec((TM, 2 * KV_W), lambda i: (i, 0)),
            pl.BlockSpec((TM, POOL_W), lambda i: (i, 0)),
        ],
        out_shape=[
            jax.ShapeDtypeStruct((NT, ATTN_W), BF16),
            jax.ShapeDtypeStruct((NT, 2 * KV_W), BF16),
            jax.ShapeDtypeStruct((NT, 2 * KV_W), BF16),
            jax.ShapeDtypeStruct((NT, POOL_W), F32),
        ],
        compiler_params=_cparams(1),
        name="inproj",
    )(*h, mods, w_in, cos_t, sin_t)


def _rope_tables():
    rows = SEQ // GRID_W
    row = jnp.repeat(jnp.arange(rows, dtype=F32), GRID_W)
    col = jnp.tile(jnp.arange(GRID_W, dtype=F32), rows)
    inv = ROPE_BASE ** (-jnp.arange(ROPE_FREQS, dtype=F32) / ROPE_FREQS)
    ang_r = row[:, None] * inv[None, :]
    ang_c = col[:, None] * inv[None, :]
    cr, sr, cc, sc = jnp.cos(ang_r), jnp.sin(ang_r), jnp.cos(ang_c), jnp.sin(ang_c)
    cos_h = jnp.concatenate([cr, cr, cc, cc], axis=-1)
    sin_h = jnp.concatenate([-sr, sr, -sc, sc], axis=-1)
    cos_t = jnp.concatenate([jnp.tile(cos_h, (1, 2)), jnp.ones((TM, LANES), F32)], axis=0)
    sin_t = jnp.concatenate([jnp.tile(sin_h, (1, 2)), jnp.zeros((TM, LANES), F32)], axis=0)
    return cos_t, sin_t


def _pool_group(top, mid, bot, t0, seq_len, width):
    rows = mid.shape[0]
    slab = jnp.concatenate([top, mid, bot], axis=0)
    half = width // 2
    s = slab
    span = 1
    while span < width:
        n = s.shape[0] - span
        s = s[0:n] + s[span:span + n]
        span *= 2
    start = 8 - half
    total = s[start:start + rows]
    pos = t0 + lax.broadcasted_iota(I32, (rows, 1), 0)
    hi = jnp.minimum(pos + half, seq_len)
    lo = jnp.maximum(pos - half, 0)
    cnt = (hi - lo).astype(F32)
    return total / cnt - mid


def _mixer_kernel(sink_ref, q_ref, kk_ref, vv_ref, kkc_ref, vvc_ref, u_ref, wp_ref, ps_ref,
                  o_ref, *, tq, sub, seq_len, band):
    for sb in range(tq // sub):
        _mixer_block(sink_ref, q_ref, kk_ref, vv_ref, kkc_ref, vvc_ref, u_ref, wp_ref, ps_ref,
                     o_ref, t0=pl.program_id(1) * tq + sb * sub, r0=sb * sub, tq=sub,
                     seq_len=seq_len, band=band)


def _mixer_block(sink_ref, q_ref, kk_ref, vv_ref, kkc_ref, vvc_ref, u_ref, wp_ref, ps_ref,
                 o_ref, *, t0, r0, tq, seq_len, band):
    lane = lax.broadcasted_iota(I32, (1, LANES), 1)
    low = lane < HEAD_DIM

    if band:
        n_band = tq + 2 * WINDOW
        ks = jnp.clip(t0 - WINDOW, 0, seq_len - n_band)
        ks = pl.multiple_of(ks, LANES)
        kk = jnp.concatenate([kk_ref[pl.ds(ks, n_band), :], kkc_ref[...]], axis=0)
        vv = jnp.concatenate([vv_ref[pl.ds(ks, n_band), :], vvc_ref[...]], axis=0)
        qpos = t0 + lax.broadcasted_iota(I32, (tq, n_band), 0)
        kpos = ks + lax.broadcasted_iota(I32, (tq, n_band), 1)
        bias = jnp.where(jnp.abs(kpos - qpos) <= WINDOW, 0.0, NEG_INF)
    else:
        n_band = 0
        kk = kkc_ref[...]
        vv = vvc_ref[...]
        bias = None
    nk = kk.shape[0]
    zero = jnp.zeros_like(kk[:, 0:LANES])
    low_f = jnp.broadcast_to(jnp.where(low, 1.0, 0.0), (nk, LANES))
    ones_st = jnp.concatenate([low_f, 1.0 - low_f], axis=0).astype(BF16)

    for g in range(N_KV_HEADS):
        kg = kk[:, g * LANES:(g + 1) * LANES]
        vg = vv[:, g * LANES:(g + 1) * LANES]
        k_st = jnp.concatenate([jnp.where(low, kg, zero), jnp.where(low, zero, kg)], axis=0)
        v_st = jnp.concatenate([jnp.where(low, vg, zero), jnp.where(low, zero, vg)], axis=0)
        v_st = jnp.concatenate([v_st, ones_st], axis=1)
        for pr in range(2):
            j = 2 * g + pr
            qp = q_ref[r0:r0 + tq, j * LANES:(j + 1) * LANES]
            s = lax.dot_general(qp, k_st, (((1,), (1,)), ((), ())), preferred_element_type=F32)
            ps = []
            sink_terms = []
            for hh in range(2):
                sink = sink_ref[2 * j + hh] * LOG2E
                sh = s[:, hh * nk:(hh + 1) * nk]
                if bias is not None:
                    parts = [sh[:, :n_band] + bias, sh[:, n_band:]]
                else:
                    parts = [sh]
                m = sink
                for part in parts:
                    m = jnp.maximum(jnp.max(part, axis=-1, keepdims=True), m)
                sink_terms.append(jnp.exp2(sink - m))
                ps += [jnp.exp2(part - m).astype(BF16) for part in parts]
            od = jnp.dot(jnp.concatenate(ps, axis=1), v_st, preferred_element_type=F32)
            den = od[:, LANES:] + jnp.where(low, sink_terms[0], sink_terms[1])
            o_ref[r0:r0 + tq, j * LANES:(j + 1) * LANES] = (od[:, :LANES] / den).astype(BF16)

    zeros8 = jnp.zeros((8, POOL_W), F32)
    if band:
        t0a = pl.multiple_of(t0, 8)
        mid = u_ref[pl.ds(t0a, tq), :]
        top_s = pl.multiple_of(jnp.maximum(t0 - 8, 0), 8)
        bot_s = pl.multiple_of(jnp.minimum(t0 + tq, seq_len - 8), 8)
        top = jnp.where(t0 > 0, u_ref[pl.ds(top_s, 8), :], zeros8)
        bot = jnp.where(t0 + tq < seq_len, u_ref[pl.ds(bot_s, 8), :], zeros8)
    else:
        mid = u_ref[...]
        top = zeros8
        bot = zeros8
    for g, width in enumerate(POOL_WINDOWS):
        sl = slice(g * POOL_G, (g + 1) * POOL_G)
        diff = _pool_group(top[:, sl], mid[:, sl], bot[:, sl], t0, seq_len, width)
        mixed = jnp.dot(diff.astype(BF16), wp_ref[g], preferred_element_type=F32)
        mixed = mixed * ps_ref[:, sl]
        o_ref[r0:r0 + tq, ATTN_W + g * POOL_G:ATTN_W + (g + 1) * POOL_G] = mixed.astype(BF16)


def _mixer_latent(q, kk, vv, u, sink_l, w_pool, pool_scale, layer):
    nq = SEQ // TQ_STEP
    ctx_blk = NL // CTX
    kern = functools.partial(_mixer_kernel, tq=TQ_STEP, sub=TQ, seq_len=SEQ, band=True)
    return pl.pallas_call(
        kern,
        grid=(B, nq),
        in_specs=[
            pl.BlockSpec(memory_space=pltpu.SMEM),
            pl.BlockSpec((TQ_STEP, ATTN_W), lambda b, t: (b * nq + t, 0)),
            pl.BlockSpec((SEQ, 2 * KV_W), lambda b, t: (b, 0)),
            pl.BlockSpec((SEQ, 2 * KV_W), lambda b, t: (b, 0)),
            pl.BlockSpec((CTX, 2 * KV_W), lambda b, t: (ctx_blk + b, 0)),
            pl.BlockSpec((CTX, 2 * KV_W), lambda b, t: (ctx_blk + b, 0)),
            pl.BlockSpec((SEQ, POOL_W), lambda b, t: (b, 0)),
            pl.BlockSpec((None, len(POOL_WINDOWS), POOL_G, POOL_G), lambda b, t: (layer, 0, 0, 0)),
            pl.BlockSpec((None, 1, POOL_W), lambda b, t: (layer, 0, 0)),
        ],
        out_specs=pl.BlockSpec((TQ_STEP, D), lambda b, t: (b * nq + t, 0)),
        out_shape=jax.ShapeDtypeStruct((NL, D), BF16),
        compiler_params=_cparams(2),
        name="mixer_latent",
    )(sink_l, q, kk, vv, kk, vv, u, w_pool, pool_scale)


def _mixer_context(q, kk, vv, u, sink_l, w_pool, pool_scale, layer):
    ctx_blk = NL // CTX
    kern = functools.partial(_mixer_kernel, tq=CTX, sub=CTX, seq_len=CTX, band=False)

    def body(sink_ref, q_ref, kkc_ref, vvc_ref, u_ref, wp_ref, ps_ref, o_ref):
        kern(sink_ref, q_ref, None, None, kkc_ref, vvc_ref, u_ref, wp_ref, ps_ref, o_ref)

    return pl.pallas_call(
        body,
        grid=(B, 1),
        in_specs=[
            pl.BlockSpec(memory_space=pltpu.SMEM),
            pl.BlockSpec((CTX, ATTN_W), lambda b, t: (ctx_blk + b, 0)),
            pl.BlockSpec((CTX, 2 * KV_W), lambda b, t: (ctx_blk + b, 0)),
            pl.BlockSpec((CTX, 2 * KV_W), lambda b, t: (ctx_blk + b, 0)),
            pl.BlockSpec((CTX, POOL_W), lambda b, t: (ctx_blk + b, 0)),
            pl.BlockSpec((None, len(POOL_WINDOWS), POOL_G, POOL_G), lambda b, t: (layer, 0, 0, 0)),
            pl.BlockSpec((None, 1, POOL_W), lambda b, t: (layer, 0, 0)),
        ],
        out_specs=pl.BlockSpec((CTX, D), lambda b, t: (b, 0)),
        out_shape=jax.ShapeDtypeStruct((NC, D), BF16),
        compiler_params=_cparams(2),
        name="mixer_context",
    )(sink_l, q, kk, vv, u, w_pool, pool_scale)


def _post_norm(h, upd, gate, g, b):
    return _layernorm(ALPHA * h + gate * upd) * g + b


ROW_TILE = D // LANES


def _store_row_tiles(ref, x):
    rows = x.shape[0]
    for c in range(ROW_TILE):
        ref[pl.ds(c, rows, stride=ROW_TILE), :] = x[:, c * LANES:(c + 1) * LANES]


def _load_row_tiles(ref, rows):
    return jnp.concatenate(
        [ref[pl.ds(c, rows, stride=ROW_TILE), :] for c in range(ROW_TILE)], axis=1)


def _route_top2(a, router_ref, ri_ref, rw_ref):
    a_hi = a.astype(BF16)
    a_lo = (a - a_hi.astype(F32)).astype(BF16)
    t_hi = jnp.dot(a_hi, router_ref[...], preferred_element_type=F32)
    t_lo = jnp.dot(a_lo, router_ref[:, :LANES], preferred_element_type=F32)
    logits = t_hi[:, :LANES] + t_hi[:, LANES:] + t_lo
    lane = lax.broadcasted_iota(I32, logits.shape, 1)
    lane_f = lane.astype(F32)
    logits = jnp.where(lane < N_EXPERTS, logits, -jnp.inf)
    m1 = jnp.max(logits, axis=-1, keepdims=True)
    i1 = jnp.min(jnp.where(logits == m1, lane_f, float(LANES)), axis=-1, keepdims=True)
    rest = jnp.where(lane_f == i1, -jnp.inf, logits)
    m2 = jnp.max(rest, axis=-1, keepdims=True)
    i2 = jnp.min(jnp.where(rest == m2, lane_f, float(LANES)), axis=-1, keepdims=True)
    e = jnp.exp(m2 - m1)
    w1 = 1.0 / (1.0 + e)
    w2 = e / (1.0 + e)
    ri_ref[...] = jnp.where(lane == 0, i1, jnp.where(lane == 1, i2, 0.0)).astype(I32)
    rw_ref[...] = jnp.where(lane == 0, w1, jnp.where(lane == 1, w2, 0.0))


def _outproj_kernel(mixl_ref, mixc_ref, *rest, moe, n_lat, has_ctx, two_streams):
    if two_streams:
        hl_ref, hc_ref, mod_ref, w_ref, lng_ref, lnb_ref, *rest = rest
    else:
        hl_ref, mod_ref, w_ref, lng_ref, lnb_ref, *rest = rest
    if moe:
        router_ref, h1_ref, a2_ref, ri_ref, rw_ref = rest
    else:
        h1_ref, a2_ref = rest
    split = TM if moe else ROW_SPLIT
    for r0 in range(0, TM, split):
        rs = slice(r0, r0 + split)
        mix = mixl_ref[rs, :]
        if has_ctx:
            mix = jnp.where(pl.program_id(0) >= n_lat, mixc_ref[rs, :], mix)
        y = jnp.dot(mix, w_ref[...], preferred_element_type=F32)
        h = hl_ref[rs, :]
        if two_streams:
            h = jnp.where(pl.program_id(0) >= n_lat, hc_ref[rs, :], h)
        h1 = _post_norm(h, y, mod_ref[2:3, :], lng_ref[0:1, :], lnb_ref[0:1, :])
        h1_ref[rs, :] = h1
        a2 = _layernorm(h1) * (1.0 + mod_ref[4:5, :]) + mod_ref[3:4, :]
        if moe:
            _store_row_tiles(a2_ref.at[pl.ds(r0 * ROW_TILE, split * ROW_TILE)], a2)
            _route_top2(a2, router_ref, ri_ref.at[pl.ds(r0, split)], rw_ref.at[pl.ds(r0, split)])
        else:
            a2_ref[rs, :] = a2.astype(BF16)


def _outproj(mix_lat, mix_ctx, h, mods, w_out, ln_g, ln_b, router_pad, layer, n_rows):
    n_tiles = n_rows // TM
    tiles_per_batch = SEQ // TM
    n_lat = NL // TM
    moe = router_pad is not None
    has_ctx = n_rows > NL
    in_specs = [
        pl.BlockSpec((TM, D), lambda i: (jnp.minimum(i, n_lat - 1), 0)),
        pl.BlockSpec((TM, D), lambda i: (jnp.maximum(i - n_lat, 0), 0)),
    ] + _stream_specs(h, n_lat) + [
        pl.BlockSpec((None, None, 6, D),
                     lambda i: (layer, _mod_row(i, tiles_per_batch, n_lat), 0, 0)),
        pl.BlockSpec((None, D, D), lambda i: (layer, 0, 0)),
        pl.BlockSpec((None, 2, D), lambda i: (layer, 0, 0)),
        pl.BlockSpec((None, 2, D), lambda i: (layer, 0, 0)),
    ]
    out_specs = [pl.BlockSpec((TM, D), lambda i: (i, 0)), pl.BlockSpec((TM, D), lambda i: (i, 0))]
    out_shape = [jax.ShapeDtypeStruct((n_rows, D), F32), jax.ShapeDtypeStruct((n_rows, D), BF16)]
    args = [mix_lat, mix_ctx, *h, mods, w_out, ln_g, ln_b]
    if moe:
        in_specs.append(pl.BlockSpec((None, D, 2 * LANES), lambda i: (layer // 2, 0, 0)))
        out_specs[1] = pl.BlockSpec((TM * ROW_TILE, LANES), lambda i: (i, 0))
        out_shape[1] = jax.ShapeDtypeStruct((n_rows * ROW_TILE, LANES), F32)
        out_specs += [pl.BlockSpec((TM, LANES), lambda i: (i, 0)),
                      pl.BlockSpec((TM, LANES), lambda i: (i, 0))]
        out_shape += [jax.ShapeDtypeStruct((n_rows, LANES), I32),
                      jax.ShapeDtypeStruct((n_rows, LANES), F32)]
        args.append(router_pad)
    return pl.pallas_call(
        functools.partial(_outproj_kernel, moe=moe, n_lat=n_lat, has_ctx=has_ctx,
                          two_streams=len(h) == 2),
        grid=(n_tiles,),
        in_specs=in_specs,
        out_specs=out_specs,
        out_shape=out_shape,
        compiler_params=_cparams(1),
        name="outproj_moe" if moe else "outproj",
    )(*args)


def _swiglu_hidden(x, w1_ref, w3_ref, t_ref, d_ff):
    for c0 in range(0, d_ff, FF_CHUNK):
        c1 = min(c0 + FF_CHUNK, d_ff)
        g = jnp.dot(x, w1_ref[:, c0:c1], preferred_element_type=F32)
        u = jnp.dot(x, w3_ref[:, c0:c1], preferred_element_type=F32)
        t_ref[:, c0:c1] = (jax.nn.silu(g) * u).astype(BF16)


def _ffn_dense_kernel(a_ref, h_ref, mod_ref, w1_ref, w3_ref, w2_ref, lng_ref, lnb_ref, o_ref,
                      t_ref):
    _swiglu_hidden(a_ref[...], w1_ref, w3_ref, t_ref, D_FF_DENSE)
    f = jnp.dot(t_ref[...], w2_ref[...], preferred_element_type=F32)
    o_ref[...] = _post_norm(h_ref[...], f, mod_ref[5:6, :], lng_ref[1:2, :], lnb_ref[1:2, :])


def _ffn_dense(a2, h1, mods, w1, w3, w2, ln_g, ln_b, layer, n_rows):
    n_tiles = n_rows // TM
    tiles_per_batch = SEQ // TM
    n_lat = NL // TM
    idx = layer // 2
    resident = pl.Buffered(1)
    return pl.pallas_call(
        _ffn_dense_kernel,
        grid=(n_tiles,),
        in_specs=[
            pl.BlockSpec((TM, D), lambda i: (i, 0)),
            pl.BlockSpec((TM, D), lambda i: (i, 0)),
            pl.BlockSpec((None, None, 6, D),
                         lambda i: (layer, _mod_row(i, tiles_per_batch, n_lat), 0, 0)),
            pl.BlockSpec((None, D, D_FF_DENSE), lambda i: (idx, 0, 0), pipeline_mode=resident),
            pl.BlockSpec((None, D, D_FF_DENSE), lambda i: (idx, 0, 0), pipeline_mode=resident),
            pl.BlockSpec((None, D_FF_DENSE, D), lambda i: (idx, 0, 0), pipeline_mode=resident),
            pl.BlockSpec((None, 2, D), lambda i: (layer, 0, 0)),
            pl.BlockSpec((None, 2, D), lambda i: (layer, 0, 0)),
        ],
        out_specs=pl.BlockSpec((TM, D), lambda i: (i, 0)),
        out_shape=jax.ShapeDtypeStruct((n_rows, D), F32),
        scratch_shapes=[pltpu.VMEM((TM, D_FF_DENSE), BF16)],
        compiler_params=_cparams(1),
        name="ffn_dense",
    )(a2, h1, mods, w1, w3, w2, ln_g, ln_b)


def _expert_first_tile(e, idx, x, issue_rows, w1_hbm, w3_hbm, w2_hbm, w1_ref, w3_ref, w2_ref,
                       st_col, st_row, wsem, t_ref, o_ref):
    n_c = D_FF_EXPERT // W_CHUNK
    jobs = []
    for c in range(n_c):
        jobs += [(True, w1_hbm, w1_ref, c), (True, w3_hbm, w3_ref, c)]
    jobs += [(False, w2_hbm, w2_ref, c) for c in range(n_c)]
    n_col = 2 * n_c

    def ring_slot(j):
        return j % W_RING if j < n_col else (j - n_col) % W_RING

    def copy(j):
        by_col, src, _, c = jobs[j]
        s = ring_slot(j)
        if by_col:
            return pltpu.make_async_copy(src.at[idx, e, :, pl.ds(c * W_CHUNK, W_CHUNK)],
                                         st_col.at[s], wsem.at[s])
        return pltpu.make_async_copy(src.at[idx, e, pl.ds(c * W_CHUNK, W_CHUNK), :],
                                     st_row.at[s], wsem.at[W_RING + s])

    def consume(j):
        if j + W_RING - 1 < len(jobs):
            copy(j + W_RING - 1).start()
        copy(j).wait()
        by_col, _, dst, c = jobs[j]
        if by_col:
            dst[:, c * W_CHUNK:(c + 1) * W_CHUNK] = st_col[ring_slot(j)].astype(BF16)
        else:
            dst[c * W_CHUNK:(c + 1) * W_CHUNK, :] = st_row[ring_slot(j)].astype(BF16)

    for j in range(W_RING - 1):
        copy(j).start()
    for c in range(n_c):
        issue_rows(c * MOE_TM // n_c, (c + 1) * MOE_TM // n_c)
        consume(2 * c)
        consume(2 * c + 1)
        cs = slice(c * W_CHUNK, (c + 1) * W_CHUNK)
        g = jnp.dot(x, w1_ref[:, cs], preferred_element_type=F32)
        u = jnp.dot(x, w3_ref[:, cs], preferred_element_type=F32)
        t_ref[:, cs] = (jax.nn.silu(g) * u).astype(BF16)
    y = None
    for c0 in range(0, n_c, 2):
        consume(n_col + c0)
        consume(n_col + c0 + 1)
        ks = slice(c0 * W_CHUNK, (c0 + 2) * W_CHUNK)
        part = jnp.dot(t_ref[:, ks], w2_ref[ks, :], preferred_element_type=F32)
        y = part if y is None else y + part
    _store_row_tiles(o_ref, y)


def _experts_kernel(pos_ref, te_ref, tv_ref, cnt_ref, off_ref, end_ref,
                    a_hbm, w1_hbm, w3_hbm, w2_hbm, o_ref, src_ref, xg_ref, t_ref,
                    w1_ref, w3_ref, w2_ref, st_col, st_row, sem, wsem,
                    *, n_assign, n_slots, idx):
    i = pl.program_id(0)
    n = pl.num_programs(0)

    def start_row(tile, slot, r):
        s = pl.multiple_of(src_ref[tile * MOE_TM + r] * ROW_TILE, ROW_TILE)
        d = r * ROW_TILE if isinstance(r, int) else pl.multiple_of(r * ROW_TILE, ROW_TILE)
        pltpu.make_async_copy(a_hbm.at[pl.ds(s, ROW_TILE)],
                              xg_ref.at[slot, pl.ds(d, ROW_TILE)], sem.at[slot]).start()

    def wait_tile(slot):
        pltpu.make_async_copy(a_hbm.at[pl.ds(0, MOE_TM * ROW_TILE)], xg_ref.at[slot],
                              sem.at[slot]).wait()

    def clear(s, carry):
        src_ref[s] = 0
        return carry

    @pl.when(i == 0)
    def _():
        for e in range(N_EXPERTS):
            lax.fori_loop(off_ref[e] + cnt_ref[e], end_ref[e], clear, 0)
        lax.fori_loop(end_ref[N_EXPERTS - 1], n_slots, clear, 0)

        def fill(a, carry):
            src_ref[pos_ref[a]] = lax.shift_right_logical(a, 1)
            return carry

        lax.fori_loop(0, n_assign, fill, 0, unroll=16)

        def first(r, carry):
            start_row(0, 0, r)
            return carry

        lax.fori_loop(0, MOE_TM, first, 0, unroll=8)

    valid = tv_ref[i] > 0
    slot = i % 2
    prev = jnp.maximum(i - 1, 0)

    @pl.when(jnp.logical_or(i == 0, tv_ref[prev] > 0))
    def _():
        wait_tile(slot)

    nxt_tile = lax.rem(i + 1, n)

    def issue_rows(lo, hi):
        for r in range(lo, hi):
            start_row(nxt_tile, 1 - slot, r)

    group_start = jnp.logical_or(i == 0, te_ref[i] != te_ref[prev])

    @pl.when(jnp.logical_and(valid, group_start))
    def _():
        x = _load_row_tiles(xg_ref.at[slot], MOE_TM).astype(BF16)
        _expert_first_tile(te_ref[i], idx, x, issue_rows, w1_hbm, w3_hbm, w2_hbm,
                           w1_ref, w3_ref, w2_ref, st_col, st_row, wsem, t_ref, o_ref)

    @pl.when(jnp.logical_and(valid, jnp.logical_not(group_start)))
    def _():
        x = _load_row_tiles(xg_ref.at[slot], MOE_TM).astype(BF16)
        chunks = [(c0, min(c0 + FF_CHUNK, D_FF_EXPERT)) for c0 in range(0, D_FF_EXPERT, FF_CHUNK)]
        per_chunk = MOE_TM // len(chunks)
        for ci, (c0, c1) in enumerate(chunks):
            issue_rows(ci * per_chunk, (ci + 1) * per_chunk)
            g = jnp.dot(x, w1_ref[:, c0:c1], preferred_element_type=F32)
            u = jnp.dot(x, w3_ref[:, c0:c1], preferred_element_type=F32)
            t_ref[:, c0:c1] = (jax.nn.silu(g) * u).astype(BF16)
        y = jnp.dot(t_ref[...], w2_ref[...], preferred_element_type=F32)
        _store_row_tiles(o_ref, y)

    @pl.when(jnp.logical_not(valid))
    def _():
        o_ref[...] = jnp.zeros_like(o_ref)

    @pl.when(jnp.logical_and(valid, i == n - 1))
    def _():
        wait_tile(1 - slot)


def _experts(plan, a2r, w1, w3, w2, idx, n_rows, n_slots):
    n_tiles = n_slots // MOE_TM
    return pl.pallas_call(
        functools.partial(_experts_kernel, n_assign=2 * n_rows, n_slots=n_slots, idx=idx),
        grid_spec=pltpu.PrefetchScalarGridSpec(
            num_scalar_prefetch=6,
            grid=(n_tiles,),
            in_specs=[pl.BlockSpec(memory_space=pl.ANY)] * 4,
            out_specs=pl.BlockSpec((MOE_TM * ROW_TILE, LANES), lambda i, *_: (i, 0)),
            scratch_shapes=[
                pltpu.SMEM((n_slots,), I32),
                pltpu.VMEM((2, MOE_TM * ROW_TILE, LANES), F32),
                pltpu.VMEM((MOE_TM, D_FF_EXPERT), BF16),
                pltpu.VMEM((D, D_FF_EXPERT), BF16),
                pltpu.VMEM((D, D_FF_EXPERT), BF16),
                pltpu.VMEM((D_FF_EXPERT, D), BF16),
                pltpu.VMEM((W_RING, D, W_CHUNK), F32),
                pltpu.VMEM((W_RING, W_CHUNK, D), F32),
                pltpu.SemaphoreType.DMA((2,)),
                pltpu.SemaphoreType.DMA((2 * W_RING,)),
            ],
        ),
        out_shape=jax.ShapeDtypeStruct((n_slots * ROW_TILE, LANES), F32),
        compiler_params=_cparams(1, VMEM_LIMIT_EXPERTS),
        name="moe_experts",
    )(*plan, a2r, w1, w3, w2)


def _combine_kernel(pos_ref, y_hbm, h_ref, rw_ref, mod_ref, lng_ref, lnb_ref, o_ref, buf, sem):
    i = pl.program_id(0)
    n = pl.num_programs(0)

    def start_row(step, slot, r):
        d = r * ROW_TILE if isinstance(r, int) else pl.multiple_of(r * ROW_TILE, ROW_TILE)
        for k in range(2):
            p = pl.multiple_of(pos_ref[2 * (step * COMBINE_ROWS + r) + k] * ROW_TILE, ROW_TILE)
            pltpu.make_async_copy(y_hbm.at[pl.ds(p, ROW_TILE)],
                                  buf.at[slot, k, pl.ds(d, ROW_TILE)], sem.at[slot]).start()

    def wait_tile(slot):
        for k in range(2):
            pltpu.make_async_copy(y_hbm.at[pl.ds(0, COMBINE_ROWS * ROW_TILE)], buf.at[slot, k],
                                  sem.at[slot]).wait()

    ahead = COMBINE_SLOTS - 1

    @pl.when(i == 0)
    def _():
        for t in range(ahead):
            def first(r, carry):
                start_row(t, t, r)
                return carry

            lax.fori_loop(0, COMBINE_ROWS, first, 0, unroll=4)

    slot = lax.rem(i, COMBINE_SLOTS)
    wait_tile(slot)
    nxt = lax.rem(i + ahead, n)
    nxt_slot = lax.rem(i + ahead, COMBINE_SLOTS)
    chunk = COMBINE_ROWS // COMBINE_CHUNKS
    for c in range(COMBINE_CHUNKS):
        for r in range(c * chunk, (c + 1) * chunk):
            start_row(nxt, nxt_slot, r)
        rs = slice(c * chunk, (c + 1) * chunk)
        tiles = pl.ds(c * chunk * ROW_TILE, chunk * ROW_TILE)
        w = rw_ref[rs, :]
        f = (w[:, 0:1] * _load_row_tiles(buf.at[slot, 0, tiles], chunk)
             + w[:, 1:2] * _load_row_tiles(buf.at[slot, 1, tiles], chunk))
        o_ref[rs, :] = _post_norm(h_ref[rs, :], f, mod_ref[5:6, :], lng_ref[1:2, :],
                                  lnb_ref[1:2, :])

    @pl.when(i == n - 1)
    def _():
        for t in range(1, COMBINE_SLOTS):
            wait_tile(lax.rem(i + t, COMBINE_SLOTS))


def _combine(pos, y, h1, rw, mods, ln_g, ln_b, layer, n_rows):
    n_tiles = n_rows // COMBINE_ROWS
    tiles_per_batch = SEQ // COMBINE_ROWS
    n_lat = NL // COMBINE_ROWS
    return pl.pallas_call(
        _combine_kernel,
        grid_spec=pltpu.PrefetchScalarGridSpec(
            num_scalar_prefetch=1,
            grid=(n_tiles,),
            in_specs=[
                pl.BlockSpec(memory_space=pl.ANY),
                pl.BlockSpec((COMBINE_ROWS, D), lambda i, p: (i, 0)),
                pl.BlockSpec((COMBINE_ROWS, LANES), lambda i, p: (i, 0)),
                pl.BlockSpec((None, None, 6, D),
                             lambda i, p: (layer, _mod_row(i, tiles_per_batch, n_lat), 0, 0)),
                pl.BlockSpec((None, 2, D), lambda i, p: (layer, 0, 0)),
                pl.BlockSpec((None, 2, D), lambda i, p: (layer, 0, 0)),
            ],
            out_specs=pl.BlockSpec((COMBINE_ROWS, D), lambda i, p: (i, 0)),
            scratch_shapes=[pltpu.VMEM((COMBINE_SLOTS, 2, COMBINE_ROWS * ROW_TILE, LANES), F32),
                            pltpu.SemaphoreType.DMA((COMBINE_SLOTS,))],
        ),
        out_shape=jax.ShapeDtypeStruct((n_rows, D), F32),
        compiler_params=_cparams(1),
        name="moe_combine",
    )(pos, y, h1, rw, mods, ln_g, ln_b)


def _route_plan(eid, n_rows, n_slots):
    n_tiles = n_slots // MOE_TM
    e_flat = eid.reshape(-1)
    onehot = (e_flat[:, None] == jnp.arange(N_EXPERTS, dtype=I32)[None, :]).astype(I32)
    csum = jnp.cumsum(onehot, axis=0)
    rank = jnp.sum((csum - onehot) * onehot, axis=1)
    count = csum[-1]
    padded = ((count + MOE_TM - 1) // MOE_TM) * MOE_TM
    ends = jnp.cumsum(padded)
    offs = ends - padded
    pos = jnp.sum(onehot * offs[None, :], axis=1) + rank
    tile_start = jnp.arange(n_tiles, dtype=I32) * MOE_TM
    tile_e = jnp.minimum(jnp.sum((tile_start[:, None] >= ends[None, :]).astype(I32), axis=1),
                         N_EXPERTS - 1)
    tile_v = (tile_start < ends[-1]).astype(I32)
    return (pos.astype(I32), tile_e.astype(I32), tile_v, count.astype(I32), offs.astype(I32),
            ends.astype(I32))


def _moe(a2r, h1, ri, rw, mods, w1, w3, w2, ln_g, ln_b, layer, n_rows):
    n_slots = 2 * n_rows + N_EXPERTS * MOE_TM
    plan = _route_plan(ri[:, :2], n_rows, n_slots)
    ys = _experts(plan, a2r, w1, w3, w2, layer // 2, n_rows, n_slots)
    return _combine(plan[0], ys, h1, rw, mods, ln_g, ln_b, layer, n_rows)


def kernel(x, c, ctx, c_ctx, w_ada, b_ada, w_in, w_pool, pool_scale, sink, w_out, ln_g, ln_b,
           dense_w1, dense_w3, dense_w2, router, moe_w1, moe_w3, moe_w2):
    cond = jnp.concatenate([c, c_ctx[None, :], jnp.zeros((MOD_ROWS - B - 1, D), F32)], axis=0)
    mods = _ada_tables(cond, w_ada, b_ada)
    cos_t, sin_t = _rope_tables()

    w_in_b = w_in.astype(BF16)
    w_pool_b = w_pool.astype(BF16)
    w_out_b = w_out.astype(BF16)
    dw1, dw3, dw2 = dense_w1.astype(BF16), dense_w3.astype(BF16), dense_w2.astype(BF16)
    r_hi = router.astype(BF16)
    r_lo = (router - r_hi.astype(F32)).astype(BF16)
    lane_pad = ((0, 0), (0, 0), (0, LANES - N_EXPERTS))
    router_pad = jnp.concatenate([jnp.pad(r_hi, lane_pad), jnp.pad(r_lo, lane_pad)], axis=-1)
    pool_scale3 = pool_scale.reshape(DEPTH, 1, POOL_W)

    h = (x.reshape(NL, D), ctx.reshape(NC, D))
    for l in range(DEPTH):
        last = l == DEPTH - 1
        n_rows = NL if last else NT
        q, kk, vv, u = _inproj(h, mods, w_in_b, cos_t, sin_t, l)
        mix = _mixer_latent(q, kk, vv, u, sink[l], w_pool_b, pool_scale3, l)
        mix_c = mix if last else _mixer_context(q, kk, vv, u, sink[l], w_pool_b, pool_scale3, l)
        if l % 2 == 0:
            h1, a2 = _outproj(mix, mix_c, h, mods, w_out_b, ln_g, ln_b, None, l, n_rows)
            h = (_ffn_dense(a2, h1, mods, dw1, dw3, dw2, ln_g, ln_b, l, n_rows),)
        else:
            h1, a2, ri, rw = _outproj(mix, mix_c, h, mods, w_out_b, ln_g, ln_b, router_pad, l,
                                      n_rows)
            h = (_moe(a2, h1, ri, rw, mods, moe_w1, moe_w3, moe_w2, ln_g, ln_b, l, n_rows),)
    return h[0].reshape(B, SEQ, D)
```

```python
import functools

import jax
import jax.numpy as jnp
from jax import lax
from jax.experimental import pallas as pl
from jax.experimental.pallas import tpu as pltpu

F32 = jnp.float32
BF16 = jnp.bfloat16
I32 = jnp.int32

D = 1024
B = 8
SEQ = 2048
DEPTH = 4
CTX = 256
GRID_W = 64
HEAD_DIM = 64
N_Q_HEADS = 8
N_KV_HEADS = 2
ATTN_W = N_Q_HEADS * HEAD_DIM
KV_W = N_KV_HEADS * HEAD_DIM
POOL_WINDOWS = (2, 4, 8, 16)
POOL_W = D - ATTN_W
POOL_G = POOL_W // len(POOL_WINDOWS)
IN_W = ATTN_W + 2 * KV_W + POOL_W
WINDOW = 128
ROPE_BASE = 10000.0
ROPE_FREQS = HEAD_DIM // 4
D_FF_DENSE = 2816
N_EXPERTS = 8
D_FF_EXPERT = 3584
ALPHA = (2 * DEPTH) ** 0.25
LN_EPS = 1e-6
NEG_INF = -1e30
LOG2E = 1.4426950408889634

NL = B * SEQ
NC = B * CTX
NT = NL + NC
MOD_ROWS = 16
LANES = 128

TM = 1024
TM_FFN = 512
ROW_SPLIT_ROUTER = 512
TQ = 128
TQ_STEP = 512
ROW_SPLIT = 256
MOE_TM = 512
FF_CHUNK = 1024
W_CHUNK = 256
W_RING = 5
COMBINE_ROWS = 256
COMBINE_CHUNKS = 4
COMBINE_SLOTS = 3

VMEM_LIMIT = 56 * 1024 * 1024
VMEM_LIMIT_EXPERTS = 60 * 1024 * 1024


def _cparams(n_axes, vmem=VMEM_LIMIT):
    return pltpu.CompilerParams(
        dimension_semantics=("arbitrary",) * n_axes, vmem_limit_bytes=vmem)


def _layernorm(x):
    mu = jnp.mean(x, axis=-1, keepdims=True)
    xc = x - mu
    var = jnp.mean(xc * xc, axis=-1, keepdims=True)
    return xc * lax.rsqrt(var + LN_EPS)


def _mod_row(tile, tiles_per_batch, n_latent_tiles):
    return jnp.where(tile < n_latent_tiles, tile // tiles_per_batch, B)


def _stream_specs(h, n_lat):
    if len(h) == 1:
        return [pl.BlockSpec((TM, D), lambda i: (i, 0))]
    return [pl.BlockSpec((TM, D), lambda i: (jnp.minimum(i, n_lat - 1), 0)),
            pl.BlockSpec((TM, D), lambda i: (jnp.maximum(i - n_lat, 0), 0))]


ADA_TN = 1536


def _ada_kernel(s_ref, w_ref, b_ref, o_ref):
    s = jax.nn.silu(s_ref[...]).astype(BF16)
    w = w_ref[...].astype(BF16)
    o_ref[...] = jnp.dot(s, w, preferred_element_type=F32) + b_ref[...]


def _ada_tables(cond, w_ada, b_ada):
    n_col = (6 * D) // ADA_TN
    out = pl.pallas_call(
        _ada_kernel,
        grid=(DEPTH, n_col),
        in_specs=[
            pl.BlockSpec((MOD_ROWS, D), lambda l, j: (0, 0)),
            pl.BlockSpec((None, D, ADA_TN), lambda l, j: (l, 0, j)),
            pl.BlockSpec((None, 1, ADA_TN), lambda l, j: (l, 0, j)),
        ],
        out_specs=pl.BlockSpec((None, MOD_ROWS, ADA_TN), lambda l, j: (l, 0, j)),
        out_shape=jax.ShapeDtypeStruct((DEPTH, MOD_ROWS, 6 * D), F32),
        compiler_params=_cparams(2),
        name="ada_tables",
    )(cond, w_ada, b_ada.reshape(DEPTH, 1, 6 * D))
    return out.reshape(DEPTH, MOD_ROWS, 6, D)


def _inproj_kernel(*refs, n_lat, two_streams):
    if two_streams:
        hl_ref, hc_ref, mod_ref, w_ref, cos_ref, sin_ref, q_ref, kk_ref, vv_ref, u_ref = refs
    else:
        hl_ref, mod_ref, w_ref, cos_ref, sin_ref, q_ref, kk_ref, vv_ref, u_ref = refs
    lane = lax.broadcasted_iota(I32, (1, LANES), 1)
    first_half = (lane % 32) < 16
    low = lane < HEAD_DIM
    for r0 in range(0, TM, ROW_SPLIT):
        rs = slice(r0, r0 + ROW_SPLIT)
        h = hl_ref[rs, :]
        if two_streams:
            h = jnp.where(pl.program_id(0) >= n_lat, hc_ref[rs, :], h)
        a = _layernorm(h) * (1.0 + mod_ref[1:2, :]) + mod_ref[0:1, :]
        p = jnp.dot(a.astype(BF16), w_ref[...], preferred_element_type=F32)
        cos = cos_ref[rs, :]
        sin = sin_ref[rs, :]

        def rope(x):
            swapped = jnp.where(first_half, pltpu.roll(x, LANES - 16, 1), pltpu.roll(x, 16, 1))
            return x * cos + swapped * sin

        for c in range(ATTN_W // LANES):
            qc = rope(p[:, c * LANES:(c + 1) * LANES]) * (HEAD_DIM ** -0.5 * LOG2E)
            q_ref[rs, c * LANES:(c + 1) * LANES] = qc.astype(BF16)
        k = rope(p[:, ATTN_W:ATTN_W + KV_W])
        k_sw = pltpu.roll(k, HEAD_DIM, 1)
        kk_ref[rs, 0:LANES] = jnp.where(low, k, k_sw).astype(BF16)
        kk_ref[rs, LANES:2 * LANES] = jnp.where(low, k_sw, k).astype(BF16)
        v = p[:, ATTN_W + KV_W:ATTN_W + 2 * KV_W]
        v_sw = pltpu.roll(v, HEAD_DIM, 1)
        vv_ref[rs, 0:LANES] = jnp.where(low, v, v_sw).astype(BF16)
        vv_ref[rs, LANES:2 * LANES] = jnp.where(low, v_sw, v).astype(BF16)
        u_ref[rs, :] = p[:, ATTN_W + 2 * KV_W:]


def _inproj(h, mods, w_in, cos_t, sin_t, layer):
    n_tiles = NT // TM
    tiles_per_batch = SEQ // TM
    n_lat = NL // TM

    def rope_idx(i):
        return (jnp.where(i < n_lat, i % tiles_per_batch, tiles_per_batch), 0)

    return pl.pallas_call(
        functools.partial(_inproj_kernel, n_lat=n_lat, two_streams=len(h) == 2),
        grid=(n_tiles,),
        in_specs=_stream_specs(h, n_lat) + [
            pl.BlockSpec((None, None, 6, D),
                         lambda i: (layer, _mod_row(i, tiles_per_batch, n_lat), 0, 0)),
            pl.BlockSpec((None, D, IN_W), lambda i: (layer, 0, 0)),
            pl.BlockSpec((TM, LANES), rope_idx),
            pl.BlockSpec((TM, LANES), rope_idx),
        ],
        out_specs=[
            pl.BlockSpec((TM, ATTN_W), lambda i: (i, 0)),
            pl.BlockSpec((TM, 2 * KV_W), lambda i: (i, 0)),
            pl.BlockSpec((TM, 2 * KV_W), lambda i: (i, 0)),
            pl.BlockSpec((TM, POOL_W), lambda i: (i, 0)),
        ],
        out_shape=[
            jax.ShapeDtypeStruct((NT, ATTN_W), BF16),
            jax.ShapeDtypeStruct((NT, 2 * KV_W), BF16),
            jax.ShapeDtypeStruct((NT, 2 * KV_W), BF16),
            jax.ShapeDtypeStruct((NT, POOL_W), F32),
        ],
        compiler_params=_cparams(1),
        name="inproj",
    )(*h, mods, w_in, cos_t, sin_t)


def _rope_tables():
    rows = SEQ // GRID_W
    row = jnp.repeat(jnp.arange(rows, dtype=F32), GRID_W)
    col = jnp.tile(jnp.arange(GRID_W, dtype=F32), rows)
    inv = ROPE_BASE ** (-jnp.arange(ROPE_FREQS, dtype=F32) / ROPE_FREQS)
    ang_r = row[:, None] * inv[None, :]
    ang_c = col[:, None] * inv[None, :]
    cr, sr, cc, sc = jnp.cos(ang_r), jnp.sin(ang_r), jnp.cos(ang_c), jnp.sin(ang_c)
    cos_h = jnp.concatenate([cr, cr, cc, cc], axis=-1)
    sin_h = jnp.concatenate([-sr, sr, -sc, sc], axis=-1)
    cos_t = jnp.concatenate([jnp.tile(cos_h, (1, 2)), jnp.ones((TM, LANES), F32)], axis=0)
    sin_t = jnp.concatenate([jnp.tile(sin_h, (1, 2)), jnp.zeros((TM, LANES), F32)], axis=0)
    return cos_t, sin_t


def _pool_group(top, mid, bot, t0, seq_len, width):
    rows = mid.shape[0]
    slab = jnp.concatenate([top, mid, bot], axis=0)
    half = width // 2
    s = slab
    span = 1
    while span < width:
        n = s.shape[0] - span
        s = s[0:n] + s[span:span + n]
        span *= 2
    start = 8 - half
    total = s[start:start + rows]
    pos = t0 + lax.broadcasted_iota(I32, (rows, 1), 0)
    hi = jnp.minimum(pos + half, seq_len)
    lo = jnp.maximum(pos - half, 0)
    cnt = (hi - lo).astype(F32)
    return total / cnt - mid


def _mixer_kernel(sink_ref, q_ref, kk_ref, vv_ref, kkc_ref, vvc_ref, u_ref, wp_ref, ps_ref,
                  o_ref, *, tq, sub, seq_len, band):
    for sb in range(tq // sub):
        _mixer_block(sink_ref, q_ref, kk_ref, vv_ref, kkc_ref, vvc_ref, u_ref, wp_ref, ps_ref,
                     o_ref, t0=pl.program_id(1) * tq + sb * sub, r0=sb * sub, tq=sub,
                     seq_len=seq_len, band=band)


def _mixer_block(sink_ref, q_ref, kk_ref, vv_ref, kkc_ref, vvc_ref, u_ref, wp_ref, ps_ref,
                 o_ref, *, t0, r0, tq, seq_len, band):
    lane = lax.broadcasted_iota(I32, (1, LANES), 1)
    low = lane < HEAD_DIM

    if band:
        n_band = tq + 2 * WINDOW
        ks = jnp.clip(t0 - WINDOW, 0, seq_len - n_band)
        ks = pl.multiple_of(ks, LANES)
        kk = jnp.concatenate([kk_ref[pl.ds(ks, n_band), :], kkc_ref[...]], axis=0)
        vv = jnp.concatenate([vv_ref[pl.ds(ks, n_band), :], vvc_ref[...]], axis=0)
        qpos = t0 + lax.broadcasted_iota(I32, (tq, n_band), 0)
        kpos = ks + lax.broadcasted_iota(I32, (tq, n_band), 1)
        bias = jnp.where(jnp.abs(kpos - qpos) <= WINDOW, 0.0, NEG_INF)
    else:
        n_band = 0
        kk = kkc_ref[...]
        vv = vvc_ref[...]
        bias = None
    nk = kk.shape[0]
    zero = jnp.zeros_like(kk[:, 0:LANES])
    low_f = jnp.broadcast_to(jnp.where(low, 1.0, 0.0), (nk, LANES))
    ones_st = jnp.concatenate([low_f, 1.0 - low_f], axis=0).astype(BF16)

    for g in range(N_KV_HEADS):
        kg = kk[:, g * LANES:(g + 1) * LANES]
        vg = vv[:, g * LANES:(g + 1) * LANES]
        k_st = jnp.concatenate([jnp.where(low, kg, zero), jnp.where(low, zero, kg)], axis=0)
        v_st = jnp.concatenate([jnp.where(low, vg, zero), jnp.where(low, zero, vg)], axis=0)
        v_st = jnp.concatenate([v_st, ones_st], axis=1)
        for pr in range(2):
            j = 2 * g + pr
            qp = q_ref[r0:r0 + tq, j * LANES:(j + 1) * LANES]
            s = lax.dot_general(qp, k_st, (((1,), (1,)), ((), ())), preferred_element_type=F32)
            ps = []
            sink_terms = []
            for hh in range(2):
                sink = sink_ref[2 * j + hh] * LOG2E
                sh = s[:, hh * nk:(hh + 1) * nk]
                if bias is not None:
                    parts = [sh[:, :n_band] + bias, sh[:, n_band:]]
                else:
                    parts = [sh]
                m = sink
                for part in parts:
                    m = jnp.maximum(jnp.max(part, axis=-1, keepdims=True), m)
                sink_terms.append(jnp.exp2(sink - m))
                ps += [jnp.exp2(part - m).astype(BF16) for part in parts]
            od = jnp.dot(jnp.concatenate(ps, axis=1), v_st, preferred_element_type=F32)
            den = od[:, LANES:] + jnp.where(low, sink_terms[0], sink_terms[1])
            o_ref[r0:r0 + tq, j * LANES:(j + 1) * LANES] = (od[:, :LANES] / den).astype(BF16)

    zeros8 = jnp.zeros((8, POOL_W), F32)
    if band:
        t0a = pl.multiple_of(t0, 8)
        mid = u_ref[pl.ds(t0a, tq), :]
        top_s = pl.multiple_of(jnp.maximum(t0 - 8, 0), 8)
        bot_s = pl.multiple_of(jnp.minimum(t0 + tq, seq_len - 8), 8)
        top = jnp.where(t0 > 0, u_ref[pl.ds(top_s, 8), :], zeros8)
        bot = jnp.where(t0 + tq < seq_len, u_ref[pl.ds(bot_s, 8), :], zeros8)
    else:
        mid = u_ref[...]
        top = zeros8
        bot = zeros8
    for g, width in enumerate(POOL_WINDOWS):
        sl = slice(g * POOL_G, (g + 1) * POOL_G)
        diff = _pool_group(top[:, sl], mid[:, sl], bot[:, sl], t0, seq_len, width)
        mixed = jnp.dot(diff.astype(BF16), wp_ref[g], preferred_element_type=F32)
        mixed = mixed * ps_ref[:, sl]
        o_ref[r0:r0 + tq, ATTN_W + g * POOL_G:ATTN_W + (g + 1) * POOL_G] = mixed.astype(BF16)


def _mixer_latent(q, kk, vv, u, sink_l, w_pool, pool_scale, layer):
    nq = SEQ // TQ_STEP
    ctx_blk = NL // CTX
    kern = functools.partial(_mixer_kernel, tq=TQ_STEP, sub=TQ, seq_len=SEQ, band=True)
    return pl.pallas_call(
        kern,
        grid=(B, nq),
        in_specs=[
            pl.BlockSpec(memory_space=pltpu.SMEM),
            pl.BlockSpec((TQ_STEP, ATTN_W), lambda b, t: (b * nq + t, 0)),
            pl.BlockSpec((SEQ, 2 * KV_W), lambda b, t: (b, 0)),
            pl.BlockSpec((SEQ, 2 * KV_W), lambda b, t: (b, 0)),
            pl.BlockSpec((CTX, 2 * KV_W), lambda b, t: (ctx_blk + b, 0)),
            pl.BlockSpec((CTX, 2 * KV_W), lambda b, t: (ctx_blk + b, 0)),
            pl.BlockSpec((SEQ, POOL_W), lambda b, t: (b, 0)),
            pl.BlockSpec((None, len(POOL_WINDOWS), POOL_G, POOL_G), lambda b, t: (layer, 0, 0, 0)),
            pl.BlockSpec((None, 1, POOL_W), lambda b, t: (layer, 0, 0)),
        ],
        out_specs=pl.BlockSpec((TQ_STEP, D), lambda b, t: (b * nq + t, 0)),
        out_shape=jax.ShapeDtypeStruct((NL, D), BF16),
        compiler_params=_cparams(2),
        name="mixer_latent",
    )(sink_l, q, kk, vv, kk, vv, u, w_pool, pool_scale)


def _mixer_context(q, kk, vv, u, sink_l, w_pool, pool_scale, layer):
    ctx_blk = NL // CTX
    kern = functools.partial(_mixer_kernel, tq=CTX, sub=CTX, seq_len=CTX, band=False)

    def body(sink_ref, q_ref, kkc_ref, vvc_ref, u_ref, wp_ref, ps_ref, o_ref):
        kern(sink_ref, q_ref, None, None, kkc_ref, vvc_ref, u_ref, wp_ref, ps_ref, o_ref)

    return pl.pallas_call(
        body,
        grid=(B, 1),
        in_specs=[
            pl.BlockSpec(memory_space=pltpu.SMEM),
            pl.BlockSpec((CTX, ATTN_W), lambda b, t: (ctx_blk + b, 0)),
            pl.BlockSpec((CTX, 2 * KV_W), lambda b, t: (ctx_blk + b, 0)),
            pl.BlockSpec((CTX, 2 * KV_W), lambda b, t: (ctx_blk + b, 0)),
            pl.BlockSpec((CTX, POOL_W), lambda b, t: (ctx_blk + b, 0)),
            pl.BlockSpec((None, len(POOL_WINDOWS), POOL_G, POOL_G), lambda b, t: (layer, 0, 0, 0)),
            pl.BlockSpec((None, 1, POOL_W), lambda b, t: (layer, 0, 0)),
        ],
        out_specs=pl.BlockSpec((CTX, D), lambda b, t: (b, 0)),
        out_shape=jax.ShapeDtypeStruct((NC, D), BF16),
        compiler_params=_cparams(2),
        name="mixer_context",
    )(sink_l, q, kk, vv, u, w_pool, pool_scale)


def _post_norm(h, upd, gate, g, b):
    return _layernorm(ALPHA * h + gate * upd) * g + b


ROW_TILE = D // LANES


def _store_row_tiles(ref, x):
    rows = x.shape[0]
    for c in range(ROW_TILE):
        ref[pl.ds(c, rows, stride=ROW_TILE), :] = x[:, c * LANES:(c + 1) * LANES]


def _load_row_tiles(ref, rows):
    return jnp.concatenate(
        [ref[pl.ds(c, rows, stride=ROW_TILE), :] for c in range(ROW_TILE)], axis=1)


def _route_top2(a, router_ref, ri_ref, rw_ref):
    a_hi = a.astype(BF16)
    a_lo = (a - a_hi.astype(F32)).astype(BF16)
    t_hi = jnp.dot(a_hi, router_ref[...], preferred_element_type=F32)
    t_lo = jnp.dot(a_lo, router_ref[:, :LANES], preferred_element_type=F32)
    logits = t_hi[:, :LANES] + t_hi[:, LANES:] + t_lo
    lane = lax.broadcasted_iota(I32, logits.shape, 1)
    lane_f = lane.astype(F32)
    logits = jnp.where(lane < N_EXPERTS, logits, -jnp.inf)
    m1 = jnp.max(logits, axis=-1, keepdims=True)
    i1 = jnp.min(jnp.where(logits == m1, lane_f, float(LANES)), axis=-1, keepdims=True)
    rest = jnp.where(lane_f == i1, -jnp.inf, logits)
    m2 = jnp.max(rest, axis=-1, keepdims=True)
    i2 = jnp.min(jnp.where(rest == m2, lane_f, float(LANES)), axis=-1, keepdims=True)
    e = jnp.exp(m2 - m1)
    w1 = 1.0 / (1.0 + e)
    w2 = e / (1.0 + e)
    ri_ref[...] = jnp.where(lane == 0, i1, jnp.where(lane == 1, i2, 0.0)).astype(I32)
    rw_ref[...] = jnp.where(lane == 0, w1, jnp.where(lane == 1, w2, 0.0))


def _outproj_kernel(mixl_ref, mixc_ref, *rest, moe, n_lat, has_ctx, two_streams):
    if two_streams:
        hl_ref, hc_ref, mod_ref, w_ref, lng_ref, lnb_ref, *rest = rest
    else:
        hl_ref, mod_ref, w_ref, lng_ref, lnb_ref, *rest = rest
    if moe:
        router_ref, h1_ref, a2_ref, ri_ref, rw_ref = rest
    else:
        h1_ref, a2_ref = rest
    split = ROW_SPLIT_ROUTER if moe else ROW_SPLIT
    for r0 in range(0, TM, split):
        rs = slice(r0, r0 + split)
        mix = mixl_ref[rs, :]
        if has_ctx:
            mix = jnp.where(pl.program_id(0) >= n_lat, mixc_ref[rs, :], mix)
        y = jnp.dot(mix, w_ref[...], preferred_element_type=F32)
        h = hl_ref[rs, :]
        if two_streams:
            h = jnp.where(pl.program_id(0) >= n_lat, hc_ref[rs, :], h)
        h1 = _post_norm(h, y, mod_ref[2:3, :], lng_ref[0:1, :], lnb_ref[0:1, :])
        h1_ref[rs, :] = h1
        a2 = _layernorm(h1) * (1.0 + mod_ref[4:5, :]) + mod_ref[3:4, :]
        if moe:
            _store_row_tiles(a2_ref.at[pl.ds(r0 * ROW_TILE, split * ROW_TILE)], a2)
            _route_top2(a2, router_ref, ri_ref.at[pl.ds(r0, split)], rw_ref.at[pl.ds(r0, split)])
        else:
            a2_ref[rs, :] = a2.astype(BF16)


def _outproj(mix_lat, mix_ctx, h, mods, w_out, ln_g, ln_b, router_pad, layer, n_rows):
    n_tiles = n_rows // TM
    tiles_per_batch = SEQ // TM
    n_lat = NL // TM
    moe = router_pad is not None
    has_ctx = n_rows > NL
    in_specs = [
        pl.BlockSpec((TM, D), lambda i: (jnp.minimum(i, n_lat - 1), 0)),
        pl.BlockSpec((TM, D), lambda i: (jnp.maximum(i - n_lat, 0), 0)),
    ] + _stream_specs(h, n_lat) + [
        pl.BlockSpec((None, None, 6, D),
                     lambda i: (layer, _mod_row(i, tiles_per_batch, n_lat), 0, 0)),
        pl.BlockSpec((None, D, D), lambda i: (layer, 0, 0)),
        pl.BlockSpec((None, 2, D), lambda i: (layer, 0, 0)),
        pl.BlockSpec((None, 2, D), lambda i: (layer, 0, 0)),
    ]
    out_specs = [pl.BlockSpec((TM, D), lambda i: (i, 0)), pl.BlockSpec((TM, D), lambda i: (i, 0))]
    out_shape = [jax.ShapeDtypeStruct((n_rows, D), F32), jax.ShapeDtypeStruct((n_rows, D), BF16)]
    args = [mix_lat, mix_ctx, *h, mods, w_out, ln_g, ln_b]
    if moe:
        in_specs.append(pl.BlockSpec((None, D, 2 * LANES), lambda i: (layer // 2, 0, 0)))
        out_specs[1] = pl.BlockSpec((TM * ROW_TILE, LANES), lambda i: (i, 0))
        out_shape[1] = jax.ShapeDtypeStruct((n_rows * ROW_TILE, LANES), F32)
        out_specs += [pl.BlockSpec((TM, LANES), lambda i: (i, 0)),
                      pl.BlockSpec((TM, LANES), lambda i: (i, 0))]
        out_shape += [jax.ShapeDtypeStruct((n_rows, LANES), I32),
                      jax.ShapeDtypeStruct((n_rows, LANES), F32)]
        args.append(router_pad)
    return pl.pallas_call(
        functools.partial(_outproj_kernel, moe=moe, n_lat=n_lat, has_ctx=has_ctx,
                          two_streams=len(h) == 2),
        grid=(n_tiles,),
        in_specs=in_specs,
        out_specs=out_specs,
        out_shape=out_shape,
        compiler_params=_cparams(1),
        name="outproj_moe" if moe else "outproj",
    )(*args)


def _swiglu_hidden(x, w1_ref, w3_ref, t_ref, d_ff):
    for c0 in range(0, d_ff, FF_CHUNK):
        c1 = min(c0 + FF_CHUNK, d_ff)
        g = jnp.dot(x, w1_ref[:, c0:c1], preferred_element_type=F32)
        u = jnp.dot(x, w3_ref[:, c0:c1], preferred_element_type=F32)
        t_ref[:, c0:c1] = (jax.nn.silu(g) * u).astype(BF16)


def _ffn_dense_kernel(a_ref, h_ref, mod_ref, w1_ref, w3_ref, w2_ref, lng_ref, lnb_ref, o_ref,
                      t_ref):
    _swiglu_hidden(a_ref[...], w1_ref, w3_ref, t_ref, D_FF_DENSE)
    f = jnp.dot(t_ref[...], w2_ref[...], preferred_element_type=F32)
    o_ref[...] = _post_norm(h_ref[...], f, mod_ref[5:6, :], lng_ref[1:2, :], lnb_ref[1:2, :])


def _ffn_dense(a2, h1, mods, w1, w3, w2, ln_g, ln_b, layer, n_rows):
    n_tiles = n_rows // TM_FFN
    tiles_per_batch = SEQ // TM_FFN
    n_lat = NL // TM_FFN
    idx = layer // 2
    resident = pl.Buffered(1)
    return pl.pallas_call(
        _ffn_dense_kernel,
        grid=(n_tiles,),
        in_specs=[
            pl.BlockSpec((TM_FFN, D), lambda i: (i, 0)),
            pl.BlockSpec((TM_FFN, D), lambda i: (i, 0)),
            pl.BlockSpec((None, None, 6, D),
                         lambda i: (layer, _mod_row(i, tiles_per_batch, n_lat), 0, 0)),
            pl.BlockSpec((None, D, D_FF_DENSE), lambda i: (idx, 0, 0), pipeline_mode=resident),
            pl.BlockSpec((None, D, D_FF_DENSE), lambda i: (idx, 0, 0), pipeline_mode=resident),
            pl.BlockSpec((None, D_FF_DENSE, D), lambda i: (idx, 0, 0), pipeline_mode=resident),
            pl.BlockSpec((None, 2, D), lambda i: (layer, 0, 0)),
            pl.BlockSpec((None, 2, D), lambda i: (layer, 0, 0)),
        ],
        out_specs=pl.BlockSpec((TM_FFN, D), lambda i: (i, 0)),
        out_shape=jax.ShapeDtypeStruct((n_rows, D), F32),
        scratch_shapes=[pltpu.VMEM((TM_FFN, D_FF_DENSE), BF16)],
        compiler_params=_cparams(1),
        name="ffn_dense",
    )(a2, h1, mods, w1, w3, w2, ln_g, ln_b)


def _expert_first_tile(e, idx, x, issue_rows, w1_hbm, w3_hbm, w2_hbm, w1_ref, w3_ref, w2_ref,
                       st_col, st_row, wsem, t_ref, o_ref):
    n_c = D_FF_EXPERT // W_CHUNK
    jobs = []
    for c in range(n_c):
        jobs += [(True, w1_hbm, w1_ref, c), (True, w3_hbm, w3_ref, c)]
    jobs += [(False, w2_hbm, w2_ref, c) for c in range(n_c)]
    n_col = 2 * n_c

    def ring_slot(j):
        return j % W_RING if j < n_col else (j - n_col) % W_RING

    def copy(j):
        by_col, src, _, c = jobs[j]
        s = ring_slot(j)
        if by_col:
            return pltpu.make_async_copy(src.at[idx, e, :, pl.ds(c * W_CHUNK, W_CHUNK)],
                                         st_col.at[s], wsem.at[s])
        return pltpu.make_async_copy(src.at[idx, e, pl.ds(c * W_CHUNK, W_CHUNK), :],
                                     st_row.at[s], wsem.at[W_RING + s])

    def consume(j):
        if j + W_RING - 1 < len(jobs):
            copy(j + W_RING - 1).start()
        copy(j).wait()
        by_col, _, dst, c = jobs[j]
        if by_col:
            dst[:, c * W_CHUNK:(c + 1) * W_CHUNK] = st_col[ring_slot(j)].astype(BF16)
        else:
            dst[c * W_CHUNK:(c + 1) * W_CHUNK, :] = st_row[ring_slot(j)].astype(BF16)

    for j in range(W_RING - 1):
        copy(j).start()
    for c in range(n_c):
        issue_rows(c * MOE_TM // n_c, (c + 1) * MOE_TM // n_c)
        consume(2 * c)
        consume(2 * c + 1)
        cs = slice(c * W_CHUNK, (c + 1) * W_CHUNK)
        g = jnp.dot(x, w1_ref[:, cs], preferred_element_type=F32)
        u = jnp.dot(x, w3_ref[:, cs], preferred_element_type=F32)
        t_ref[:, cs] = (jax.nn.silu(g) * u).astype(BF16)
    y = None
    for c0 in range(0, n_c, 2):
        consume(n_col + c0)
        consume(n_col + c0 + 1)
        ks = slice(c0 * W_CHUNK, (c0 + 2) * W_CHUNK)
        part = jnp.dot(t_ref[:, ks], w2_ref[ks, :], preferred_element_type=F32)
        y = part if y is None else y + part
    _store_row_tiles(o_ref, y)


def _experts_kernel(pos_ref, te_ref, tv_ref, cnt_ref, off_ref, end_ref,
                    a_hbm, w1_hbm, w3_hbm, w2_hbm, o_ref, src_ref, xg_ref, t_ref,
                    w1_ref, w3_ref, w2_ref, st_col, st_row, sem, wsem,
                    *, n_assign, n_slots, idx):
    i = pl.program_id(0)
    n = pl.num_programs(0)

    def start_row(tile, slot, r):
        s = pl.multiple_of(src_ref[tile * MOE_TM + r] * ROW_TILE, ROW_TILE)
        d = r * ROW_TILE if isinstance(r, int) else pl.multiple_of(r * ROW_TILE, ROW_TILE)
        pltpu.make_async_copy(a_hbm.at[pl.ds(s, ROW_TILE)],
                              xg_ref.at[slot, pl.ds(d, ROW_TILE)], sem.at[slot]).start()

    def wait_tile(slot):
        pltpu.make_async_copy(a_hbm.at[pl.ds(0, MOE_TM * ROW_TILE)], xg_ref.at[slot],
                              sem.at[slot]).wait()

    def clear(s, carry):
        src_ref[s] = 0
        return carry

    @pl.when(i == 0)
    def _():
        for e in range(N_EXPERTS):
            lax.fori_loop(off_ref[e] + cnt_ref[e], end_ref[e], clear, 0)
        lax.fori_loop(end_ref[N_EXPERTS - 1], n_slots, clear, 0)

        def fill(a, carry):
            src_ref[pos_ref[a]] = lax.shift_right_logical(a, 1)
            return carry

        lax.fori_loop(0, n_assign, fill, 0, unroll=16)

        def first(r, carry):
            start_row(0, 0, r)
            return carry

        lax.fori_loop(0, MOE_TM, first, 0, unroll=8)

    valid = tv_ref[i] > 0
    slot = i % 2
    prev = jnp.maximum(i - 1, 0)

    @pl.when(jnp.logical_or(i == 0, tv_ref[prev] > 0))
    def _():
        wait_tile(slot)

    nxt_tile = lax.rem(i + 1, n)

    def issue_rows(lo, hi):
        for r in range(lo, hi):
            start_row(nxt_tile, 1 - slot, r)

    group_start = jnp.logical_or(i == 0, te_ref[i] != te_ref[prev])

    @pl.when(jnp.logical_and(valid, group_start))
    def _():
        x = _load_row_tiles(xg_ref.at[slot], MOE_TM).astype(BF16)
        _expert_first_tile(te_ref[i], idx, x, issue_rows, w1_hbm, w3_hbm, w2_hbm,
                           w1_ref, w3_ref, w2_ref, st_col, st_row, wsem, t_ref, o_ref)

    @pl.when(jnp.logical_and(valid, jnp.logical_not(group_start)))
    def _():
        x = _load_row_tiles(xg_ref.at[slot], MOE_TM).astype(BF16)
        chunks = [(c0, min(c0 + FF_CHUNK, D_FF_EXPERT)) for c0 in range(0, D_FF_EXPERT, FF_CHUNK)]
        per_chunk = MOE_TM // len(chunks)
        for ci, (c0, c1) in enumerate(chunks):
            issue_rows(ci * per_chunk, (ci + 1) * per_chunk)
            g = jnp.dot(x, w1_ref[:, c0:c1], preferred_element_type=F32)
            u = jnp.dot(x, w3_ref[:, c0:c1], preferred_element_type=F32)
            t_ref[:, c0:c1] = (jax.nn.silu(g) * u).astype(BF16)
        y = jnp.dot(t_ref[...], w2_ref[...], preferred_element_type=F32)
        _store_row_tiles(o_ref, y)

    @pl.when(jnp.logical_not(valid))
    def _():
        o_ref[...] = jnp.zeros_like(o_ref)

    @pl.when(jnp.logical_and(valid, i == n - 1))
    def _():
        wait_tile(1 - slot)


def _experts(plan, a2r, w1, w3, w2, idx, n_rows, n_slots):
    n_tiles = n_slots // MOE_TM
    return pl.pallas_call(
        functools.partial(_experts_kernel, n_assign=2 * n_rows, n_slots=n_slots, idx=idx),
        grid_spec=pltpu.PrefetchScalarGridSpec(
            num_scalar_prefetch=6,
            grid=(n_tiles,),
            in_specs=[pl.BlockSpec(memory_space=pl.ANY)] * 4,
            out_specs=pl.BlockSpec((MOE_TM * ROW_TILE, LANES), lambda i, *_: (i, 0)),
            scratch_shapes=[
                pltpu.SMEM((n_slots,), I32),
                pltpu.VMEM((2, MOE_TM * ROW_TILE, LANES), F32),
                pltpu.VMEM((MOE_TM, D_FF_EXPERT), BF16),
                pltpu.VMEM((D, D_FF_EXPERT), BF16),
                pltpu.VMEM((D, D_FF_EXPERT), BF16),
                pltpu.VMEM((D_FF_EXPERT, D), BF16),
                pltpu.VMEM((W_RING, D, W_CHUNK), F32),
                pltpu.VMEM((W_RING, W_CHUNK, D), F32),
                pltpu.SemaphoreType.DMA((2,)),
                pltpu.SemaphoreType.DMA((2 * W_RING,)),
            ],
        ),
        out_shape=jax.ShapeDtypeStruct((n_slots * ROW_TILE, LANES), F32),
        compiler_params=_cparams(1, VMEM_LIMIT_EXPERTS),
        name="moe_experts",
    )(*plan, a2r, w1, w3, w2)


def _combine_kernel(pos_ref, y_hbm, h_ref, rw_ref, mod_ref, lng_ref, lnb_ref, o_ref, buf, sem):
    i = pl.program_id(0)
    n = pl.num_programs(0)

    def start_row(step, slot, r):
        d = r * ROW_TILE if isinstance(r, int) else pl.multiple_of(r * ROW_TILE, ROW_TILE)
        for k in range(2):
            p = pl.multiple_of(pos_ref[2 * (step * COMBINE_ROWS + r) + k] * ROW_TILE, ROW_TILE)
            pltpu.make_async_copy(y_hbm.at[pl.ds(p, ROW_TILE)],
                                  buf.at[slot, k, pl.ds(d, ROW_TILE)], sem.at[slot]).start()

    def wait_tile(slot):
        for k in range(2):
            pltpu.make_async_copy(y_hbm.at[pl.ds(0, COMBINE_ROWS * ROW_TILE)], buf.at[slot, k],
                                  sem.at[slot]).wait()

    ahead = COMBINE_SLOTS - 1

    @pl.when(i == 0)
    def _():
        for t in range(ahead):
            def first(r, carry):
                start_row(t, t, r)
                return carry

            lax.fori_loop(0, COMBINE_ROWS, first, 0, unroll=4)

    slot = lax.rem(i, COMBINE_SLOTS)
    wait_tile(slot)
    nxt = lax.rem(i + ahead, n)
    nxt_slot = lax.rem(i + ahead, COMBINE_SLOTS)
    chunk = COMBINE_ROWS // COMBINE_CHUNKS
    for c in range(COMBINE_CHUNKS):
        for r in range(c * chunk, (c + 1) * chunk):
            start_row(nxt, nxt_slot, r)
        rs = slice(c * chunk, (c + 1) * chunk)
        tiles = pl.ds(c * chunk * ROW_TILE, chunk * ROW_TILE)
        w = rw_ref[rs, :]
        f = (w[:, 0:1] * _load_row_tiles(buf.at[slot, 0, tiles], chunk)
             + w[:, 1:2] * _load_row_tiles(buf.at[slot, 1, tiles], chunk))
        o_ref[rs, :] = _post_norm(h_ref[rs, :], f, mod_ref[5:6, :], lng_ref[1:2, :],
                                  lnb_ref[1:2, :])

    @pl.when(i == n - 1)
    def _():
        for t in range(1, COMBINE_SLOTS):
            wait_tile(lax.rem(i + t, COMBINE_SLOTS))


def _combine(pos, y, h1, rw, mods, ln_g, ln_b, layer, n_rows):
    n_tiles = n_rows // COMBINE_ROWS
    tiles_per_batch = SEQ // COMBINE_ROWS
    n_lat = NL // COMBINE_ROWS
    return pl.pallas_call(
        _combine_kernel,
        grid_spec=pltpu.PrefetchScalarGridSpec(
            num_scalar_prefetch=1,
            grid=(n_tiles,),
            in_specs=[
                pl.BlockSpec(memory_space=pl.ANY),
                pl.BlockSpec((COMBINE_ROWS, D), lambda i, p: (i, 0)),
                pl.BlockSpec((COMBINE_ROWS, LANES), lambda i, p: (i, 0)),
                pl.BlockSpec((None, None, 6, D),
                             lambda i, p: (layer, _mod_row(i, tiles_per_batch, n_lat), 0, 0)),
                pl.BlockSpec((None, 2, D), lambda i, p: (layer, 0, 0)),
                pl.BlockSpec((None, 2, D), lambda i, p: (layer, 0, 0)),
            ],
            out_specs=pl.BlockSpec((COMBINE_ROWS, D), lambda i, p: (i, 0)),
            scratch_shapes=[pltpu.VMEM((COMBINE_SLOTS, 2, COMBINE_ROWS * ROW_TILE, LANES), F32),
                            pltpu.SemaphoreType.DMA((COMBINE_SLOTS,))],
        ),
        out_shape=jax.ShapeDtypeStruct((n_rows, D), F32),
        compiler_params=_cparams(1),
        name="moe_combine",
    )(pos, y, h1, rw, mods, ln_g, ln_b)


def _route_plan(eid, n_rows, n_slots):
    n_tiles = n_slots // MOE_TM
    e_flat = eid.reshape(-1)
    onehot = (e_flat[:, None] == jnp.arange(N_EXPERTS, dtype=I32)[None, :]).astype(I32)
    csum = jnp.cumsum(onehot, axis=0)
    rank = jnp.sum((csum - onehot) * onehot, axis=1)
    count = csum[-1]
    padded = ((count + MOE_TM - 1) // MOE_TM) * MOE_TM
    ends = jnp.cumsum(padded)
    offs = ends - padded
    pos = jnp.sum(onehot * offs[None, :], axis=1) + rank
    tile_start = jnp.arange(n_tiles, dtype=I32) * MOE_TM
    tile_e = jnp.minimum(jnp.sum((tile_start[:, None] >= ends[None, :]).astype(I32), axis=1),
                         N_EXPERTS - 1)
    tile_v = (tile_start < ends[-1]).astype(I32)
    return (pos.astype(I32), tile_e.astype(I32), tile_v, count.astype(I32), offs.astype(I32),
            ends.astype(I32))


def _moe(a2r, h1, ri, rw, mods, w1, w3, w2, ln_g, ln_b, layer, n_rows):
    n_slots = 2 * n_rows + N_EXPERTS * MOE_TM
    plan = _route_plan(ri[:, :2], n_rows, n_slots)
    ys = _experts(plan, a2r, w1, w3, w2, layer // 2, n_rows, n_slots)
    return _combine(plan[0], ys, h1, rw, mods, ln_g, ln_b, layer, n_rows)


def kernel(x, c, ctx, c_ctx, w_ada, b_ada, w_in, w_pool, pool_scale, sink, w_out, ln_g, ln_b,
           dense_w1, dense_w3, dense_w2, router, moe_w1, moe_w3, moe_w2):
    cond = jnp.concatenate([c, c_ctx[None, :], jnp.zeros((MOD_ROWS - B - 1, D), F32)], axis=0)
    mods = _ada_tables(cond, w_ada, b_ada)
    cos_t, sin_t = _rope_tables()

    w_in_b = w_in.astype(BF16)
    w_pool_b = w_pool.astype(BF16)
    w_out_b = w_out.astype(BF16)
    dw1, dw3, dw2 = dense_w1.astype(BF16), dense_w3.astype(BF16), dense_w2.astype(BF16)
    r_hi = router.astype(BF16)
    r_lo = (router - r_hi.astype(F32)).astype(BF16)
    lane_pad = ((0, 0), (0, 0), (0, LANES - N_EXPERTS))
    router_pad = jnp.concatenate([jnp.pad(r_hi, lane_pad), jnp.pad(r_lo, lane_pad)], axis=-1)
    pool_scale3 = pool_scale.reshape(DEPTH, 1, POOL_W)

    h = (x.reshape(NL, D), ctx.reshape(NC, D))
    for l in range(DEPTH):
        last = l == DEPTH - 1
        n_rows = NL if last else NT
        q, kk, vv, u = _inproj(h, mods, w_in_b, cos_t, sin_t, l)
        mix = _mixer_latent(q, kk, vv, u, sink[l], w_pool_b, pool_scale3, l)
        mix_c = mix if last else _mixer_context(q, kk, vv, u, sink[l], w_pool_b, pool_scale3, l)
        if l % 2 == 0:
            h1, a2 = _outproj(mix, mix_c, h, mods, w_out_b, ln_g, ln_b, None, l, n_rows)
            h = (_ffn_dense(a2, h1, mods, dw1, dw3, dw2, ln_g, ln_b, l, n_rows),)
        else:
            h1, a2, ri, rw = _outproj(mix, mix_c, h, mods, w_out_b, ln_g, ln_b, router_pad, l,
                                      n_rows)
            h = (_moe(a2, h1, ri, rw, mods, moe_w1, moe_w3, moe_w2, ln_g, ln_b, l, n_rows),)
    return h[0].reshape(B, SEQ, D)
```

```python
import functools

import jax
import jax.numpy as jnp
from jax import lax
from jax.experimental import pallas as pl
from jax.experimental.pallas import tpu as pltpu

F32 = jnp.float32
BF16 = jnp.bfloat16
I32 = jnp.int32

D = 1024
B = 8
SEQ = 2048
DEPTH = 4
CTX = 256
GRID_W = 64
HEAD_DIM = 64
N_Q_HEADS = 8
N_KV_HEADS = 2
ATTN_W = N_Q_HEADS * HEAD_DIM
KV_W = N_KV_HEADS * HEAD_DIM
POOL_WINDOWS = (2, 4, 8, 16)
POOL_W = D - ATTN_W
POOL_G = POOL_W // len(POOL_WINDOWS)
IN_W = ATTN_W + 2 * KV_W + POOL_W
WINDOW = 128
ROPE_BASE = 10000.0
ROPE_FREQS = HEAD_DIM // 4
D_FF_DENSE = 2816
N_EXPERTS = 8
D_FF_EXPERT = 3584
ALPHA = (2 * DEPTH) ** 0.25
LN_EPS = 1e-6
NEG_INF = -1e30
LOG2E = 1.4426950408889634

NL = B * SEQ
NC = B * CTX
NT = NL + NC
MOD_ROWS = 16
LANES = 128

TM = 1024
TM_FFN = 512
ROW_SPLIT_ROUTER = 512
TQ = 128
TQ_STEP = 1024
POOL_HALO = 8
ROW_SPLIT = 256
MOE_TM = 512
FF_CHUNK = 1024
W_CHUNK = 256
W_RING = 5
COMBINE_ROWS = 256
COMBINE_CHUNKS = 4
COMBINE_SLOTS = 3

VMEM_LIMIT = 56 * 1024 * 1024
VMEM_LIMIT_EXPERTS = 60 * 1024 * 1024


def _cparams(n_axes, vmem=VMEM_LIMIT):
    return pltpu.CompilerParams(
        dimension_semantics=("arbitrary",) * n_axes, vmem_limit_bytes=vmem)


def _layernorm(x):
    mu = jnp.mean(x, axis=-1, keepdims=True)
    xc = x - mu
    var = jnp.mean(xc * xc, axis=-1, keepdims=True)
    return xc * lax.rsqrt(var + LN_EPS)


def _mod_row(tile, tiles_per_batch, n_latent_tiles):
    return jnp.where(tile < n_latent_tiles, tile // tiles_per_batch, B)


def _stream_specs(h, n_lat):
    if len(h) == 1:
        return [pl.BlockSpec((TM, D), lambda i: (i, 0))]
    return [pl.BlockSpec((TM, D), lambda i: (jnp.minimum(i, n_lat - 1), 0)),
            pl.BlockSpec((TM, D), lambda i: (jnp.maximum(i - n_lat, 0), 0))]


ADA_TN = 1536


def _ada_kernel(s_ref, w_ref, b_ref, o_ref):
    s = jax.nn.silu(s_ref[...]).astype(BF16)
    w = w_ref[...].astype(BF16)
    o_ref[...] = jnp.dot(s, w, preferred_element_type=F32) + b_ref[...]


def _ada_tables(cond, w_ada, b_ada):
    n_col = (6 * D) // ADA_TN
    out = pl.pallas_call(
        _ada_kernel,
        grid=(DEPTH, n_col),
        in_specs=[
            pl.BlockSpec((MOD_ROWS, D), lambda l, j: (0, 0)),
            pl.BlockSpec((None, D, ADA_TN), lambda l, j: (l, 0, j)),
            pl.BlockSpec((None, 1, ADA_TN), lambda l, j: (l, 0, j)),
        ],
        out_specs=pl.BlockSpec((None, MOD_ROWS, ADA_TN), lambda l, j: (l, 0, j)),
        out_shape=jax.ShapeDtypeStruct((DEPTH, MOD_ROWS, 6 * D), F32),
        compiler_params=_cparams(2),
        name="ada_tables",
    )(cond, w_ada, b_ada.reshape(DEPTH, 1, 6 * D))
    return out.reshape(DEPTH, MOD_ROWS, 6, D)


def _inproj_kernel(*refs, n_lat, two_streams):
    if two_streams:
        hl_ref, hc_ref, mod_ref, w_ref, cos_ref, sin_ref, q_ref, kk_ref, vv_ref, u_ref = refs
    else:
        hl_ref, mod_ref, w_ref, cos_ref, sin_ref, q_ref, kk_ref, vv_ref, u_ref = refs
    lane = lax.broadcasted_iota(I32, (1, LANES), 1)
    first_half = (lane % (2 * ROPE_FREQS)) < ROPE_FREQS
    low = lane < HEAD_DIM
    for r0 in range(0, TM, ROW_SPLIT):
        rs = slice(r0, r0 + ROW_SPLIT)
        h = hl_ref[rs, :]
        if two_streams:
            h = jnp.where(pl.program_id(0) >= n_lat, hc_ref[rs, :], h)
        a = _layernorm(h) * (1.0 + mod_ref[1:2, :]) + mod_ref[0:1, :]
        p = jnp.dot(a.astype(BF16), w_ref[...], preferred_element_type=F32)
        cos = cos_ref[rs, :]
        sin = sin_ref[rs, :]

        def rope(x):
            swapped = jnp.where(first_half, pltpu.roll(x, LANES - ROPE_FREQS, 1),
                                pltpu.roll(x, ROPE_FREQS, 1))
            return x * cos + swapped * sin

        for c in range(ATTN_W // LANES):
            qc = rope(p[:, c * LANES:(c + 1) * LANES]) * (HEAD_DIM ** -0.5 * LOG2E)
            q_ref[rs, c * LANES:(c + 1) * LANES] = qc.astype(BF16)
        k = rope(p[:, ATTN_W:ATTN_W + KV_W])
        k_sw = pltpu.roll(k, HEAD_DIM, 1)
        kk_ref[rs, 0:LANES] = jnp.where(low, k, k_sw).astype(BF16)
        kk_ref[rs, LANES:2 * LANES] = jnp.where(low, k_sw, k).astype(BF16)
        v = p[:, ATTN_W + KV_W:ATTN_W + 2 * KV_W]
        v_sw = pltpu.roll(v, HEAD_DIM, 1)
        vv_ref[rs, 0:LANES] = jnp.where(low, v, v_sw).astype(BF16)
        vv_ref[rs, LANES:2 * LANES] = jnp.where(low, v_sw, v).astype(BF16)
        u_ref[rs, :] = p[:, ATTN_W + 2 * KV_W:]


def _inproj(h, mods, w_in, cos_t, sin_t, layer):
    n_tiles = NT // TM
    tiles_per_batch = SEQ // TM
    n_lat = NL // TM

    def rope_idx(i):
        return (jnp.where(i < n_lat, i % tiles_per_batch, tiles_per_batch), 0)

    return pl.pallas_call(
        functools.partial(_inproj_kernel, n_lat=n_lat, two_streams=len(h) == 2),
        grid=(n_tiles,),
        in_specs=_stream_specs(h, n_lat) + [
            pl.BlockSpec((None, None, 6, D),
                         lambda i: (layer, _mod_row(i, tiles_per_batch, n_lat), 0, 0)),
            pl.BlockSpec((None, D, IN_W), lambda i: (layer, 0, 0)),
            pl.BlockSpec((TM, LANES), rope_idx),
            pl.BlockSpec((TM, LANES), rope_idx),
        ],
        out_specs=[
            pl.BlockSpec((TM, ATTN_W), lambda i: (i, 0)),
            pl.BlockSpec((TM, 2 * KV_W), lambda i: (i, 0)),
            pl.BlockSpec((TM, 2 * KV_W), lambda i: (i, 0)),
            pl.BlockSpec((TM, POOL_W), lambda i: (i, 0)),
        ],
        out_shape=[
            jax.ShapeDtypeStruct((NT, ATTN_W), BF16),
            jax.ShapeDtypeStruct((NT, 2 * KV_W), BF16),
            jax.ShapeDtypeStruct((NT, 2 * KV_W), BF16),
            jax.ShapeDtypeStruct((NT, POOL_W), F32),
        ],
        compiler_params=_cparams(1),
        name="inproj",
    )(*h, mods, w_in, cos_t, sin_t)


def _rope_tables():
    rows = SEQ // GRID_W
    row = jnp.repeat(jnp.arange(rows, dtype=F32), GRID_W)
    col = jnp.tile(jnp.arange(GRID_W, dtype=F32), rows)
    inv = ROPE_BASE ** (-jnp.arange(ROPE_FREQS, dtype=F32) / ROPE_FREQS)
    ang_r = row[:, None] * inv[None, :]
    ang_c = col[:, None] * inv[None, :]
    cr, sr, cc, sc = jnp.cos(ang_r), jnp.sin(ang_r), jnp.cos(ang_c), jnp.sin(ang_c)
    cos_h = jnp.concatenate([cr, cr, cc, cc], axis=-1)
    sin_h = jnp.concatenate([-sr, sr, -sc, sc], axis=-1)
    cos_t = jnp.concatenate([jnp.tile(cos_h, (1, 2)), jnp.ones((TM, LANES), F32)], axis=0)
    sin_t = jnp.concatenate([jnp.tile(sin_h, (1, 2)), jnp.zeros((TM, LANES), F32)], axis=0)
    return cos_t, sin_t


def _pool_group(top, mid, bot, t0, seq_len, width):
    rows = mid.shape[0]
    slab = jnp.concatenate([top, mid, bot], axis=0)
    half = width // 2
    s = slab
    span = 1
    while span < width:
        n = s.shape[0] - span
        s = s[0:n] + s[span:span + n]
        span *= 2
    start = POOL_HALO - half
    total = s[start:start + rows]
    pos = t0 + lax.broadcasted_iota(I32, (rows, 1), 0)
    hi = jnp.minimum(pos + half, seq_len)
    lo = jnp.maximum(pos - half, 0)
    cnt = (hi - lo).astype(F32)
    return total / cnt - mid


def _mixer_kernel(sink_ref, q_ref, kk_ref, vv_ref, kkc_ref, vvc_ref, u_ref, wp_ref, ps_ref,
                  o_ref, *, tq, sub, seq_len, band):
    for sb in range(tq // sub):
        _mixer_block(sink_ref, q_ref, kk_ref, vv_ref, kkc_ref, vvc_ref, u_ref, wp_ref, ps_ref,
                     o_ref, t0=pl.program_id(1) * tq + sb * sub, r0=sb * sub, tq=sub,
                     seq_len=seq_len, band=band)


def _mixer_block(sink_ref, q_ref, kk_ref, vv_ref, kkc_ref, vvc_ref, u_ref, wp_ref, ps_ref,
                 o_ref, *, t0, r0, tq, seq_len, band):
    lane = lax.broadcasted_iota(I32, (1, LANES), 1)
    low = lane < HEAD_DIM

    if band:
        n_band = tq + 2 * WINDOW
        ks = jnp.clip(t0 - WINDOW, 0, seq_len - n_band)
        ks = pl.multiple_of(ks, LANES)
        kk = jnp.concatenate([kk_ref[pl.ds(ks, n_band), :], kkc_ref[...]], axis=0)
        vv = jnp.concatenate([vv_ref[pl.ds(ks, n_band), :], vvc_ref[...]], axis=0)
        qpos = t0 + lax.broadcasted_iota(I32, (tq, n_band), 0)
        kpos = ks + lax.broadcasted_iota(I32, (tq, n_band), 1)
        bias = jnp.where(jnp.abs(kpos - qpos) <= WINDOW, 0.0, NEG_INF)
    else:
        n_band = 0
        kk = kkc_ref[...]
        vv = vvc_ref[...]
        bias = None
    nk = kk.shape[0]
    zero = jnp.zeros_like(kk[:, 0:LANES])
    low_f = jnp.broadcast_to(jnp.where(low, 1.0, 0.0), (nk, LANES))
    ones_st = jnp.concatenate([low_f, 1.0 - low_f], axis=0).astype(BF16)

    for g in range(N_KV_HEADS):
        kg = kk[:, g * LANES:(g + 1) * LANES]
        vg = vv[:, g * LANES:(g + 1) * LANES]
        k_st = jnp.concatenate([jnp.where(low, kg, zero), jnp.where(low, zero, kg)], axis=0)
        v_st = jnp.concatenate([jnp.where(low, vg, zero), jnp.where(low, zero, vg)], axis=0)
        v_st = jnp.concatenate([v_st, ones_st], axis=1)
        for pr in range(2):
            j = 2 * g + pr
            qp = q_ref[r0:r0 + tq, j * LANES:(j + 1) * LANES]
            s = lax.dot_general(qp, k_st, (((1,), (1,)), ((), ())), preferred_element_type=F32)
            ps = []
            sink_terms = []
            for hh in range(2):
                sink = sink_ref[2 * j + hh] * LOG2E
                sh = s[:, hh * nk:(hh + 1) * nk]
                if bias is not None:
                    parts = [sh[:, :n_band] + bias, sh[:, n_band:]]
                else:
                    parts = [sh]
                m = sink
                for part in parts:
                    m = jnp.maximum(jnp.max(part, axis=-1, keepdims=True), m)
                sink_terms.append(jnp.exp2(sink - m))
                ps += [jnp.exp2(part - m).astype(BF16) for part in parts]
            od = jnp.dot(jnp.concatenate(ps, axis=1), v_st, preferred_element_type=F32)
            den = od[:, LANES:] + jnp.where(low, sink_terms[0], sink_terms[1])
            o_ref[r0:r0 + tq, j * LANES:(j + 1) * LANES] = (od[:, :LANES] / den).astype(BF16)

    no_halo = jnp.zeros((POOL_HALO, POOL_W), F32)
    if band:
        t0a = pl.multiple_of(t0, POOL_HALO)
        mid = u_ref[pl.ds(t0a, tq), :]
        top_s = pl.multiple_of(jnp.maximum(t0 - POOL_HALO, 0), POOL_HALO)
        bot_s = pl.multiple_of(jnp.minimum(t0 + tq, seq_len - POOL_HALO), POOL_HALO)
        top = jnp.where(t0 > 0, u_ref[pl.ds(top_s, POOL_HALO), :], no_halo)
        bot = jnp.where(t0 + tq < seq_len, u_ref[pl.ds(bot_s, POOL_HALO), :], no_halo)
    else:
        mid = u_ref[...]
        top = no_halo
        bot = no_halo
    for g, width in enumerate(POOL_WINDOWS):
        sl = slice(g * POOL_G, (g + 1) * POOL_G)
        diff = _pool_group(top[:, sl], mid[:, sl], bot[:, sl], t0, seq_len, width)
        mixed = jnp.dot(diff.astype(BF16), wp_ref[g], preferred_element_type=F32)
        mixed = mixed * ps_ref[:, sl]
        o_ref[r0:r0 + tq, ATTN_W + g * POOL_G:ATTN_W + (g + 1) * POOL_G] = mixed.astype(BF16)


def _mixer_latent(q, kk, vv, u, sink_l, w_pool, pool_scale, layer):
    nq = SEQ // TQ_STEP
    ctx_blk = NL // CTX
    kern = functools.partial(_mixer_kernel, tq=TQ_STEP, sub=TQ, seq_len=SEQ, band=True)
    return pl.pallas_call(
        kern,
        grid=(B, nq),
        in_specs=[
            pl.BlockSpec(memory_space=pltpu.SMEM),
            pl.BlockSpec((TQ_STEP, ATTN_W), lambda b, t: (b * nq + t, 0)),
            pl.BlockSpec((SEQ, 2 * KV_W), lambda b, t: (b, 0)),
            pl.BlockSpec((SEQ, 2 * KV_W), lambda b, t: (b, 0)),
            pl.BlockSpec((CTX, 2 * KV_W), lambda b, t: (ctx_blk + b, 0)),
            pl.BlockSpec((CTX, 2 * KV_W), lambda b, t: (ctx_blk + b, 0)),
            pl.BlockSpec((SEQ, POOL_W), lambda b, t: (b, 0)),
            pl.BlockSpec((None, len(POOL_WINDOWS), POOL_G, POOL_G), lambda b, t: (layer, 0, 0, 0)),
            pl.BlockSpec((None, 1, POOL_W), lambda b, t: (layer, 0, 0)),
        ],
        out_specs=pl.BlockSpec((TQ_STEP, D), lambda b, t: (b * nq + t, 0)),
        out_shape=jax.ShapeDtypeStruct((NL, D), BF16),
        compiler_params=_cparams(2),
        name="mixer_latent",
    )(sink_l, q, kk, vv, kk, vv, u, w_pool, pool_scale)


def _mixer_context(q, kk, vv, u, sink_l, w_pool, pool_scale, layer):
    ctx_blk = NL // CTX
    kern = functools.partial(_mixer_kernel, tq=CTX, sub=CTX, seq_len=CTX, band=False)

    def body(sink_ref, q_ref, kkc_ref, vvc_ref, u_ref, wp_ref, ps_ref, o_ref):
        kern(sink_ref, q_ref, None, None, kkc_ref, vvc_ref, u_ref, wp_ref, ps_ref, o_ref)

    return pl.pallas_call(
        body,
        grid=(B, 1),
        in_specs=[
            pl.BlockSpec(memory_space=pltpu.SMEM),
            pl.BlockSpec((CTX, ATTN_W), lambda b, t: (ctx_blk + b, 0)),
            pl.BlockSpec((CTX, 2 * KV_W), lambda b, t: (ctx_blk + b, 0)),
            pl.BlockSpec((CTX, 2 * KV_W), lambda b, t: (ctx_blk + b, 0)),
            pl.BlockSpec((CTX, POOL_W), lambda b, t: (ctx_blk + b, 0)),
            pl.BlockSpec((None, len(POOL_WINDOWS), POOL_G, POOL_G), lambda b, t: (layer, 0, 0, 0)),
            pl.BlockSpec((None, 1, POOL_W), lambda b, t: (layer, 0, 0)),
        ],
        out_specs=pl.BlockSpec((CTX, D), lambda b, t: (b, 0)),
        out_shape=jax.ShapeDtypeStruct((NC, D), BF16),
        compiler_params=_cparams(2),
        name="mixer_context",
    )(sink_l, q, kk, vv, u, w_pool, pool_scale)


def _post_norm(h, upd, gate, g, b):
    return _layernorm(ALPHA * h + gate * upd) * g + b


ROW_TILE = D // LANES


def _store_row_tiles(ref, x):
    rows = x.shape[0]
    for c in range(ROW_TILE):
        ref[pl.ds(c, rows, stride=ROW_TILE), :] = x[:, c * LANES:(c + 1) * LANES]


def _load_row_tiles(ref, rows):
    return jnp.concatenate(
        [ref[pl.ds(c, rows, stride=ROW_TILE), :] for c in range(ROW_TILE)], axis=1)


def _route_top2(a, router_ref, ri_ref, rw_ref):
    a_hi = a.astype(BF16)
    a_lo = (a - a_hi.astype(F32)).astype(BF16)
    t_hi = jnp.dot(a_hi, router_ref[...], preferred_element_type=F32)
    t_lo = jnp.dot(a_lo, router_ref[:, :LANES], preferred_element_type=F32)
    logits = t_hi[:, :LANES] + t_hi[:, LANES:] + t_lo
    lane = lax.broadcasted_iota(I32, logits.shape, 1)
    lane_f = lane.astype(F32)
    logits = jnp.where(lane < N_EXPERTS, logits, -jnp.inf)
    m1 = jnp.max(logits, axis=-1, keepdims=True)
    i1 = jnp.min(jnp.where(logits == m1, lane_f, float(LANES)), axis=-1, keepdims=True)
    rest = jnp.where(lane_f == i1, -jnp.inf, logits)
    m2 = jnp.max(rest, axis=-1, keepdims=True)
    i2 = jnp.min(jnp.where(rest == m2, lane_f, float(LANES)), axis=-1, keepdims=True)
    e = jnp.exp(m2 - m1)
    w1 = 1.0 / (1.0 + e)
    w2 = e / (1.0 + e)
    ri_ref[...] = jnp.where(lane == 0, i1, jnp.where(lane == 1, i2, 0.0)).astype(I32)
    rw_ref[...] = jnp.where(lane == 0, w1, jnp.where(lane == 1, w2, 0.0))


def _outproj_kernel(mixl_ref, mixc_ref, *rest, moe, n_lat, has_ctx, two_streams):
    if two_streams:
        hl_ref, hc_ref, mod_ref, w_ref, lng_ref, lnb_ref, *rest = rest
    else:
        hl_ref, mod_ref, w_ref, lng_ref, lnb_ref, *rest = rest
    if moe:
        router_ref, h1_ref, a2_ref, ri_ref, rw_ref = rest
    else:
        h1_ref, a2_ref = rest
    split = ROW_SPLIT_ROUTER if moe else ROW_SPLIT
    for r0 in range(0, TM, split):
        rs = slice(r0, r0 + split)
        mix = mixl_ref[rs, :]
        if has_ctx:
            mix = jnp.where(pl.program_id(0) >= n_lat, mixc_ref[rs, :], mix)
        y = jnp.dot(mix, w_ref[...], preferred_element_type=F32)
        h = hl_ref[rs, :]
        if two_streams:
            h = jnp.where(pl.program_id(0) >= n_lat, hc_ref[rs, :], h)
        h1 = _post_norm(h, y, mod_ref[2:3, :], lng_ref[0:1, :], lnb_ref[0:1, :])
        h1_ref[rs, :] = h1
        a2 = _layernorm(h1) * (1.0 + mod_ref[4:5, :]) + mod_ref[3:4, :]
        if moe:
            _store_row_tiles(a2_ref.at[pl.ds(r0 * ROW_TILE, split * ROW_TILE)], a2)
            _route_top2(a2, router_ref, ri_ref.at[pl.ds(r0, split)], rw_ref.at[pl.ds(r0, split)])
        else:
            a2_ref[rs, :] = a2.astype(BF16)


def _outproj(mix_lat, mix_ctx, h, mods, w_out, ln_g, ln_b, router_pad, layer, n_rows):
    n_tiles = n_rows // TM
    tiles_per_batch = SEQ // TM
    n_lat = NL // TM
    moe = router_pad is not None
    has_ctx = n_rows > NL
    in_specs = [
        pl.BlockSpec((TM, D), lambda i: (jnp.minimum(i, n_lat - 1), 0)),
        pl.BlockSpec((TM, D), lambda i: (jnp.maximum(i - n_lat, 0), 0)),
    ] + _stream_specs(h, n_lat) + [
        pl.BlockSpec((None, None, 6, D),
                     lambda i: (layer, _mod_row(i, tiles_per_batch, n_lat), 0, 0)),
        pl.BlockSpec((None, D, D), lambda i: (layer, 0, 0)),
        pl.BlockSpec((None, 2, D), lambda i: (layer, 0, 0)),
        pl.BlockSpec((None, 2, D), lambda i: (layer, 0, 0)),
    ]
    out_specs = [pl.BlockSpec((TM, D), lambda i: (i, 0)), pl.BlockSpec((TM, D), lambda i: (i, 0))]
    out_shape = [jax.ShapeDtypeStruct((n_rows, D), F32), jax.ShapeDtypeStruct((n_rows, D), BF16)]
    args = [mix_lat, mix_ctx, *h, mods, w_out, ln_g, ln_b]
    if moe:
        in_specs.append(pl.BlockSpec((None, D, 2 * LANES), lambda i: (layer // 2, 0, 0)))
        out_specs[1] = pl.BlockSpec((TM * ROW_TILE, LANES), lambda i: (i, 0))
        out_shape[1] = jax.ShapeDtypeStruct((n_rows * ROW_TILE, LANES), F32)
        out_specs += [pl.BlockSpec((TM, LANES), lambda i: (i, 0)),
                      pl.BlockSpec((TM, LANES), lambda i: (i, 0))]
        out_shape += [jax.ShapeDtypeStruct((n_rows, LANES), I32),
                      jax.ShapeDtypeStruct((n_rows, LANES), F32)]
        args.append(router_pad)
    return pl.pallas_call(
        functools.partial(_outproj_kernel, moe=moe, n_lat=n_lat, has_ctx=has_ctx,
                          two_streams=len(h) == 2),
        grid=(n_tiles,),
        in_specs=in_specs,
        out_specs=out_specs,
        out_shape=out_shape,
        compiler_params=_cparams(1),
        name="outproj_moe" if moe else "outproj",
    )(*args)


def _swiglu_hidden(x, w1_ref, w3_ref, t_ref, d_ff):
    for c0 in range(0, d_ff, FF_CHUNK):
        c1 = min(c0 + FF_CHUNK, d_ff)
        g = jnp.dot(x, w1_ref[:, c0:c1], preferred_element_type=F32)
        u = jnp.dot(x, w3_ref[:, c0:c1], preferred_element_type=F32)
        t_ref[:, c0:c1] = (jax.nn.silu(g) * u).astype(BF16)


def _ffn_dense_kernel(a_ref, h_ref, mod_ref, w1_ref, w3_ref, w2_ref, lng_ref, lnb_ref, o_ref,
                      t_ref):
    _swiglu_hidden(a_ref[...], w1_ref, w3_ref, t_ref, D_FF_DENSE)
    f = jnp.dot(t_ref[...], w2_ref[...], preferred_element_type=F32)
    o_ref[...] = _post_norm(h_ref[...], f, mod_ref[5:6, :], lng_ref[1:2, :], lnb_ref[1:2, :])


def _ffn_dense(a2, h1, mods, w1, w3, w2, ln_g, ln_b, layer, n_rows):
    n_tiles = n_rows // TM_FFN
    tiles_per_batch = SEQ // TM_FFN
    n_lat = NL // TM_FFN
    idx = layer // 2
    resident = pl.Buffered(1)
    return pl.pallas_call(
        _ffn_dense_kernel,
        grid=(n_tiles,),
        in_specs=[
            pl.BlockSpec((TM_FFN, D), lambda i: (i, 0)),
            pl.BlockSpec((TM_FFN, D), lambda i: (i, 0)),
            pl.BlockSpec((None, None, 6, D),
                         lambda i: (layer, _mod_row(i, tiles_per_batch, n_lat), 0, 0)),
            pl.BlockSpec((None, D, D_FF_DENSE), lambda i: (idx, 0, 0), pipeline_mode=resident),
            pl.BlockSpec((None, D, D_FF_DENSE), lambda i: (idx, 0, 0), pipeline_mode=resident),
            pl.BlockSpec((None, D_FF_DENSE, D), lambda i: (idx, 0, 0), pipeline_mode=resident),
            pl.BlockSpec((None, 2, D), lambda i: (layer, 0, 0)),
            pl.BlockSpec((None, 2, D), lambda i: (layer, 0, 0)),
        ],
        out_specs=pl.BlockSpec((TM_FFN, D), lambda i: (i, 0)),
        out_shape=jax.ShapeDtypeStruct((n_rows, D), F32),
        scratch_shapes=[pltpu.VMEM((TM_FFN, D_FF_DENSE), BF16)],
        compiler_params=_cparams(1),
        name="ffn_dense",
    )(a2, h1, mods, w1, w3, w2, ln_g, ln_b)


def _expert_first_tile(e, idx, x, issue_rows, w1_hbm, w3_hbm, w2_hbm, w1_ref, w3_ref, w2_ref,
                       st_col, st_row, wsem, t_ref, o_ref):
    n_c = D_FF_EXPERT // W_CHUNK
    jobs = []
    for c in range(n_c):
        jobs += [(True, w1_hbm, w1_ref, c), (True, w3_hbm, w3_ref, c)]
    jobs += [(False, w2_hbm, w2_ref, c) for c in range(n_c)]
    n_col = 2 * n_c

    def ring_slot(j):
        return j % W_RING if j < n_col else (j - n_col) % W_RING

    def copy(j):
        by_col, src, _, c = jobs[j]
        s = ring_slot(j)
        if by_col:
            return pltpu.make_async_copy(src.at[idx, e, :, pl.ds(c * W_CHUNK, W_CHUNK)],
                                         st_col.at[s], wsem.at[s])
        return pltpu.make_async_copy(src.at[idx, e, pl.ds(c * W_CHUNK, W_CHUNK), :],
                                     st_row.at[s], wsem.at[W_RING + s])

    def consume(j):
        if j + W_RING - 1 < len(jobs):
            copy(j + W_RING - 1).start()
        copy(j).wait()
        by_col, _, dst, c = jobs[j]
        if by_col:
            dst[:, c * W_CHUNK:(c + 1) * W_CHUNK] = st_col[ring_slot(j)].astype(BF16)
        else:
            dst[c * W_CHUNK:(c + 1) * W_CHUNK, :] = st_row[ring_slot(j)].astype(BF16)

    for j in range(W_RING - 1):
        copy(j).start()
    for c in range(n_c):
        issue_rows(c * MOE_TM // n_c, (c + 1) * MOE_TM // n_c)
        consume(2 * c)
        consume(2 * c + 1)
        cs = slice(c * W_CHUNK, (c + 1) * W_CHUNK)
        g = jnp.dot(x, w1_ref[:, cs], preferred_element_type=F32)
        u = jnp.dot(x, w3_ref[:, cs], preferred_element_type=F32)
        t_ref[:, cs] = (jax.nn.silu(g) * u).astype(BF16)
    y = None
    for c0 in range(0, n_c, 2):
        consume(n_col + c0)
        consume(n_col + c0 + 1)
        ks = slice(c0 * W_CHUNK, (c0 + 2) * W_CHUNK)
        part = jnp.dot(t_ref[:, ks], w2_ref[ks, :], preferred_element_type=F32)
        y = part if y is None else y + part
    _store_row_tiles(o_ref, y)


def _experts_kernel(pos_ref, te_ref, tv_ref, cnt_ref, off_ref, end_ref,
                    a_hbm, w1_hbm, w3_hbm, w2_hbm, o_ref, src_ref, xg_ref, t_ref,
                    w1_ref, w3_ref, w2_ref, st_col, st_row, sem, wsem,
                    *, n_assign, n_slots, idx):
    i = pl.program_id(0)
    n = pl.num_programs(0)

    def start_row(tile, slot, r):
        s = pl.multiple_of(src_ref[tile * MOE_TM + r] * ROW_TILE, ROW_TILE)
        d = r * ROW_TILE if isinstance(r, int) else pl.multiple_of(r * ROW_TILE, ROW_TILE)
        pltpu.make_async_copy(a_hbm.at[pl.ds(s, ROW_TILE)],
                              xg_ref.at[slot, pl.ds(d, ROW_TILE)], sem.at[slot]).start()

    def wait_tile(slot):
        pltpu.make_async_copy(a_hbm.at[pl.ds(0, MOE_TM * ROW_TILE)], xg_ref.at[slot],
                              sem.at[slot]).wait()

    def clear(s, carry):
        src_ref[s] = 0
        return carry

    @pl.when(i == 0)
    def _():
        for e in range(N_EXPERTS):
            lax.fori_loop(off_ref[e] + cnt_ref[e], end_ref[e], clear, 0)
        lax.fori_loop(end_ref[N_EXPERTS - 1], n_slots, clear, 0)

        def fill(a, carry):
            src_ref[pos_ref[a]] = lax.shift_right_logical(a, 1)
            return carry

        lax.fori_loop(0, n_assign, fill, 0, unroll=16)

        def first(r, carry):
            start_row(0, 0, r)
            return carry

        lax.fori_loop(0, MOE_TM, first, 0, unroll=8)

    valid = tv_ref[i] > 0
    slot = i % 2
    prev = jnp.maximum(i - 1, 0)

    @pl.when(jnp.logical_or(i == 0, tv_ref[prev] > 0))
    def _():
        wait_tile(slot)

    nxt_tile = lax.rem(i + 1, n)

    def issue_rows(lo, hi):
        for r in range(lo, hi):
            start_row(nxt_tile, 1 - slot, r)

    group_start = jnp.logical_or(i == 0, te_ref[i] != te_ref[prev])

    @pl.when(jnp.logical_and(valid, group_start))
    def _():
        x = _load_row_tiles(xg_ref.at[slot], MOE_TM).astype(BF16)
        _expert_first_tile(te_ref[i], idx, x, issue_rows, w1_hbm, w3_hbm, w2_hbm,
                           w1_ref, w3_ref, w2_ref, st_col, st_row, wsem, t_ref, o_ref)

    @pl.when(jnp.logical_and(valid, jnp.logical_not(group_start)))
    def _():
        x = _load_row_tiles(xg_ref.at[slot], MOE_TM).astype(BF16)
        chunks = [(c0, min(c0 + FF_CHUNK, D_FF_EXPERT)) for c0 in range(0, D_FF_EXPERT, FF_CHUNK)]
        per_chunk = MOE_TM // len(chunks)
        for ci, (c0, c1) in enumerate(chunks):
            issue_rows(ci * per_chunk, (ci + 1) * per_chunk)
            g = jnp.dot(x, w1_ref[:, c0:c1], preferred_element_type=F32)
            u = jnp.dot(x, w3_ref[:, c0:c1], preferred_element_type=F32)
            t_ref[:, c0:c1] = (jax.nn.silu(g) * u).astype(BF16)
        y = jnp.dot(t_ref[...], w2_ref[...], preferred_element_type=F32)
        _store_row_tiles(o_ref, y)

    @pl.when(jnp.logical_not(valid))
    def _():
        o_ref[...] = jnp.zeros_like(o_ref)

    @pl.when(jnp.logical_and(valid, i == n - 1))
    def _():
        wait_tile(1 - slot)


def _experts(plan, a2r, w1, w3, w2, idx, n_rows, n_slots):
    n_tiles = n_slots // MOE_TM
    return pl.pallas_call(
        functools.partial(_experts_kernel, n_assign=2 * n_rows, n_slots=n_slots, idx=idx),
        grid_spec=pltpu.PrefetchScalarGridSpec(
            num_scalar_prefetch=6,
            grid=(n_tiles,),
            in_specs=[pl.BlockSpec(memory_space=pl.ANY)] * 4,
            out_specs=pl.BlockSpec((MOE_TM * ROW_TILE, LANES), lambda i, *_: (i, 0)),
            scratch_shapes=[
                pltpu.SMEM((n_slots,), I32),
                pltpu.VMEM((2, MOE_TM * ROW_TILE, LANES), F32),
                pltpu.VMEM((MOE_TM, D_FF_EXPERT), BF16),
                pltpu.VMEM((D, D_FF_EXPERT), BF16),
                pltpu.VMEM((D, D_FF_EXPERT), BF16),
                pltpu.VMEM((D_FF_EXPERT, D), BF16),
                pltpu.VMEM((W_RING, D, W_CHUNK), F32),
                pltpu.VMEM((W_RING, W_CHUNK, D), F32),
                pltpu.SemaphoreType.DMA((2,)),
                pltpu.SemaphoreType.DMA((2 * W_RING,)),
            ],
        ),
        out_shape=jax.ShapeDtypeStruct((n_slots * ROW_TILE, LANES), F32),
        compiler_params=_cparams(1, VMEM_LIMIT_EXPERTS),
        name="moe_experts",
    )(*plan, a2r, w1, w3, w2)


def _combine_kernel(pos_ref, y_hbm, h_ref, rw_ref, mod_ref, lng_ref, lnb_ref, o_ref, buf, sem):
    i = pl.program_id(0)
    n = pl.num_programs(0)

    def start_row(step, slot, r):
        d = r * ROW_TILE if isinstance(r, int) else pl.multiple_of(r * ROW_TILE, ROW_TILE)
        for k in range(2):
            p = pl.multiple_of(pos_ref[2 * (step * COMBINE_ROWS + r) + k] * ROW_TILE, ROW_TILE)
            pltpu.make_async_copy(y_hbm.at[pl.ds(p, ROW_TILE)],
                                  buf.at[slot, k, pl.ds(d, ROW_TILE)], sem.at[slot]).start()

    def wait_tile(slot):
        for k in range(2):
            pltpu.make_async_copy(y_hbm.at[pl.ds(0, COMBINE_ROWS * ROW_TILE)], buf.at[slot, k],
                                  sem.at[slot]).wait()

    ahead = COMBINE_SLOTS - 1

    @pl.when(i == 0)
    def _():
        for t in range(ahead):
            def first(r, carry):
                start_row(t, t, r)
                return carry

            lax.fori_loop(0, COMBINE_ROWS, first, 0, unroll=4)

    slot = lax.rem(i, COMBINE_SLOTS)
    wait_tile(slot)
    nxt = lax.rem(i + ahead, n)
    nxt_slot = lax.rem(i + ahead, COMBINE_SLOTS)
    chunk = COMBINE_ROWS // COMBINE_CHUNKS
    for c in range(COMBINE_CHUNKS):
        for r in range(c * chunk, (c + 1) * chunk):
            start_row(nxt, nxt_slot, r)
        rs = slice(c * chunk, (c + 1) * chunk)
        tiles = pl.ds(c * chunk * ROW_TILE, chunk * ROW_TILE)
        w = rw_ref[rs, :]
        f = (w[:, 0:1] * _load_row_tiles(buf.at[slot, 0, tiles], chunk)
             + w[:, 1:2] * _load_row_tiles(buf.at[slot, 1, tiles], chunk))
        o_ref[rs, :] = _post_norm(h_ref[rs, :], f, mod_ref[5:6, :], lng_ref[1:2, :],
                                  lnb_ref[1:2, :])

    @pl.when(i == n - 1)
    def _():
        for t in range(1, COMBINE_SLOTS):
            wait_tile(lax.rem(i + t, COMBINE_SLOTS))


def _combine(pos, y, h1, rw, mods, ln_g, ln_b, layer, n_rows):
    n_tiles = n_rows // COMBINE_ROWS
    tiles_per_batch = SEQ // COMBINE_ROWS
    n_lat = NL // COMBINE_ROWS
    return pl.pallas_call(
        _combine_kernel,
        grid_spec=pltpu.PrefetchScalarGridSpec(
            num_scalar_prefetch=1,
            grid=(n_tiles,),
            in_specs=[
                pl.BlockSpec(memory_space=pl.ANY),
                pl.BlockSpec((COMBINE_ROWS, D), lambda i, p: (i, 0)),
                pl.BlockSpec((COMBINE_ROWS, LANES), lambda i, p: (i, 0)),
                pl.BlockSpec((None, None, 6, D),
                             lambda i, p: (layer, _mod_row(i, tiles_per_batch, n_lat), 0, 0)),
                pl.BlockSpec((None, 2, D), lambda i, p: (layer, 0, 0)),
                pl.BlockSpec((None, 2, D), lambda i, p: (layer, 0, 0)),
            ],
            out_specs=pl.BlockSpec((COMBINE_ROWS, D), lambda i, p: (i, 0)),
            scratch_shapes=[pltpu.VMEM((COMBINE_SLOTS, 2, COMBINE_ROWS * ROW_TILE, LANES), F32),
                            pltpu.SemaphoreType.DMA((COMBINE_SLOTS,))],
        ),
        out_shape=jax.ShapeDtypeStruct((n_rows, D), F32),
        compiler_params=_cparams(1),
        name="moe_combine",
    )(pos, y, h1, rw, mods, ln_g, ln_b)


def _route_plan(eid, n_rows, n_slots):
    n_tiles = n_slots // MOE_TM
    e_flat = eid.reshape(-1)
    onehot = (e_flat[:, None] == jnp.arange(N_EXPERTS, dtype=I32)[None, :]).astype(I32)
    csum = jnp.cumsum(onehot, axis=0)
    rank = jnp.sum((csum - onehot) * onehot, axis=1)
    count = csum[-1]
    padded = ((count + MOE_TM - 1) // MOE_TM) * MOE_TM
    ends = jnp.cumsum(padded)
    offs = ends - padded
    pos = jnp.sum(onehot * offs[None, :], axis=1) + rank
    tile_start = jnp.arange(n_tiles, dtype=I32) * MOE_TM
    tile_e = jnp.minimum(jnp.sum((tile_start[:, None] >= ends[None, :]).astype(I32), axis=1),
                         N_EXPERTS - 1)
    tile_v = (tile_start < ends[-1]).astype(I32)
    return (pos.astype(I32), tile_e.astype(I32), tile_v, count.astype(I32), offs.astype(I32),
            ends.astype(I32))


def _moe(a2r, h1, ri, rw, mods, w1, w3, w2, ln_g, ln_b, layer, n_rows):
    n_slots = 2 * n_rows + N_EXPERTS * MOE_TM
    plan = _route_plan(ri[:, :2], n_rows, n_slots)
    ys = _experts(plan, a2r, w1, w3, w2, layer // 2, n_rows, n_slots)
    return _combine(plan[0], ys, h1, rw, mods, ln_g, ln_b, layer, n_rows)


def kernel(x, c, ctx, c_ctx, w_ada, b_ada, w_in, w_pool, pool_scale, sink, w_out, ln_g, ln_b,
           dense_w1, dense_w3, dense_w2, router, moe_w1, moe_w3, moe_w2):
    cond = jnp.concatenate([c, c_ctx[None, :], jnp.zeros((MOD_ROWS - B - 1, D), F32)], axis=0)
    mods = _ada_tables(cond, w_ada, b_ada)
    cos_t, sin_t = _rope_tables()

    w_in_b = w_in.astype(BF16)
    w_pool_b = w_pool.astype(BF16)
    w_out_b = w_out.astype(BF16)
    dw1, dw3, dw2 = dense_w1.astype(BF16), dense_w3.astype(BF16), dense_w2.astype(BF16)
    r_hi = router.astype(BF16)
    r_lo = (router - r_hi.astype(F32)).astype(BF16)
    lane_pad = ((0, 0), (0, 0), (0, LANES - N_EXPERTS))
    router_pad = jnp.concatenate([jnp.pad(r_hi, lane_pad), jnp.pad(r_lo, lane_pad)], axis=-1)
    pool_scale3 = pool_scale.reshape(DEPTH, 1, POOL_W)

    h = (x.reshape(NL, D), ctx.reshape(NC, D))
    for l in range(DEPTH):
        last = l == DEPTH - 1
        n_rows = NL if last else NT
        q, kk, vv, u = _inproj(h, mods, w_in_b, cos_t, sin_t, l)
        mix = _mixer_latent(q, kk, vv, u, sink[l], w_pool_b, pool_scale3, l)
        mix_c = mix if last else _mixer_context(q, kk, vv, u, sink[l], w_pool_b, pool_scale3, l)
        if l % 2 == 0:
            h1, a2 = _outproj(mix, mix_c, h, mods, w_out_b, ln_g, ln_b, None, l, n_rows)
            h = (_ffn_dense(a2, h1, mods, dw1, dw3, dw2, ln_g, ln_b, l, n_rows),)
        else:
            h1, a2, ri, rw = _outproj(mix, mix_c, h, mods, w_out_b, ln_g, ln_b, router_pad, l,
                                      n_rows)
            h = (_moe(a2, h1, ri, rw, mods, moe_w1, moe_w3, moe_w2, ln_g, ln_b, l, n_rows),)
    return h[0].reshape(B, SEQ, D)
```

```python
import functools

import jax
import jax.numpy as jnp
from jax import lax
from jax.experimental import pallas as pl
from jax.experimental.pallas import tpu as pltpu

F32 = jnp.float32
BF16 = jnp.bfloat16
I32 = jnp.int32

D = 1024
B = 8
SEQ = 2048
DEPTH = 4
CTX = 256
GRID_W = 64
HEAD_DIM = 64
N_Q_HEADS = 8
N_KV_HEADS = 2
ATTN_W = N_Q_HEADS * HEAD_DIM
KV_W = N_KV_HEADS * HEAD_DIM
POOL_WINDOWS = (2, 4, 8, 16)
POOL_W = D - ATTN_W
POOL_G = POOL_W // len(POOL_WINDOWS)
IN_W = ATTN_W + 2 * KV_W + POOL_W
WINDOW = 128
ROPE_BASE = 10000.0
ROPE_FREQS = HEAD_DIM // 4
D_FF_DENSE = 2816
N_EXPERTS = 8
D_FF_EXPERT = 3584
ALPHA = (2 * DEPTH) ** 0.25
LN_EPS = 1e-6
NEG_INF = -1e30
LOG2E = 1.4426950408889634

NL = B * SEQ
NC = B * CTX
NT = NL + NC
MOD_ROWS = 16
LANES = 128

TM = 1024
TM_FFN = 512
ROW_SPLIT_ROUTER = 512
TQ = 128
TQ_STEP = 1024
POOL_HALO = 8
ROW_SPLIT = 256
MOE_TM = 512
FF_CHUNK = 1024
W_CHUNK = 256
W_RING = 5
COMBINE_ROWS = 256
COMBINE_CHUNKS = 4
COMBINE_SLOTS = 3

VMEM_LIMIT = 56 * 1024 * 1024
VMEM_LIMIT_EXPERTS = 60 * 1024 * 1024


def _cparams(n_axes, vmem=VMEM_LIMIT):
    return pltpu.CompilerParams(
        dimension_semantics=("arbitrary",) * n_axes, vmem_limit_bytes=vmem)


def _layernorm(x):
    mu = jnp.mean(x, axis=-1, keepdims=True)
    xc = x - mu
    var = jnp.mean(xc * xc, axis=-1, keepdims=True)
    return xc * lax.rsqrt(var + LN_EPS)


def _mod_row(tile, tiles_per_batch, n_latent_tiles):
    return jnp.where(tile < n_latent_tiles, tile // tiles_per_batch, B)


def _stream_specs(h, n_lat):
    if len(h) == 1:
        return [pl.BlockSpec((TM, D), lambda i: (i, 0))]
    return [pl.BlockSpec((TM, D), lambda i: (jnp.minimum(i, n_lat - 1), 0)),
            pl.BlockSpec((TM, D), lambda i: (jnp.maximum(i - n_lat, 0), 0))]


ADA_TN = 1536


def _ada_kernel(s_ref, w_ref, b_ref, o_ref):
    s = jax.nn.silu(s_ref[...]).astype(BF16)
    w = w_ref[...].astype(BF16)
    o_ref[...] = jnp.dot(s, w, preferred_element_type=F32) + b_ref[...]


def _ada_tables(cond, w_ada, b_ada):
    n_col = (6 * D) // ADA_TN
    out = pl.pallas_call(
        _ada_kernel,
        grid=(DEPTH, n_col),
        in_specs=[
            pl.BlockSpec((MOD_ROWS, D), lambda l, j: (0, 0)),
            pl.BlockSpec((None, D, ADA_TN), lambda l, j: (l, 0, j)),
            pl.BlockSpec((None, 1, ADA_TN), lambda l, j: (l, 0, j)),
        ],
        out_specs=pl.BlockSpec((None, MOD_ROWS, ADA_TN), lambda l, j: (l, 0, j)),
        out_shape=jax.ShapeDtypeStruct((DEPTH, MOD_ROWS, 6 * D), F32),
        compiler_params=_cparams(2),
        name="ada_tables",
    )(cond, w_ada, b_ada.reshape(DEPTH, 1, 6 * D))
    return out.reshape(DEPTH, MOD_ROWS, 6, D)


def _inproj_kernel(*refs, n_lat, two_streams):
    if two_streams:
        hl_ref, hc_ref, mod_ref, w_ref, cos_ref, sin_ref, q_ref, kk_ref, vv_ref, u_ref = refs
    else:
        hl_ref, mod_ref, w_ref, cos_ref, sin_ref, q_ref, kk_ref, vv_ref, u_ref = refs
    lane = lax.broadcasted_iota(I32, (1, LANES), 1)
    first_half = (lane % (2 * ROPE_FREQS)) < ROPE_FREQS
    low = lane < HEAD_DIM
    for r0 in range(0, TM, ROW_SPLIT):
        rs = slice(r0, r0 + ROW_SPLIT)
        h = hl_ref[rs, :]
        if two_streams:
            h = jnp.where(pl.program_id(0) >= n_lat, hc_ref[rs, :], h)
        a = _layernorm(h) * (1.0 + mod_ref[1:2, :]) + mod_ref[0:1, :]
        p = jnp.dot(a.astype(BF16), w_ref[...], preferred_element_type=F32)
        cos = cos_ref[rs, :]
        sin = sin_ref[rs, :]

        def rope(x):
            swapped = jnp.where(first_half, pltpu.roll(x, LANES - ROPE_FREQS, 1),
                                pltpu.roll(x, ROPE_FREQS, 1))
            return x * cos + swapped * sin

        for c in range(ATTN_W // LANES):
            qc = rope(p[:, c * LANES:(c + 1) * LANES]) * (HEAD_DIM ** -0.5 * LOG2E)
            q_ref[rs, c * LANES:(c + 1) * LANES] = qc.astype(BF16)
        k = rope(p[:, ATTN_W:ATTN_W + KV_W])
        k_sw = pltpu.roll(k, HEAD_DIM, 1)
        kk_ref[rs, 0:LANES] = jnp.where(low, k, k_sw).astype(BF16)
        kk_ref[rs, LANES:2 * LANES] = jnp.where(low, k_sw, k).astype(BF16)
        v = p[:, ATTN_W + KV_W:ATTN_W + 2 * KV_W]
        v_sw = pltpu.roll(v, HEAD_DIM, 1)
        vv_ref[rs, 0:LANES] = jnp.where(low, v, v_sw).astype(BF16)
        vv_ref[rs, LANES:2 * LANES] = jnp.where(low, v_sw, v).astype(BF16)
        u_ref[rs, :] = p[:, ATTN_W + 2 * KV_W:]


def _inproj(h, mods, w_in, cos_t, sin_t, layer):
    n_tiles = NT // TM
    tiles_per_batch = SEQ // TM
    n_lat = NL // TM

    def rope_idx(i):
        return (jnp.where(i < n_lat, i % tiles_per_batch, tiles_per_batch), 0)

    return pl.pallas_call(
        functools.partial(_inproj_kernel, n_lat=n_lat, two_streams=len(h) == 2),
        grid=(n_tiles,),
        in_specs=_stream_specs(h, n_lat) + [
            pl.BlockSpec((None, None, 6, D),
                         lambda i: (layer, _mod_row(i, tiles_per_batch, n_lat), 0, 0)),
            pl.BlockSpec((None, D, IN_W), lambda i: (layer, 0, 0)),
            pl.BlockSpec((TM, LANES), rope_idx),
            pl.BlockSpec((TM, LANES), rope_idx),
        ],
        out_specs=[
            pl.BlockSpec((TM, ATTN_W), lambda i: (i, 0)),
            pl.BlockSpec((TM, 2 * KV_W), lambda i: (i, 0)),
            pl.BlockSpec((TM, 2 * KV_W), lambda i: (i, 0)),
            pl.BlockSpec((TM, POOL_W), lambda i: (i, 0)),
        ],
        out_shape=[
            jax.ShapeDtypeStruct((NT, ATTN_W), BF16),
            jax.ShapeDtypeStruct((NT, 2 * KV_W), BF16),
            jax.ShapeDtypeStruct((NT, 2 * KV_W), BF16),
            jax.ShapeDtypeStruct((NT, POOL_W), F32),
        ],
        compiler_params=_cparams(1),
        name="inproj",
    )(*h, mods, w_in, cos_t, sin_t)


def _rope_tables():
    rows = SEQ // GRID_W
    row = jnp.repeat(jnp.arange(rows, dtype=F32), GRID_W)
    col = jnp.tile(jnp.arange(GRID_W, dtype=F32), rows)
    inv = ROPE_BASE ** (-jnp.arange(ROPE_FREQS, dtype=F32) / ROPE_FREQS)
    ang_r = row[:, None] * inv[None, :]
    ang_c = col[:, None] * inv[None, :]
    cr, sr, cc, sc = jnp.cos(ang_r), jnp.sin(ang_r), jnp.cos(ang_c), jnp.sin(ang_c)
    cos_h = jnp.concatenate([cr, cr, cc, cc], axis=-1)
    sin_h = jnp.concatenate([-sr, sr, -sc, sc], axis=-1)
    cos_t = jnp.concatenate([jnp.tile(cos_h, (1, 2)), jnp.ones((TM, LANES), F32)], axis=0)
    sin_t = jnp.concatenate([jnp.tile(sin_h, (1, 2)), jnp.zeros((TM, LANES), F32)], axis=0)
    return cos_t, sin_t


def _pool_group(top, mid, bot, t0, seq_len, width):
    rows = mid.shape[0]
    slab = jnp.concatenate([top, mid, bot], axis=0)
    half = width // 2
    s = slab
    span = 1
    while span < width:
        n = s.shape[0] - span
        s = s[0:n] + s[span:span + n]
        span *= 2
    start = POOL_HALO - half
    total = s[start:start + rows]
    pos = t0 + lax.broadcasted_iota(I32, (rows, 1), 0)
    hi = jnp.minimum(pos + half, seq_len)
    lo = jnp.maximum(pos - half, 0)
    cnt = (hi - lo).astype(F32)
    return total / cnt - mid


def _mixer_kernel(sink_ref, q_ref, kk_ref, vv_ref, kkc_ref, vvc_ref, u_ref, wp_ref, ps_ref,
                  o_ref, *, tq, sub, seq_len, band):
    for sb in range(tq // sub):
        _mixer_block(sink_ref, q_ref, kk_ref, vv_ref, kkc_ref, vvc_ref, u_ref, wp_ref, ps_ref,
                     o_ref, t0=pl.program_id(1) * tq + sb * sub, r0=sb * sub, tq=sub,
                     seq_len=seq_len, band=band)


def _mixer_block(sink_ref, q_ref, kk_ref, vv_ref, kkc_ref, vvc_ref, u_ref, wp_ref, ps_ref,
                 o_ref, *, t0, r0, tq, seq_len, band):
    lane = lax.broadcasted_iota(I32, (1, LANES), 1)
    low = lane < HEAD_DIM

    if band:
        n_band = tq + 2 * WINDOW
        ks = jnp.clip(t0 - WINDOW, 0, seq_len - n_band)
        ks = pl.multiple_of(ks, LANES)
        kk = jnp.concatenate([kk_ref[pl.ds(ks, n_band), :], kkc_ref[...]], axis=0)
        vv = jnp.concatenate([vv_ref[pl.ds(ks, n_band), :], vvc_ref[...]], axis=0)
        qpos = t0 + lax.broadcasted_iota(I32, (tq, n_band), 0)
        kpos = ks + lax.broadcasted_iota(I32, (tq, n_band), 1)
        bias = jnp.where(jnp.abs(kpos - qpos) <= WINDOW, 0.0, NEG_INF)
    else:
        n_band = 0
        kk = kkc_ref[...]
        vv = vvc_ref[...]
        bias = None
    nk = kk.shape[0]
    zero = jnp.zeros_like(kk[:, 0:LANES])
    low_f = jnp.broadcast_to(jnp.where(low, 1.0, 0.0), (nk, LANES))
    ones_st = jnp.concatenate([low_f, 1.0 - low_f], axis=0).astype(BF16)

    for g in range(N_KV_HEADS):
        kg = kk[:, g * LANES:(g + 1) * LANES]
        vg = vv[:, g * LANES:(g + 1) * LANES]
        k_st = jnp.concatenate([jnp.where(low, kg, zero), jnp.where(low, zero, kg)], axis=0)
        v_st = jnp.concatenate([jnp.where(low, vg, zero), jnp.where(low, zero, vg)], axis=0)
        v_st = jnp.concatenate([v_st, ones_st], axis=1)
        for pr in range(2):
            j = 2 * g + pr
            qp = q_ref[r0:r0 + tq, j * LANES:(j + 1) * LANES]
            s = lax.dot_general(qp, k_st, (((1,), (1,)), ((), ())), preferred_element_type=F32)
            ps = []
            sink_terms = []
            for hh in range(2):
                sink = sink_ref[2 * j + hh] * LOG2E
                sh = s[:, hh * nk:(hh + 1) * nk]
                if bias is not None:
                    parts = [sh[:, :n_band] + bias, sh[:, n_band:]]
                else:
                    parts = [sh]
                m = sink
                for part in parts:
                    m = jnp.maximum(jnp.max(part, axis=-1, keepdims=True), m)
                sink_terms.append(jnp.exp2(sink - m))
                ps += [jnp.exp2(part - m).astype(BF16) for part in parts]
            od = jnp.dot(jnp.concatenate(ps, axis=1), v_st, preferred_element_type=F32)
            den = od[:, LANES:] + jnp.where(low, sink_terms[0], sink_terms[1])
            o_ref[r0:r0 + tq, j * LANES:(j + 1) * LANES] = (od[:, :LANES] / den).astype(BF16)

    no_halo = jnp.zeros((POOL_HALO, POOL_W), F32)
    if band:
        t0a = pl.multiple_of(t0, POOL_HALO)
        mid = u_ref[pl.ds(t0a, tq), :]
        top_s = pl.multiple_of(jnp.maximum(t0 - POOL_HALO, 0), POOL_HALO)
        bot_s = pl.multiple_of(jnp.minimum(t0 + tq, seq_len - POOL_HALO), POOL_HALO)
        top = jnp.where(t0 > 0, u_ref[pl.ds(top_s, POOL_HALO), :], no_halo)
        bot = jnp.where(t0 + tq < seq_len, u_ref[pl.ds(bot_s, POOL_HALO), :], no_halo)
    else:
        mid = u_ref[...]
        top = no_halo
        bot = no_halo
    for g, width in enumerate(POOL_WINDOWS):
        sl = slice(g * POOL_G, (g + 1) * POOL_G)
        diff = _pool_group(top[:, sl], mid[:, sl], bot[:, sl], t0, seq_len, width)
        mixed = jnp.dot(diff.astype(BF16), wp_ref[g], preferred_element_type=F32)
        mixed = mixed * ps_ref[:, sl]
        o_ref[r0:r0 + tq, ATTN_W + g * POOL_G:ATTN_W + (g + 1) * POOL_G] = mixed.astype(BF16)


def _mixer_latent(q, kk, vv, u, sink_l, w_pool, pool_scale, layer):
    nq = SEQ // TQ_STEP
    ctx_blk = NL // CTX
    kern = functools.partial(_mixer_kernel, tq=TQ_STEP, sub=TQ, seq_len=SEQ, band=True)
    return pl.pallas_call(
        kern,
        grid=(B, nq),
        in_specs=[
            pl.BlockSpec(memory_space=pltpu.SMEM),
            pl.BlockSpec((TQ_STEP, ATTN_W), lambda b, t: (b * nq + t, 0)),
            pl.BlockSpec((SEQ, 2 * KV_W), lambda b, t: (b, 0)),
            pl.BlockSpec((SEQ, 2 * KV_W), lambda b, t: (b, 0)),
            pl.BlockSpec((CTX, 2 * KV_W), lambda b, t: (ctx_blk + b, 0)),
            pl.BlockSpec((CTX, 2 * KV_W), lambda b, t: (ctx_blk + b, 0)),
            pl.BlockSpec((SEQ, POOL_W), lambda b, t: (b, 0)),
            pl.BlockSpec((None, len(POOL_WINDOWS), POOL_G, POOL_G), lambda b, t: (layer, 0, 0, 0)),
            pl.BlockSpec((None, 1, POOL_W), lambda b, t: (layer, 0, 0)),
        ],
        out_specs=pl.BlockSpec((TQ_STEP, D), lambda b, t: (b * nq + t, 0)),
        out_shape=jax.ShapeDtypeStruct((NL, D), BF16),
        compiler_params=_cparams(2),
        name="mixer_latent",
    )(sink_l, q, kk, vv, kk, vv, u, w_pool, pool_scale)


def _mixer_context(q, kk, vv, u, sink_l, w_pool, pool_scale, layer):
    ctx_blk = NL // CTX
    kern = functools.partial(_mixer_kernel, tq=CTX, sub=CTX, seq_len=CTX, band=False)

    def body(sink_ref, q_ref, kkc_ref, vvc_ref, u_ref, wp_ref, ps_ref, o_ref):
        kern(sink_ref, q_ref, None, None, kkc_ref, vvc_ref, u_ref, wp_ref, ps_ref, o_ref)

    return pl.pallas_call(
        body,
        grid=(B, 1),
        in_specs=[
            pl.BlockSpec(memory_space=pltpu.SMEM),
            pl.BlockSpec((CTX, ATTN_W), lambda b, t: (ctx_blk + b, 0)),
            pl.BlockSpec((CTX, 2 * KV_W), lambda b, t: (ctx_blk + b, 0)),
            pl.BlockSpec((CTX, 2 * KV_W), lambda b, t: (ctx_blk + b, 0)),
            pl.BlockSpec((CTX, POOL_W), lambda b, t: (ctx_blk + b, 0)),
            pl.BlockSpec((None, len(POOL_WINDOWS), POOL_G, POOL_G), lambda b, t: (layer, 0, 0, 0)),
            pl.BlockSpec((None, 1, POOL_W), lambda b, t: (layer, 0, 0)),
        ],
        out_specs=pl.BlockSpec((CTX, D), lambda b, t: (b, 0)),
        out_shape=jax.ShapeDtypeStruct((NC, D), BF16),
        compiler_params=_cparams(2),
        name="mixer_context",
    )(sink_l, q, kk, vv, u, w_pool, pool_scale)


def _post_norm(h, upd, gate, g, b):
    return _layernorm(ALPHA * h + gate * upd) * g + b


ROW_TILE = D // LANES


def _store_row_tiles(ref, x):
    rows = x.shape[0]
    for c in range(ROW_TILE):
        ref[pl.ds(c, rows, stride=ROW_TILE), :] = x[:, c * LANES:(c + 1) * LANES]


def _load_row_tiles(ref, rows):
    return jnp.concatenate(
        [ref[pl.ds(c, rows, stride=ROW_TILE), :] for c in range(ROW_TILE)], axis=1)


def _route_top2(a, router_ref, ri_ref, rw_ref):
    a_hi = a.astype(BF16)
    a_lo = (a - a_hi.astype(F32)).astype(BF16)
    t_hi = jnp.dot(a_hi, router_ref[...], preferred_element_type=F32)
    t_lo = jnp.dot(a_lo, router_ref[:, :LANES], preferred_element_type=F32)
    logits = t_hi[:, :LANES] + t_hi[:, LANES:] + t_lo
    lane = lax.broadcasted_iota(I32, logits.shape, 1)
    lane_f = lane.astype(F32)
    logits = jnp.where(lane < N_EXPERTS, logits, -jnp.inf)
    m1 = jnp.max(logits, axis=-1, keepdims=True)
    i1 = jnp.min(jnp.where(logits == m1, lane_f, float(LANES)), axis=-1, keepdims=True)
    rest = jnp.where(lane_f == i1, -jnp.inf, logits)
    m2 = jnp.max(rest, axis=-1, keepdims=True)
    i2 = jnp.min(jnp.where(rest == m2, lane_f, float(LANES)), axis=-1, keepdims=True)
    e = jnp.exp(m2 - m1)
    w1 = 1.0 / (1.0 + e)
    w2 = e / (1.0 + e)
    ri_ref[...] = jnp.where(lane == 0, i1, jnp.where(lane == 1, i2, 0.0)).astype(I32)
    rw_ref[...] = jnp.where(lane == 0, w1, jnp.where(lane == 1, w2, 0.0))


def _outproj_kernel(mixl_ref, mixc_ref, *rest, moe, n_lat, has_ctx, two_streams):
    if two_streams:
        hl_ref, hc_ref, mod_ref, w_ref, lng_ref, lnb_ref, *rest = rest
    else:
        hl_ref, mod_ref, w_ref, lng_ref, lnb_ref, *rest = rest
    if moe:
        router_ref, h1_ref, a2_ref, ri_ref, rw_ref = rest
    else:
        h1_ref, a2_ref = rest
    split = ROW_SPLIT_ROUTER if moe else ROW_SPLIT
    for r0 in range(0, TM, split):
        rs = slice(r0, r0 + split)
        mix = mixl_ref[rs, :]
        if has_ctx:
            mix = jnp.where(pl.program_id(0) >= n_lat, mixc_ref[rs, :], mix)
        y = jnp.dot(mix, w_ref[...], preferred_element_type=F32)
        h = hl_ref[rs, :]
        if two_streams:
            h = jnp.where(pl.program_id(0) >= n_lat, hc_ref[rs, :], h)
        h1 = _post_norm(h, y, mod_ref[2:3, :], lng_ref[0:1, :], lnb_ref[0:1, :])
        h1_ref[rs, :] = h1
        a2 = _layernorm(h1) * (1.0 + mod_ref[4:5, :]) + mod_ref[3:4, :]
        if moe:
            _store_row_tiles(a2_ref.at[pl.ds(r0 * ROW_TILE, split * ROW_TILE)], a2)
            _route_top2(a2, router_ref, ri_ref.at[pl.ds(r0, split)], rw_ref.at[pl.ds(r0, split)])
        else:
            a2_ref[rs, :] = a2.astype(BF16)


def _outproj(mix_lat, mix_ctx, h, mods, w_out, ln_g, ln_b, router_pad, layer, n_rows):
    n_tiles = n_rows // TM
    tiles_per_batch = SEQ // TM
    n_lat = NL // TM
    moe = router_pad is not None
    has_ctx = n_rows > NL
    in_specs = [
        pl.BlockSpec((TM, D), lambda i: (jnp.minimum(i, n_lat - 1), 0)),
        pl.BlockSpec((TM, D), lambda i: (jnp.maximum(i - n_lat, 0), 0)),
    ] + _stream_specs(h, n_lat) + [
        pl.BlockSpec((None, None, 6, D),
                     lambda i: (layer, _mod_row(i, tiles_per_batch, n_lat), 0, 0)),
        pl.BlockSpec((None, D, D), lambda i: (layer, 0, 0)),
        pl.BlockSpec((None, 2, D), lambda i: (layer, 0, 0)),
        pl.BlockSpec((None, 2, D), lambda i: (layer, 0, 0)),
    ]
    out_specs = [pl.BlockSpec((TM, D), lambda i: (i, 0)), pl.BlockSpec((TM, D), lambda i: (i, 0))]
    out_shape = [jax.ShapeDtypeStruct((n_rows, D), F32), jax.ShapeDtypeStruct((n_rows, D), BF16)]
    args = [mix_lat, mix_ctx, *h, mods, w_out, ln_g, ln_b]
    if moe:
        in_specs.append(pl.BlockSpec((None, D, 2 * LANES), lambda i: (layer // 2, 0, 0)))
        out_specs[1] = pl.BlockSpec((TM * ROW_TILE, LANES), lambda i: (i, 0))
        out_shape[1] = jax.ShapeDtypeStruct((n_rows * ROW_TILE, LANES), F32)
        out_specs += [pl.BlockSpec((TM, LANES), lambda i: (i, 0)),
                      pl.BlockSpec((TM, LANES), lambda i: (i, 0))]
        out_shape += [jax.ShapeDtypeStruct((n_rows, LANES), I32),
                      jax.ShapeDtypeStruct((n_rows, LANES), F32)]
        args.append(router_pad)
    return pl.pallas_call(
        functools.partial(_outproj_kernel, moe=moe, n_lat=n_lat, has_ctx=has_ctx,
                          two_streams=len(h) == 2),
        grid=(n_tiles,),
        in_specs=in_specs,
        out_specs=out_specs,
        out_shape=out_shape,
        compiler_params=_cparams(1),
        name="outproj_moe" if moe else "outproj",
    )(*args)


def _swiglu_hidden(x, w1_ref, w3_ref, t_ref, d_ff):
    for c0 in range(0, d_ff, FF_CHUNK):
        c1 = min(c0 + FF_CHUNK, d_ff)
        g = jnp.dot(x, w1_ref[:, c0:c1], preferred_element_type=F32)
        u = jnp.dot(x, w3_ref[:, c0:c1], preferred_element_type=F32)
        t_ref[:, c0:c1] = (jax.nn.silu(g) * u).astype(BF16)


def _ffn_dense_kernel(a_ref, h_ref, mod_ref, w1_ref, w3_ref, w2_ref, lng_ref, lnb_ref, o_ref,
                      t_ref):
    _swiglu_hidden(a_ref[...], w1_ref, w3_ref, t_ref, D_FF_DENSE)
    f = jnp.dot(t_ref[...], w2_ref[...], preferred_element_type=F32)
    o_ref[...] = _post_norm(h_ref[...], f, mod_ref[5:6, :], lng_ref[1:2, :], lnb_ref[1:2, :])


def _ffn_dense(a2, h1, mods, w1, w3, w2, ln_g, ln_b, layer, n_rows):
    n_tiles = n_rows // TM_FFN
    tiles_per_batch = SEQ // TM_FFN
    n_lat = NL // TM_FFN
    idx = layer // 2
    resident = pl.Buffered(1)
    return pl.pallas_call(
        _ffn_dense_kernel,
        grid=(n_tiles,),
        in_specs=[
            pl.BlockSpec((TM_FFN, D), lambda i: (i, 0)),
            pl.BlockSpec((TM_FFN, D), lambda i: (i, 0)),
            pl.BlockSpec((None, None, 6, D),
                         lambda i: (layer, _mod_row(i, tiles_per_batch, n_lat), 0, 0)),
            pl.BlockSpec((None, D, D_FF_DENSE), lambda i: (idx, 0, 0), pipeline_mode=resident),
            pl.BlockSpec((None, D, D_FF_DENSE), lambda i: (idx, 0, 0), pipeline_mode=resident),
            pl.BlockSpec((None, D_FF_DENSE, D), lambda i: (idx, 0, 0), pipeline_mode=resident),
            pl.BlockSpec((None, 2, D), lambda i: (layer, 0, 0)),
            pl.BlockSpec((None, 2, D), lambda i: (layer, 0, 0)),
        ],
        out_specs=pl.BlockSpec((TM_FFN, D), lambda i: (i, 0)),
        out_shape=jax.ShapeDtypeStruct((n_rows, D), F32),
        scratch_shapes=[pltpu.VMEM((TM_FFN, D_FF_DENSE), BF16)],
        compiler_params=_cparams(1),
        name="ffn_dense",
    )(a2, h1, mods, w1, w3, w2, ln_g, ln_b)


def _expert_first_tile(e, idx, x, issue_rows, w1_hbm, w3_hbm, w2_hbm, w1_ref, w3_ref, w2_ref,
                       st_col, st_row, wsem, t_ref, o_ref):
    n_c = D_FF_EXPERT // W_CHUNK
    jobs = []
    for c in range(n_c):
        jobs += [(True, w1_hbm, w1_ref, c), (True, w3_hbm, w3_ref, c)]
    jobs += [(False, w2_hbm, w2_ref, c) for c in range(n_c)]
    n_col = 2 * n_c

    def ring_slot(j):
        return j % W_RING if j < n_col else (j - n_col) % W_RING

    def copy(j):
        by_col, src, _, c = jobs[j]
        s = ring_slot(j)
        if by_col:
            return pltpu.make_async_copy(src.at[idx, e, :, pl.ds(c * W_CHUNK, W_CHUNK)],
                                         st_col.at[s], wsem.at[s])
        return pltpu.make_async_copy(src.at[idx, e, pl.ds(c * W_CHUNK, W_CHUNK), :],
                                     st_row.at[s], wsem.at[W_RING + s])

    def consume(j):
        if j + W_RING - 1 < len(jobs):
            copy(j + W_RING - 1).start()
        copy(j).wait()
        by_col, _, dst, c = jobs[j]
        if by_col:
            dst[:, c * W_CHUNK:(c + 1) * W_CHUNK] = st_col[ring_slot(j)].astype(BF16)
        else:
            dst[c * W_CHUNK:(c + 1) * W_CHUNK, :] = st_row[ring_slot(j)].astype(BF16)

    for j in range(W_RING - 1):
        copy(j).start()
    for c in range(n_c):
        issue_rows(c * MOE_TM // n_c, (c + 1) * MOE_TM // n_c)
        consume(2 * c)
        consume(2 * c + 1)
        cs = slice(c * W_CHUNK, (c + 1) * W_CHUNK)
        g = jnp.dot(x, w1_ref[:, cs], preferred_element_type=F32)
        u = jnp.dot(x, w3_ref[:, cs], preferred_element_type=F32)
        t_ref[:, cs] = (jax.nn.silu(g) * u).astype(BF16)
    y = None
    for c0 in range(0, n_c, 2):
        consume(n_col + c0)
        consume(n_col + c0 + 1)
        ks = slice(c0 * W_CHUNK, (c0 + 2) * W_CHUNK)
        part = jnp.dot(t_ref[:, ks], w2_ref[ks, :], preferred_element_type=F32)
        y = part if y is None else y + part
    _store_row_tiles(o_ref, y)


def _experts_kernel(pos_ref, te_ref, tv_ref, cnt_ref, off_ref, end_ref,
                    a_hbm, w1_hbm, w3_hbm, w2_hbm, o_ref, src_ref, xg_ref, t_ref,
                    w1_ref, w3_ref, w2_ref, st_col, st_row, sem, wsem,
                    *, n_assign, n_slots, idx):
    i = pl.program_id(0)
    n = pl.num_programs(0)

    def start_row(tile, slot, r):
        s = pl.multiple_of(src_ref[tile * MOE_TM + r] * ROW_TILE, ROW_TILE)
        d = r * ROW_TILE if isinstance(r, int) else pl.multiple_of(r * ROW_TILE, ROW_TILE)
        prio = r % 2 if isinstance(r, int) else 0
        pltpu.make_async_copy(a_hbm.at[pl.ds(s, ROW_TILE)],
                              xg_ref.at[slot, pl.ds(d, ROW_TILE)], sem.at[slot]).start(priority=prio)

    def wait_tile(slot):
        pltpu.make_async_copy(a_hbm.at[pl.ds(0, MOE_TM * ROW_TILE)], xg_ref.at[slot],
                              sem.at[slot]).wait()

    def clear(s, carry):
        src_ref[s] = 0
        return carry

    @pl.when(i == 0)
    def _():
        for e in range(N_EXPERTS):
            lax.fori_loop(off_ref[e] + cnt_ref[e], end_ref[e], clear, 0)
        lax.fori_loop(end_ref[N_EXPERTS - 1], n_slots, clear, 0)

        def fill(a, carry):
            src_ref[pos_ref[a]] = lax.shift_right_logical(a, 1)
            return carry

        lax.fori_loop(0, n_assign, fill, 0, unroll=16)

        def first(r, carry):
            start_row(0, 0, r)
            return carry

        lax.fori_loop(0, MOE_TM, first, 0, unroll=8)

    valid = tv_ref[i] > 0
    slot = i % 2
    prev = jnp.maximum(i - 1, 0)

    @pl.when(jnp.logical_or(i == 0, tv_ref[prev] > 0))
    def _():
        wait_tile(slot)

    nxt_tile = lax.rem(i + 1, n)

    def issue_rows(lo, hi):
        for r in range(lo, hi):
            start_row(nxt_tile, 1 - slot, r)

    group_start = jnp.logical_or(i == 0, te_ref[i] != te_ref[prev])

    @pl.when(jnp.logical_and(valid, group_start))
    def _():
        x = _load_row_tiles(xg_ref.at[slot], MOE_TM).astype(BF16)
        _expert_first_tile(te_ref[i], idx, x, issue_rows, w1_hbm, w3_hbm, w2_hbm,
                           w1_ref, w3_ref, w2_ref, st_col, st_row, wsem, t_ref, o_ref)

    @pl.when(jnp.logical_and(valid, jnp.logical_not(group_start)))
    def _():
        x = _load_row_tiles(xg_ref.at[slot], MOE_TM).astype(BF16)
        chunks = [(c0, min(c0 + FF_CHUNK, D_FF_EXPERT)) for c0 in range(0, D_FF_EXPERT, FF_CHUNK)]
        per_chunk = MOE_TM // len(chunks)
        for ci, (c0, c1) in enumerate(chunks):
            issue_rows(ci * per_chunk, (ci + 1) * per_chunk)
            g = jnp.dot(x, w1_ref[:, c0:c1], preferred_element_type=F32)
            u = jnp.dot(x, w3_ref[:, c0:c1], preferred_element_type=F32)
            t_ref[:, c0:c1] = (jax.nn.silu(g) * u).astype(BF16)
        y = jnp.dot(t_ref[...], w2_ref[...], preferred_element_type=F32)
        _store_row_tiles(o_ref, y)

    @pl.when(jnp.logical_not(valid))
    def _():
        o_ref[...] = jnp.zeros_like(o_ref)

    @pl.when(jnp.logical_and(valid, i == n - 1))
    def _():
        wait_tile(1 - slot)


def _experts(plan, a2r, w1, w3, w2, idx, n_rows, n_slots):
    n_tiles = n_slots // MOE_TM
    return pl.pallas_call(
        functools.partial(_experts_kernel, n_assign=2 * n_rows, n_slots=n_slots, idx=idx),
        grid_spec=pltpu.PrefetchScalarGridSpec(
            num_scalar_prefetch=6,
            grid=(n_tiles,),
            in_specs=[pl.BlockSpec(memory_space=pl.ANY)] * 4,
            out_specs=pl.BlockSpec((MOE_TM * ROW_TILE, LANES), lambda i, *_: (i, 0)),
            scratch_shapes=[
                pltpu.SMEM((n_slots,), I32),
                pltpu.VMEM((2, MOE_TM * ROW_TILE, LANES), F32),
                pltpu.VMEM((MOE_TM, D_FF_EXPERT), BF16),
                pltpu.VMEM((D, D_FF_EXPERT), BF16),
                pltpu.VMEM((D, D_FF_EXPERT), BF16),
                pltpu.VMEM((D_FF_EXPERT, D), BF16),
                pltpu.VMEM((W_RING, D, W_CHUNK), F32),
                pltpu.VMEM((W_RING, W_CHUNK, D), F32),
                pltpu.SemaphoreType.DMA((2,)),
                pltpu.SemaphoreType.DMA((2 * W_RING,)),
            ],
        ),
        out_shape=jax.ShapeDtypeStruct((n_slots * ROW_TILE, LANES), F32),
        compiler_params=_cparams(1, VMEM_LIMIT_EXPERTS),
        name="moe_experts",
    )(*plan, a2r, w1, w3, w2)


def _combine_kernel(pos_ref, y_hbm, h_ref, rw_ref, mod_ref, lng_ref, lnb_ref, o_ref, buf, sem):
    i = pl.program_id(0)
    n = pl.num_programs(0)

    def start_row(step, slot, r):
        d = r * ROW_TILE if isinstance(r, int) else pl.multiple_of(r * ROW_TILE, ROW_TILE)
        for k in range(2):
            p = pl.multiple_of(pos_ref[2 * (step * COMBINE_ROWS + r) + k] * ROW_TILE, ROW_TILE)
            prio = k if isinstance(r, int) else 0
            pltpu.make_async_copy(y_hbm.at[pl.ds(p, ROW_TILE)],
                                  buf.at[slot, k, pl.ds(d, ROW_TILE)],
                                  sem.at[slot]).start(priority=prio)

    def wait_tile(slot):
        for k in range(2):
            pltpu.make_async_copy(y_hbm.at[pl.ds(0, COMBINE_ROWS * ROW_TILE)], buf.at[slot, k],
                                  sem.at[slot]).wait()

    ahead = COMBINE_SLOTS - 1

    @pl.when(i == 0)
    def _():
        for t in range(ahead):
            def first(r, carry):
                start_row(t, t, r)
                return carry

            lax.fori_loop(0, COMBINE_ROWS, first, 0, unroll=4)

    slot = lax.rem(i, COMBINE_SLOTS)
    wait_tile(slot)
    nxt = lax.rem(i + ahead, n)
    nxt_slot = lax.rem(i + ahead, COMBINE_SLOTS)
    chunk = COMBINE_ROWS // COMBINE_CHUNKS
    for c in range(COMBINE_CHUNKS):
        for r in range(c * chunk, (c + 1) * chunk):
            start_row(nxt, nxt_slot, r)
        rs = slice(c * chunk, (c + 1) * chunk)
        tiles = pl.ds(c * chunk * ROW_TILE, chunk * ROW_TILE)
        w = rw_ref[rs, :]
        f = (w[:, 0:1] * _load_row_tiles(buf.at[slot, 0, tiles], chunk)
             + w[:, 1:2] * _load_row_tiles(buf.at[slot, 1, tiles], chunk))
        o_ref[rs, :] = _post_norm(h_ref[rs, :], f, mod_ref[5:6, :], lng_ref[1:2, :],
                                  lnb_ref[1:2, :])

    @pl.when(i == n - 1)
    def _():
        for t in range(1, COMBINE_SLOTS):
            wait_tile(lax.rem(i + t, COMBINE_SLOTS))


def _combine(pos, y, h1, rw, mods, ln_g, ln_b, layer, n_rows):
    n_tiles = n_rows // COMBINE_ROWS
    tiles_per_batch = SEQ // COMBINE_ROWS
    n_lat = NL // COMBINE_ROWS
    return pl.pallas_call(
        _combine_kernel,
        grid_spec=pltpu.PrefetchScalarGridSpec(
            num_scalar_prefetch=1,
            grid=(n_tiles,),
            in_specs=[
                pl.BlockSpec(memory_space=pl.ANY),
                pl.BlockSpec((COMBINE_ROWS, D), lambda i, p: (i, 0)),
                pl.BlockSpec((COMBINE_ROWS, LANES), lambda i, p: (i, 0)),
                pl.BlockSpec((None, None, 6, D),
                             lambda i, p: (layer, _mod_row(i, tiles_per_batch, n_lat), 0, 0)),
                pl.BlockSpec((None, 2, D), lambda i, p: (layer, 0, 0)),
                pl.BlockSpec((None, 2, D), lambda i, p: (layer, 0, 0)),
            ],
            out_specs=pl.BlockSpec((COMBINE_ROWS, D), lambda i, p: (i, 0)),
            scratch_shapes=[pltpu.VMEM((COMBINE_SLOTS, 2, COMBINE_ROWS * ROW_TILE, LANES), F32),
                            pltpu.SemaphoreType.DMA((COMBINE_SLOTS,))],
        ),
        out_shape=jax.ShapeDtypeStruct((n_rows, D), F32),
        compiler_params=_cparams(1),
        name="moe_combine",
    )(pos, y, h1, rw, mods, ln_g, ln_b)


def _route_plan(eid, n_rows, n_slots):
    n_tiles = n_slots // MOE_TM
    e_flat = eid.reshape(-1)
    onehot = (e_flat[:, None] == jnp.arange(N_EXPERTS, dtype=I32)[None, :]).astype(I32)
    csum = jnp.cumsum(onehot, axis=0)
    rank = jnp.sum((csum - onehot) * onehot, axis=1)
    count = csum[-1]
    padded = ((count + MOE_TM - 1) // MOE_TM) * MOE_TM
    ends = jnp.cumsum(padded)
    offs = ends - padded
    pos = jnp.sum(onehot * offs[None, :], axis=1) + rank
    tile_start = jnp.arange(n_tiles, dtype=I32) * MOE_TM
    tile_e = jnp.minimum(jnp.sum((tile_start[:, None] >= ends[None, :]).astype(I32), axis=1),
                         N_EXPERTS - 1)
    tile_v = (tile_start < ends[-1]).astype(I32)
    return (pos.astype(I32), tile_e.astype(I32), tile_v, count.astype(I32), offs.astype(I32),
            ends.astype(I32))


def _moe(a2r, h1, ri, rw, mods, w1, w3, w2, ln_g, ln_b, layer, n_rows):
    n_slots = 2 * n_rows + N_EXPERTS * MOE_TM
    plan = _route_plan(ri[:, :2], n_rows, n_slots)
    ys = _experts(plan, a2r, w1, w3, w2, layer // 2, n_rows, n_slots)
    return _combine(plan[0], ys, h1, rw, mods, ln_g, ln_b, layer, n_rows)


def kernel(x, c, ctx, c_ctx, w_ada, b_ada, w_in, w_pool, pool_scale, sink, w_out, ln_g, ln_b,
           dense_w1, dense_w3, dense_w2, router, moe_w1, moe_w3, moe_w2):
    cond = jnp.concatenate([c, c_ctx[None, :], jnp.zeros((MOD_ROWS - B - 1, D), F32)], axis=0)
    mods = _ada_tables(cond, w_ada, b_ada)
    cos_t, sin_t = _rope_tables()

    w_in_b = w_in.astype(BF16)
    w_pool_b = w_pool.astype(BF16)
    w_out_b = w_out.astype(BF16)
    dw1, dw3, dw2 = dense_w1.astype(BF16), dense_w3.astype(BF16), dense_w2.astype(BF16)
    r_hi = router.astype(BF16)
    r_lo = (router - r_hi.astype(F32)).astype(BF16)
    lane_pad = ((0, 0), (0, 0), (0, LANES - N_EXPERTS))
    router_pad = jnp.concatenate([jnp.pad(r_hi, lane_pad), jnp.pad(r_lo, lane_pad)], axis=-1)
    pool_scale3 = pool_scale.reshape(DEPTH, 1, POOL_W)

    h = (x.reshape(NL, D), ctx.reshape(NC, D))
    for l in range(DEPTH):
        last = l == DEPTH - 1
        n_rows = NL if last else NT
        q, kk, vv, u = _inproj(h, mods, w_in_b, cos_t, sin_t, l)
        mix = _mixer_latent(q, kk, vv, u, sink[l], w_pool_b, pool_scale3, l)
        mix_c = mix if last else _mixer_context(q, kk, vv, u, sink[l], w_pool_b, pool_scale3, l)
        if l % 2 == 0:
            h1, a2 = _outproj(mix, mix_c, h, mods, w_out_b, ln_g, ln_b, None, l, n_rows)
            h = (_ffn_dense(a2, h1, mods, dw1, dw3, dw2, ln_g, ln_b, l, n_rows),)
        else:
            h1, a2, ri, rw = _outproj(mix, mix_c, h, mods, w_out_b, ln_g, ln_b, router_pad, l,
                                      n_rows)
            h = (_moe(a2, h1, ri, rw, mods, moe_w1, moe_w3, moe_w2, ln_g, ln_b, l, n_rows),)
    return h[0].reshape(B, SEQ, D)
```
